```python
import math
import jax, jax.numpy as jnp
from jax import lax
import numpy as np

D_MODEL = 1024
BATCH = 16
SEQ = 2048
DEPTH = 1
DEC_BATCH = 4
DEC_SEQ = 4096
PAST_LEN = 128

GRID_W = 64
MIX_WIDTH = D_MODEL
ATTN_WIDTH = D_MODEL // 2
HYENA_WIDTH = MIX_WIDTH - ATTN_WIDTH
HEAD_DIM = 64
N_ATTN_HEADS = ATTN_WIDTH // HEAD_DIM
HYENA_GROUP = 64
N_HYENA_GROUPS = HYENA_WIDTH // HYENA_GROUP
NA_ROWS_MAX = 8
NA_COLS = 16
HYENA_ORDER = 2
N_FILTERS = HYENA_ORDER - 1
FILTER_EMB = 33
FILTER_HIDDEN = 64
FILTER_INNER = 2
DECAY_TARGET = 1e-2
FAST_DECAY_PCT = 0.3
SLOW_DECAY_PCT = 1.5
MAX_DECAY = math.log(DECAY_TARGET) / FAST_DECAY_PCT
MIN_DECAY = math.log(DECAY_TARGET) / SLOW_DECAY_PCT
SHORT_CONV = 3
D_FF = 2816
FFN_CONV = 3
ALPHA = (2 * DEPTH) ** 0.25
BETA = (8 * DEPTH) ** -0.25
LN_EPS = 1e-5
RMS_EPS = 1e-6

kernel_name = "hybrid_natten_hyena_encoder"


def _layer_norm(x, g, b):
    x32 = x.astype(jnp.float32)
    mu = jnp.mean(x32, -1, keepdims=True)
    var = jnp.mean(jnp.square(x32 - mu), -1, keepdims=True)
    y = (x32 - mu) * lax.rsqrt(var + LN_EPS)
    return (y * g.astype(jnp.float32) + b.astype(jnp.float32)).astype(x.dtype)


def _rms_norm(x, g):
    x32 = x.astype(jnp.float32)
    y = x32 * lax.rsqrt(jnp.mean(jnp.square(x32), -1, keepdims=True) + RMS_EPS)
    return (y * g.astype(jnp.float32)).astype(x.dtype)


def _dwconv3(u, w, b):
    up = jnp.pad(u, ((0, 0), (1, 1), (0, 0)))
    return up[:, :-2] * w[0] + up[:, 1:-1] * w[1] + up[:, 2:] * w[2] + b


def _neighbourhood_attention(q, k, v, rpb):
    bsz, seq_len, _ = q.shape
    rows = seq_len // GRID_W
    kr = min(NA_ROWS_MAX, rows)
    shp = (bsz, rows, GRID_W, N_ATTN_HEADS, HEAD_DIM)
    q = q.reshape(shp) * (HEAD_DIM ** -0.5)
    k = k.reshape(shp)
    v = v.reshape(shp)
    r = np.arange(rows)
    row_idx = np.clip(r - kr // 2, 0, rows - kr)[:, None] + np.arange(kr)[None, :]
    c = np.arange(GRID_W)
    col_start = np.clip(c - NA_COLS // 2, 0, GRID_W - NA_COLS)
    col_mask = (c[None, :] >= col_start[:, None]) & (c[None, :] < col_start[:, None] + NA_COLS)
    dr = (row_idx - r[:, None]) + (NA_ROWS_MAX - 1)
    dc = np.clip(c[None, :] - c[:, None], -(NA_COLS - 1), NA_COLS - 1) + (NA_COLS - 1)
    bias = rpb[:, dr[:, None, :, None], dc[None, :, None, :]]
    k_blk = jnp.take(k, row_idx, axis=1)
    v_blk = jnp.take(v, row_idx, axis=1)
    s = jnp.einsum("brqhd,brikhd->bhrqik", q, k_blk).astype(jnp.float32)
    s = jnp.where(col_mask[None, None, None, :, None, :], s + bias.astype(jnp.float32)[None], -jnp.inf)
    p = jax.nn.softmax(s.reshape(s.shape[:4] + (kr * GRID_W,)), axis=-1).reshape(s.shape)
    o = jnp.einsum("bhrqik,brikhd->brqhd", p.astype(v.dtype), v_blk)
    return o.reshape(bsz, seq_len, N_ATTN_HEADS * HEAD_DIM)


def _hyena_filter_spectrum(seq_len, w1, b1, freq, w_inner, b_inner, w3):
    f32 = jnp.float32
    w1, b1, freq, w_inner, b_inner, w3 = (a.astype(f32) for a in (w1, b1, freq, w_inner, b_inner, w3))
    t = jnp.linspace(0.0, 1.0, seq_len, dtype=f32)[:, None]
    bands = (FILTER_EMB - 1) // 2
    freqs = jnp.linspace(1e-4, bands - 1, bands, dtype=f32)
    ang = (2.0 * math.pi * jnp.arange(seq_len, dtype=f32) / seq_len)[:, None] * freqs[None, :]
    z = jnp.concatenate([t, jnp.cos(ang), -jnp.sin(ang)], axis=-1)
    h = jnp.sin(freq * (z @ w1 + b1))
    for i in range(FILTER_INNER):
        h = jnp.sin(freq * (h @ w_inner[i] + b_inner[i]))
    h = (h @ w3).reshape(seq_len, 2, N_FILTERS, HYENA_WIDTH)
    deltas = jnp.abs(jnp.linspace(MIN_DECAY, MAX_DECAY, HYENA_WIDTH, dtype=f32))
    decay = jnp.exp(-t * deltas[None, :])
    h = h * decay[:, None, None, :]
    h_fwd, h_bwd = h[:, 0], h[:, 1]
    kern = jnp.concatenate([h_fwd, jnp.zeros_like(h_fwd[:1]), h_bwd[:0:-1]], axis=0)
    return jnp.fft.rfft(kern, axis=0)


def _hyena_mixer(u3, short_w, short_b, w1, b1, freq, w_inner, b_inner, w3, fbias):
    seq_len = u3.shape[1]
    uc = _dwconv3(u3, short_w, short_b)
    gates = [uc[..., o * HYENA_WIDTH:(o + 1) * HYENA_WIDTH].astype(jnp.float32) for o in range(HYENA_ORDER)]
    z = uc[..., HYENA_ORDER * HYENA_WIDTH:].astype(jnp.float32)
    kf = _hyena_filter_spectrum(seq_len, w1, b1, freq, w_inner, b_inner, w3)
    fb = fbias.astype(jnp.float32)
    for o, gate in enumerate(reversed(gates[1:])):
        z = z * gate
        zf = jnp.fft.rfft(z, n=2 * seq_len, axis=1)
        z = jnp.fft.irfft(zf * kf[:, o][None], n=2 * seq_len, axis=1)[:, :seq_len] + z * fb[o]
    return (z * gates[0]).astype(u3.dtype)


def _encoder_layer(x, w_in, short_w, short_b, rpb, filt_w1, filt_b1, filt_freq, filt_w_inner,
                   filt_b_inner, filt_w3, filt_bias, g_attn, g_hyena, w_out, ln1_g, ln1_b,
                   ffn_w_in, ffn_conv_w, ffn_conv_b, ffn_w_out, ln2_g, ln2_b):
    proj = jnp.einsum("bld,de->ble", x, w_in)
    q = proj[..., :ATTN_WIDTH]
    k = proj[..., ATTN_WIDTH:2 * ATTN_WIDTH]
    v = proj[..., 2 * ATTN_WIDTH:3 * ATTN_WIDTH]
    attn = _neighbourhood_attention(q, k, v, rpb)
    hy = _hyena_mixer(proj[..., 3 * ATTN_WIDTH:], short_w, short_b, filt_w1, filt_b1, filt_freq,
                      filt_w_inner, filt_b_inner, filt_w3, filt_bias)
    merged = jnp.concatenate([_rms_norm(attn, g_attn), _rms_norm(hy, g_hyena)], axis=-1)
    x = _layer_norm(ALPHA * x + merged @ w_out, ln1_g, ln1_b)
    u = _dwconv3(x @ ffn_w_in, ffn_conv_w, ffn_conv_b)
    hid = u[..., :D_FF] * jax.nn.gelu(u[..., D_FF:], approximate=False)
    x = _layer_norm(ALPHA * x + hid @ ffn_w_out, ln2_g, ln2_b)
    return x


def _trunk(x, params):
    for layer in range(DEPTH):
        x = _encoder_layer(x, *[p[layer] for p in params])
    return x


def setup_inputs(seed: int = 0) -> dict:
    key = jax.random.key(seed)
    ks = jax.random.split(key, 26)
    f32 = jnp.float32

    def nrm(k, shape, scale):
        return jax.random.normal(k, shape, f32) * scale

    n_proj = 3 * ATTN_WIDTH + (HYENA_ORDER + 1) * HYENA_WIDTH
    n_hy = (HYENA_ORDER + 1) * HYENA_WIDTH
    return {
        "x_prompt": nrm(ks[0], (BATCH, SEQ, D_MODEL), 1.0),
        "x_sample": nrm(ks[1], (DEC_BATCH, DEC_SEQ, D_MODEL), 1.0),
        "w_in": nrm(ks[2], (DEPTH, D_MODEL, n_proj), D_MODEL ** -0.5),
        "short_w": nrm(ks[3], (DEPTH, SHORT_CONV, n_hy), SHORT_CONV ** -0.5),
        "short_b": nrm(ks[4], (DEPTH, n_hy), 0.02),
        "rpb": nrm(ks[5], (DEPTH, N_ATTN_HEADS, 2 * NA_ROWS_MAX - 1, 2 * NA_COLS - 1), 0.1),
        "filt_w1": nrm(ks[6], (DEPTH, FILTER_EMB, FILTER_HIDDEN), FILTER_EMB ** -0.5),
        "filt_b1": nrm(ks[7], (DEPTH, FILTER_HIDDEN), 0.1),
        "filt_freq": 1.0 + nrm(ks[8], (DEPTH, FILTER_HIDDEN), 0.05),
        "filt_w_inner": nrm(ks[9], (DEPTH, FILTER_INNER, FILTER_HIDDEN, FILTER_HIDDEN), FILTER_HIDDEN ** -0.5),
        "filt_b_inner": nrm(ks[10], (DEPTH, FILTER_INNER, FILTER_HIDDEN), 0.1),
        "filt_w3": nrm(ks[11], (DEPTH, FILTER_HIDDEN, 2 * N_FILTERS * HYENA_WIDTH), FILTER_HIDDEN ** -0.5),
        "filt_bias": nrm(ks[12], (DEPTH, N_FILTERS, HYENA_WIDTH), 1.0),
        "g_attn": 1.0 + nrm(ks[13], (DEPTH, ATTN_WIDTH), 0.02),
        "g_hyena": 1.0 + nrm(ks[14], (DEPTH, HYENA_WIDTH), 0.02),
        "w_out": nrm(ks[15], (DEPTH, MIX_WIDTH, D_MODEL), MIX_WIDTH ** -0.5 * BETA),
        "ln1_g": 1.0 + nrm(ks[16], (DEPTH, D_MODEL), 0.02),
        "ln1_b": nrm(ks[17], (DEPTH, D_MODEL), 0.02),
        "ffn_w_in": nrm(ks[18], (DEPTH, D_MODEL, 2 * D_FF), D_MODEL ** -0.5),
        "ffn_conv_w": nrm(ks[19], (DEPTH, FFN_CONV, 2 * D_FF), FFN_CONV ** -0.5),
        "ffn_conv_b": nrm(ks[20], (DEPTH, 2 * D_FF), 0.02),
        "ffn_w_out": nrm(ks[21], (DEPTH, D_FF, D_MODEL), D_FF ** -0.5 * BETA),
        "ln2_g": 1.0 + nrm(ks[22], (DEPTH, D_MODEL), 0.02),
        "ln2_b": nrm(ks[23], (DEPTH, D_MODEL), 0.02),
    }


def reference(x_prompt, x_sample, w_in, short_w, short_b, rpb, filt_w1, filt_b1, filt_freq,
              filt_w_inner, filt_b_inner, filt_w3, filt_bias, g_attn, g_hyena, w_out, ln1_g, ln1_b,
              ffn_w_in, ffn_conv_w, ffn_conv_b, ffn_w_out, ln2_g, ln2_b):
    params = (w_in, short_w, short_b, rpb, filt_w1, filt_b1, filt_freq, filt_w_inner, filt_b_inner,
              filt_w3, filt_bias, g_attn, g_hyena, w_out, ln1_g, ln1_b, ffn_w_in, ffn_conv_w,
              ffn_conv_b, ffn_w_out, ln2_g, ln2_b)
    y_prompt = _trunk(x_prompt, params)
    y_sample = _trunk(x_sample, params)
    return (y_prompt, y_sample)
```

```python
import functools
import math

import numpy as np
import jax
import jax.numpy as jnp
from jax import lax
from jax.experimental import pallas as pl
from jax.experimental.pallas import tpu as pltpu

F32 = jnp.float32
BF16 = jnp.bfloat16

D_MODEL = 1024
GRID_W = 64
ATTN_WIDTH = 512
HYENA_WIDTH = 512
HEAD_DIM = 64
N_HEADS = ATTN_WIDTH // HEAD_DIM
NA_ROWS = 8
NA_COLS = 16
FILTER_EMB = 33
FILTER_BANDS = (FILTER_EMB - 1) // 2
FILTER_HIDDEN = 64
FILTER_INNER = 2
MAX_DECAY = math.log(1e-2) / 0.3
MIN_DECAY = math.log(1e-2) / 1.5
D_FF = 2816
DEPTH = 1
ALPHA = (2 * DEPTH) ** 0.25
LN_EPS = 1e-5
RMS_EPS = 1e-6

SUBLANES = 8
LANES = 128
HALO = SUBLANES
VMEM_LIMIT = 56 * 1024 * 1024

HIGHEST = lax.Precision.HIGHEST


def _params(sem):
    return pltpu.CompilerParams(dimension_semantics=sem, vmem_limit_bytes=VMEM_LIMIT)


def _resident(shape):
    nd = len(shape)
    return pl.BlockSpec(shape, lambda *_: (0,) * nd, pipeline_mode=pl.Buffered(1))


def _layer_norm(y, g, b):
    mu = jnp.mean(y, axis=-1, keepdims=True)
    yc = y - mu
    var = jnp.mean(yc * yc, axis=-1, keepdims=True)
    return yc * lax.rsqrt(var + LN_EPS) * g + b


def _rms_norm(y, g):
    ms = jnp.mean(y * y, axis=-1, keepdims=True)
    return y * lax.rsqrt(ms + RMS_EPS) * g


def _halo_block(xp_ref, x_ref, xn_ref):
    i = pl.program_id(1)
    last = pl.num_programs(1) - 1
    prev = jnp.where(i > 0, xp_ref[0], 0.0)
    nxt = jnp.where(i < last, xn_ref[0], 0.0)
    return jnp.concatenate([prev, x_ref[0], nxt], axis=0).astype(BF16)


def _dwconv3(u_ref, w_ref, b_ref, tl):
    up = u_ref[pl.ds(HALO - 1, tl), :]
    uc = u_ref[pl.ds(HALO, tl), :]
    un = u_ref[pl.ds(HALO + 1, tl), :]
    return up * w_ref[0:1, :] + uc * w_ref[1:2, :] + un * w_ref[2:3, :] + b_ref[...]


def _halo_specs(tl, seq, d):
    nb = tl // HALO
    last = seq // HALO - 1
    return [
        pl.BlockSpec((1, HALO, d), lambda b, i: (b, jnp.maximum(i * nb - 1, 0), 0)),
        pl.BlockSpec((1, tl, d), lambda b, i: (b, i, 0)),
        pl.BlockSpec((1, HALO, d), lambda b, i: (b, jnp.minimum((i + 1) * nb, last), 0)),
    ]


def _in_proj_kernel(xp_ref, x_ref, xn_ref, w_ref, sw_ref, sb_ref,
                    q_ref, k_ref, v_ref, zg_ref, g0_ref, u_ref, *, tl):
    xq = x_ref[0].astype(BF16)
    for n, ref in enumerate((q_ref, k_ref, v_ref)):
        y = jnp.dot(xq, w_ref[:, n * ATTN_WIDTH:(n + 1) * ATTN_WIDTH], preferred_element_type=F32)
        if n == 0:
            y = y * (HEAD_DIM ** -0.5)
        ref[0] = y.astype(BF16)
    xh = _halo_block(xp_ref, x_ref, xn_ref)
    conv = []
    for n in range(3):
        lo = 3 * ATTN_WIDTH + n * HYENA_WIDTH
        u_ref[...] = jnp.dot(xh, w_ref[:, lo:lo + HYENA_WIDTH], preferred_element_type=F32)
        conv.append(_dwconv3(u_ref, sw_ref.at[n], sb_ref.at[n], tl))
    g0_ref[0] = conv[0].astype(BF16)
    zg_ref[0] = (conv[2] * conv[1]).astype(BF16)


def _in_proj(x, w_in, short_w, short_b, tl):
    bsz, seq, d = x.shape
    sw = short_w.reshape(3, 3, HYENA_WIDTH).transpose(1, 0, 2)
    sb = short_b.reshape(3, 1, HYENA_WIDTH)
    out = jax.ShapeDtypeStruct((bsz, seq, ATTN_WIDTH), BF16)
    ospec = pl.BlockSpec((1, tl, ATTN_WIDTH), lambda b, i: (b, i, 0))
    return pl.pallas_call(
        functools.partial(_in_proj_kernel, tl=tl),
        out_shape=(out,) * 5,
        grid=(bsz, seq // tl),
        in_specs=_halo_specs(tl, seq, d) + [_resident(w_in.shape), _resident(sw.shape), _resident(sb.shape)],
        out_specs=(ospec,) * 5,
        scratch_shapes=[pltpu.VMEM((tl + 2 * HALO, HYENA_WIDTH), F32)],
        compiler_params=_params(("parallel", "parallel")),
        name="in_proj",
    )(x, x, x, w_in, sw, sb)


def _attn_kernel(q_ref, k_ref, v_ref, bias_ref, g_ref, o_ref, *, rows, rb):
    jb = pl.program_id(1)
    lane = lax.broadcasted_iota(jnp.int32, (GRID_W, LANES), 1)
    first = lane < HEAD_DIM
    keep = (first.astype(F32).astype(BF16), (~first).astype(F32).astype(BF16))
    kw = NA_ROWS * GRID_W

    def row_body(j, carry):
        r = jb * rb + j
        start = jnp.clip(r - NA_ROWS // 2, 0, rows - NA_ROWS)
        cls = r - start
        qoff = pl.multiple_of(j * GRID_W, GRID_W)
        koff = pl.multiple_of(start * GRID_W, GRID_W)
        outs = []
        for p in range(N_HEADS // 2):
            sl = slice(p * LANES, (p + 1) * LANES)
            qp = q_ref[0, pl.ds(qoff, GRID_W), sl]
            kp = k_ref[0, pl.ds(koff, kw), sl]
            vp = v_ref[0, pl.ds(koff, kw), sl]
            res = []
            for hh in range(2):
                s = lax.dot_general(qp * keep[hh], kp, (((1,), (1,)), ((), ())),
                                    preferred_element_type=F32)
                s = s + bias_ref[2 * p + hh, cls]
                m = jnp.max(s, axis=-1, keepdims=True)
                e = jnp.exp(s - m)
                den = jnp.sum(e, axis=-1, keepdims=True)
                pv = jnp.dot(e.astype(BF16), vp, preferred_element_type=F32)
                res.append(pv / den)
            outs.append(jnp.where(first, res[0], res[1]))
        o = jnp.concatenate(outs, axis=1)
        o_ref[0, pl.ds(qoff, GRID_W), :] = _rms_norm(o, g_ref[...]).astype(BF16)
        return carry

    lax.fori_loop(0, rb, row_body, 0)


def _attention_bias(rpb):
    c = np.arange(GRID_W)
    col_start = np.clip(c - NA_COLS // 2, 0, GRID_W - NA_COLS)
    col_mask = (c[None, :] >= col_start[:, None]) & (c[None, :] < col_start[:, None] + NA_COLS)
    dc = np.clip(c[None, :] - c[:, None], -(NA_COLS - 1), NA_COLS - 1) + (NA_COLS - 1)
    dr = np.arange(NA_ROWS)[None, :] - np.arange(NA_ROWS)[:, None] + (NA_ROWS - 1)
    bias = rpb[:, dr[:, None, :, None], dc[None, :, None, :]]
    bias = jnp.where(col_mask[None, None, :, None, :], bias.astype(F32), -jnp.inf)
    return bias.reshape(N_HEADS, NA_ROWS, GRID_W, NA_ROWS * GRID_W)


def _attention(q, k, v, bias, g_attn, rb):
    bsz, seq, w = q.shape
    rows = seq // GRID_W
    tq = rb * GRID_W
    return pl.pallas_call(
        functools.partial(_attn_kernel, rows=rows, rb=rb),
        out_shape=jax.ShapeDtypeStruct((bsz, seq, w), BF16),
        grid=(bsz, rows // rb),
        in_specs=[
            pl.BlockSpec((1, tq, w), lambda b, i: (b, i, 0)),
            pl.BlockSpec((1, seq, w), lambda b, i: (b, 0, 0)),
            pl.BlockSpec((1, seq, w), lambda b, i: (b, 0, 0)),
            _resident(bias.shape),
            _resident(g_attn.shape),
        ],
        out_specs=pl.BlockSpec((1, tq, w), lambda b, i: (b, i, 0)),
        compiler_params=_params(("parallel", "arbitrary")),
        name="attention",
    )(q, k, v, bias, g_attn)


def _filter_kernel(w1t_ref, w1c_ref, w1s_ref, b1_ref, fq_ref, wi_ref, bi_ref, w3_ref,
                   a_ref, d_ref, *, seq, tl):
    i = pl.program_id(0)
    row = lax.broadcasted_iota(jnp.int32, (tl, LANES), 0) + i * tl
    band = lax.broadcasted_iota(jnp.int32, (tl, LANES), 1)
    freqs = jnp.where(band < FILTER_BANDS,
                      1e-4 + band.astype(F32) * ((FILTER_BANDS - 1 - 1e-4) / (FILTER_BANDS - 1)), 0.0)
    chan = lax.broadcasted_iota(jnp.int32, (tl, HYENA_WIDTH), 1).astype(F32)
    deltas = jnp.abs(MIN_DECAY + chan * ((MAX_DECAY - MIN_DECAY) / (HYENA_WIDTH - 1)))
    fq = fq_ref[...]

    def taps(pos, cols):
        posf = pos.astype(F32)
        t = posf[:, 0:1] * (1.0 / (seq - 1))
        ang = posf * (2.0 * math.pi / seq) * freqs
        pre = (t * w1t_ref[...]
               + jnp.dot(jnp.cos(ang), w1c_ref[...], precision=HIGHEST, preferred_element_type=F32)
               - jnp.dot(jnp.sin(ang), w1s_ref[...], precision=HIGHEST, preferred_element_type=F32)
               + b1_ref[...])
        h = jnp.sin(fq * pre)
        for n in range(FILTER_INNER):
            h = jnp.sin(fq * (jnp.dot(h, wi_ref[n], precision=HIGHEST, preferred_element_type=F32)
                              + bi_ref[n]))
        out = jnp.dot(h, w3_ref[:, cols], precision=HIGHEST, preferred_element_type=F32)
        return out * jnp.exp(-t * deltas)

    h_fwd = taps(row, slice(0, HYENA_WIDTH))
    h_bwd = taps(row + 1, slice(HYENA_WIDTH, 2 * HYENA_WIDTH))
    h_bwd = jnp.where(row[:, 0:1] + 1 < seq, h_bwd, 0.0)
    a_ref[...] = (h_fwd + h_bwd).astype(BF16)
    d_ref[...] = (h_bwd - h_fwd).astype(BF16)


def _filter_taps(seq, w1, b1, freq, w_inner, b_inner, w3, tl=512):
    pad = LANES - FILTER_BANDS
    w1t = w1[0:1]
    w1c = jnp.pad(w1[1:1 + FILTER_BANDS], ((0, pad), (0, 0)))
    w1s = jnp.pad(w1[1 + FILTER_BANDS:], ((0, pad), (0, 0)))
    args = (w1t, w1c, w1s, b1.reshape(1, -1), freq.reshape(1, -1), w_inner,
            b_inner.reshape(FILTER_INNER, 1, -1), w3)
    out = jax.ShapeDtypeStruct((seq, HYENA_WIDTH), BF16)
    ospec = pl.BlockSpec((tl, HYENA_WIDTH), lambda i: (i, 0))
    return pl.pallas_call(
        functools.partial(_filter_kernel, seq=seq, tl=tl),
        out_shape=(out, out),
        grid=(seq // tl,),
        in_specs=[_resident(a.shape) for a in args],
        out_specs=(ospec, ospec),
        compiler_params=_params(("parallel",)),
        name="hyena_filter",
    )(*args)


def _dft_kernel(cs_ref, ss_ref, cphi_ref, sphi_ref, ca_ref, sa_ref, cb_ref, sb_ref, *, seq, tf):
    i = pl.program_id(0)
    n2 = 4 * seq

    @pl.when(i == 0)
    def _():
        s_odd = 2 * lax.broadcasted_iota(jnp.int32, (tf, seq), 1) + 1
        f_odd = 2 * lax.broadcasted_iota(jnp.int32, (tf, seq), 0) + 1
        beta = ((s_odd * f_odd) & (2 * n2 - 1)).astype(F32) * (math.pi / n2)
        cb_ref[...] = jnp.cos(beta)
        sb_ref[...] = jnp.sin(beta)
        nblk = seq // tf
        s_odd = 2 * lax.broadcasted_iota(jnp.int32, (nblk, seq), 1) + 1
        f_hi = 2 * tf * lax.broadcasted_iota(jnp.int32, (nblk, seq), 0)
        alpha = ((s_odd * f_hi) & (2 * n2 - 1)).astype(F32) * (math.pi / n2)
        ca_ref[...] = jnp.cos(alpha)
        sa_ref[...] = jnp.sin(alpha)

    ca = ca_ref[pl.ds(i, 1), :]
    sa = sa_ref[pl.ds(i, 1), :]
    cb = cb_ref[...]
    sb = sb_ref[...]
    cs_ref[...] = (ca * cb - sa * sb).astype(BF16)
    ss_ref[...] = (sa * cb + ca * sb).astype(BF16)
    f_odd = 2 * (lax.broadcasted_iota(jnp.int32, (tf, LANES), 0) + i * tf) + 1
    phi = f_odd.astype(F32) * (math.pi / n2)
    cphi_ref[...] = jnp.cos(phi)
    sphi_ref[...] = jnp.sin(phi)


def _dft_matrices(seq, tf=128):
    mat = jax.ShapeDtypeStruct((seq, seq), BF16)
    vec = jax.ShapeDtypeStruct((seq, LANES), F32)
    return pl.pallas_call(
        functools.partial(_dft_kernel, seq=seq, tf=tf),
        out_shape=(mat, mat, vec, vec),
        grid=(seq // tf,),
        out_specs=(pl.BlockSpec((tf, seq), lambda i: (i, 0)), pl.BlockSpec((tf, seq), lambda i: (i, 0)),
                   pl.BlockSpec((tf, LANES), lambda i: (i, 0)), pl.BlockSpec((tf, LANES), lambda i: (i, 0))),
        scratch_shapes=[pltpu.VMEM((seq // tf, seq), F32), pltpu.VMEM((seq // tf, seq), F32),
                        pltpu.VMEM((tf, seq), F32), pltpu.VMEM((tf, seq), F32)],
        compiler_params=_params(("arbitrary",)),
        name="dft_matrices",
    )()


def _spectrum_kernel(cs_ref, ss_ref, a_ref, d_ref, cphi_ref, sphi_ref, gr_ref, gi_ref, *, seq):
    kr = jnp.dot(cs_ref[...], a_ref[...], preferred_element_type=F32)
    ki = jnp.dot(ss_ref[...], d_ref[...], preferred_element_type=F32)
    reps = HYENA_WIDTH // LANES
    c = pltpu.repeat(cphi_ref[...], reps, axis=1) * (1.0 / seq)
    s = pltpu.repeat(sphi_ref[...], reps, axis=1) * (1.0 / seq)
    gr_ref[...] = c * kr - s * ki
    gi_ref[...] = c * ki + s * kr


def _filter_spectrum(cs, ss, a, d, cphi, sphi, tf):
    seq = cs.shape[0]
    out = jax.ShapeDtypeStruct((seq, HYENA_WIDTH), F32)
    row = lambda w: pl.BlockSpec((tf, w), lambda i: (i, 0))
    return pl.pallas_call(
        functools.partial(_spectrum_kernel, seq=seq),
        out_shape=(out, out),
        grid=(seq // tf,),
        in_specs=[row(seq), row(seq), _resident(a.shape), _resident(d.shape), row(LANES), row(LANES)],
        out_specs=(row(HYENA_WIDTH), row(HYENA_WIDTH)),
        compiler_params=_params(("parallel",)),
        name="hyena_spectrum",
    )(cs, ss, a, d, cphi, sphi)


def _hy_fwd_kernel(cs_ref, ss_ref, z_ref, gr_ref, gi_ref, yr_ref, yi_ref):
    z = z_ref[0]
    zr = jnp.dot(cs_ref[...], z, preferred_element_type=F32)
    wi = jnp.dot(ss_ref[...], z, preferred_element_type=F32)
    gr = gr_ref[...]
    gi = gi_ref[...]
    yr_ref[0] = (gr * zr + gi * wi).astype(BF16)
    yi_ref[0] = (gr * wi - gi * zr).astype(BF16)


def _hy_inv_kernel(cs_ref, ss_ref, yr_ref, yi_ref, z_ref, g0_ref, fb_ref, g_ref, o_ref):
    y = (jnp.dot(cs_ref[...], yr_ref[0], preferred_element_type=F32)
         + jnp.dot(ss_ref[...], yi_ref[0], preferred_element_type=F32))
    y = (y + z_ref[0].astype(F32) * fb_ref[...]) * g0_ref[0].astype(F32)
    o_ref[0] = _rms_norm(y, g_ref[...]).astype(BF16)


def _hyena_conv(zg, g0, cs, ss, gr, gi, fbias, g_hyena, tf):
    bsz, seq, w = zg.shape
    grid = (seq // tf, bsz)
    mat = pl.BlockSpec((tf, seq), lambda i, b: (i, 0))
    full = pl.BlockSpec((1, seq, w), lambda i, b: (b, 0, 0))
    tile = pl.BlockSpec((1, tf, w), lambda i, b: (b, i, 0))
    spec = pl.BlockSpec((tf, w), lambda i, b: (i, 0))
    act = jax.ShapeDtypeStruct((bsz, seq, w), BF16)
    yr, yi = pl.pallas_call(
        _hy_fwd_kernel,
        out_shape=(act, act),
        grid=grid,
        in_specs=[mat, mat, full, spec, spec],
        out_specs=(tile, tile),
        compiler_params=_params(("parallel", "parallel")),
        name="hyena_fwd",
    )(cs, ss, zg, gr, gi)
    return pl.pallas_call(
        _hy_inv_kernel,
        out_shape=act,
        grid=grid,
        in_specs=[mat, mat, full, full, tile, tile, _resident(fbias.shape), _resident(g_hyena.shape)],
        out_specs=tile,
        compiler_params=_params(("parallel", "parallel")),
        name="hyena_inv",
    )(cs, ss, yr, yi, zg, g0, fbias, g_hyena)


def _merge_kernel(ma_ref, mh_ref, x_ref, wa_ref, wh_ref, g_ref, b_ref, o_ref):
    y = (jnp.dot(ma_ref[0], wa_ref[...], preferred_element_type=F32)
         + jnp.dot(mh_ref[0], wh_ref[...], preferred_element_type=F32))
    o_ref[0] = _layer_norm(ALPHA * x_ref[0] + y, g_ref[...], b_ref[...])


def _merge(ma, mh, x, w_out, g, b, tl):
    bsz, seq, d = x.shape
    wa, wh = w_out[:ATTN_WIDTH], w_out[ATTN_WIDTH:]
    half = pl.BlockSpec((1, tl, ATTN_WIDTH), lambda bb, i: (bb, i, 0))
    full = pl.BlockSpec((1, tl, d), lambda bb, i: (bb, i, 0))
    return pl.pallas_call(
        _merge_kernel,
        out_shape=jax.ShapeDtypeStruct(x.shape, F32),
        grid=(bsz, seq // tl),
        in_specs=[half, half, full, _resident(wa.shape), _resident(wh.shape),
                  _resident(g.shape), _resident(b.shape)],
        out_specs=full,
        compiler_params=_params(("parallel", "parallel")),
        name="merge_ln1",
    )(ma, mh, x, wa, wh, g, b)


FF_CHUNK = 256
N_FF_CHUNKS = D_FF // FF_CHUNK


def _ffn_kernel(xp_ref, x_ref, xn_ref, wi_ref, cw_ref, cb_ref, wo_ref, g_ref, b_ref,
                o_ref, u_ref, hid_ref, *, tl):
    xh = _halo_block(xp_ref, x_ref, xn_ref)
    for j in range(N_FF_CHUNKS):
        conv = []
        for part in range(2):
            n = part * N_FF_CHUNKS + j
            u_ref[...] = jnp.dot(xh, wi_ref[n], preferred_element_type=F32)
            conv.append(_dwconv3(u_ref, cw_ref.at[n], cb_ref.at[n], tl))
        gate = conv[1]
        gelu = 0.5 * gate * (1.0 + lax.erf(gate * (2.0 ** -0.5)))
        hid_ref[:, j * FF_CHUNK:(j + 1) * FF_CHUNK] = (conv[0] * gelu).astype(BF16)
    y = jnp.dot(hid_ref[...], wo_ref[...], preferred_element_type=F32)
    o_ref[0] = _layer_norm(ALPHA * x_ref[0] + y, g_ref[...], b_ref[...])


def _ffn(x1, w_in, conv_w, conv_b, w_out, g, b, tl):
    bsz, seq, d = x1.shape
    nch = 2 * N_FF_CHUNKS
    wi = w_in.reshape(d, nch, FF_CHUNK).transpose(1, 0, 2)
    cw = conv_w.reshape(3, nch, FF_CHUNK).transpose(1, 0, 2)
    cb = conv_b.reshape(nch, 1, FF_CHUNK)
    return pl.pallas_call(
        functools.partial(_ffn_kernel, tl=tl),
        out_shape=jax.ShapeDtypeStruct(x1.shape, F32),
        grid=(bsz, seq // tl),
        in_specs=_halo_specs(tl, seq, d) + [_resident(wi.shape), _resident(cw.shape), _resident(cb.shape),
                                       _resident(w_out.shape), _resident(g.shape), _resident(b.shape)],
        out_specs=pl.BlockSpec((1, tl, d), lambda bb, i: (bb, i, 0)),
        scratch_shapes=[pltpu.VMEM((tl + 2 * HALO, FF_CHUNK), F32), pltpu.VMEM((tl, D_FF), BF16)],
        compiler_params=_params(("parallel", "parallel")),
        name="conv_ffn",
    )(x1, x1, x1, wi, cw, cb, w_out, g, b)


def _encoder_layer(x, p, tl=512, rb=8, tf=512):
    seq = x.shape[1]
    q, k, v, zg, g0 = _in_proj(x, p["w_in"], p["short_w"], p["short_b"], tl)
    ma = _attention(q, k, v, p["bias"], p["g_attn"], rb)
    a, d = _filter_taps(seq, p["filt_w1"], p["filt_b1"], p["filt_freq"], p["filt_w_inner"],
                        p["filt_b_inner"], p["filt_w3"])
    cs, ss, cphi, sphi = _dft_matrices(seq)
    gr, gi = _filter_spectrum(cs, ss, a, d, cphi, sphi, tf)
    mh = _hyena_conv(zg, g0, cs, ss, gr, gi, p["filt_bias"], p["g_hyena"], tf)
    x1 = _merge(ma, mh, x, p["w_out"], p["ln1_g"], p["ln1_b"], tl)
    return _ffn(x1, p["ffn_w_in"], p["ffn_conv_w"], p["ffn_conv_b"], p["ffn_w_out"],
                p["ln2_g"], p["ln2_b"], tl)


def kernel(x_prompt, x_sample, w_in, short_w, short_b, rpb, filt_w1, filt_b1, filt_freq, filt_w_inner,
           filt_b_inner, filt_w3, filt_bias, g_attn, g_hyena, w_out, ln1_g, ln1_b, ffn_w_in, ffn_conv_w,
           ffn_conv_b, ffn_w_out, ln2_g, ln2_b):
    assert w_in.shape[0] == DEPTH == 1
    row = lambda a: a[0].reshape(1, -1)
    p = dict(
        w_in=w_in[0].astype(BF16), short_w=short_w[0], short_b=short_b[0],
        bias=_attention_bias(rpb[0]), g_attn=row(g_attn),
        filt_w1=filt_w1[0], filt_b1=filt_b1[0], filt_freq=filt_freq[0], filt_w_inner=filt_w_inner[0],
        filt_b_inner=filt_b_inner[0], filt_w3=filt_w3[0], filt_bias=filt_bias[0], g_hyena=row(g_hyena),
        w_out=w_out[0].astype(BF16), ln1_g=row(ln1_g), ln1_b=row(ln1_b),
        ffn_w_in=ffn_w_in[0].astype(BF16), ffn_conv_w=ffn_conv_w[0], ffn_conv_b=ffn_conv_b[0],
        ffn_w_out=ffn_w_out[0].astype(BF16), ln2_g=row(ln2_g), ln2_b=row(ln2_b),
    )
    return (_encoder_layer(x_prompt, p), _encoder_layer(x_sample, p))
```

```python
import functools
import math

import numpy as np
import jax
import jax.numpy as jnp
from jax import lax
from jax.experimental import pallas as pl
from jax.experimental.pallas import tpu as pltpu

F32 = jnp.float32
BF16 = jnp.bfloat16

D_MODEL = 1024
GRID_W = 64
ATTN_WIDTH = 512
HYENA_WIDTH = 512
HEAD_DIM = 64
N_HEADS = ATTN_WIDTH // HEAD_DIM
NA_ROWS = 8
NA_COLS = 16
FILTER_EMB = 33
FILTER_BANDS = (FILTER_EMB - 1) // 2
FILTER_HIDDEN = 64
FILTER_INNER = 2
MAX_DECAY = math.log(1e-2) / 0.3
MIN_DECAY = math.log(1e-2) / 1.5
D_FF = 2816
DEPTH = 1
ALPHA = (2 * DEPTH) ** 0.25
LN_EPS = 1e-5
RMS_EPS = 1e-6

SUBLANES = 8
LANES = 128
HALO = SUBLANES
VMEM_LIMIT = 56 * 1024 * 1024

HIGHEST = lax.Precision.HIGHEST


def _params(sem):
    return pltpu.CompilerParams(dimension_semantics=sem, vmem_limit_bytes=VMEM_LIMIT)


def _resident(shape):
    nd = len(shape)
    return pl.BlockSpec(shape, lambda *_: (0,) * nd, pipeline_mode=pl.Buffered(1))


def _layer_norm(y, g, b):
    mu = jnp.mean(y, axis=-1, keepdims=True)
    yc = y - mu
    var = jnp.mean(yc * yc, axis=-1, keepdims=True)
    return yc * lax.rsqrt(var + LN_EPS) * g + b


def _rms_norm(y, g):
    ms = jnp.mean(y * y, axis=-1, keepdims=True)
    return y * lax.rsqrt(ms + RMS_EPS) * g


def _halo_block(xp_ref, x_ref, xn_ref):
    i = pl.program_id(1)
    last = pl.num_programs(1) - 1
    prev = jnp.where(i > 0, xp_ref[0], 0.0)
    nxt = jnp.where(i < last, xn_ref[0], 0.0)
    return jnp.concatenate([prev, x_ref[0], nxt], axis=0).astype(BF16)


def _dwconv3(u_ref, w_ref, b_ref, tl):
    up = u_ref[pl.ds(HALO - 1, tl), :]
    uc = u_ref[pl.ds(HALO, tl), :]
    un = u_ref[pl.ds(HALO + 1, tl), :]
    return up * w_ref[0:1, :] + uc * w_ref[1:2, :] + un * w_ref[2:3, :] + b_ref[...]


def _halo_specs(tl, seq, d):
    nb = tl // HALO
    last = seq // HALO - 1
    return [
        pl.BlockSpec((1, HALO, d), lambda b, i: (b, jnp.maximum(i * nb - 1, 0), 0)),
        pl.BlockSpec((1, tl, d), lambda b, i: (b, i, 0)),
        pl.BlockSpec((1, HALO, d), lambda b, i: (b, jnp.minimum((i + 1) * nb, last), 0)),
    ]


def _in_proj_kernel(xp_ref, x_ref, xn_ref, w_ref, sw_ref, sb_ref,
                    q_ref, k_ref, v_ref, zg_ref, g0_ref, u_ref, *, tl):
    xq = x_ref[0].astype(BF16)
    proj = lambda n: jnp.dot(xq, w_ref[:, n * ATTN_WIDTH:(n + 1) * ATTN_WIDTH], preferred_element_type=F32)
    q_ref[0] = (proj(0) * (HEAD_DIM ** -0.5)).astype(BF16)
    k_ref[0] = proj(1).astype(BF16)
    v = proj(2).astype(BF16)
    ones = jnp.ones((tl, LANES), BF16)
    v_ref[0] = jnp.concatenate([piece for p in range(N_HEADS // 2)
                                for piece in (v[:, p * LANES:(p + 1) * LANES], ones)], axis=1)
    xh = _halo_block(xp_ref, x_ref, xn_ref)
    conv = []
    for n in range(3):
        lo = 3 * ATTN_WIDTH + n * HYENA_WIDTH
        u_ref[...] = jnp.dot(xh, w_ref[:, lo:lo + HYENA_WIDTH], preferred_element_type=F32)
        conv.append(_dwconv3(u_ref, sw_ref.at[n], sb_ref.at[n], tl))
    g0_ref[0] = conv[0].astype(BF16)
    zg_ref[0] = (conv[2] * conv[1]).astype(BF16)


def _in_proj(x, w_in, short_w, short_b, tl):
    bsz, seq, d = x.shape
    sw = short_w.reshape(3, 3, HYENA_WIDTH).transpose(1, 0, 2)
    sb = short_b.reshape(3, 1, HYENA_WIDTH)
    out = jax.ShapeDtypeStruct((bsz, seq, ATTN_WIDTH), BF16)
    ospec = pl.BlockSpec((1, tl, ATTN_WIDTH), lambda b, i: (b, i, 0))
    vout = jax.ShapeDtypeStruct((bsz, seq, 2 * ATTN_WIDTH), BF16)
    vspec = pl.BlockSpec((1, tl, 2 * ATTN_WIDTH), lambda b, i: (b, i, 0))
    return pl.pallas_call(
        functools.partial(_in_proj_kernel, tl=tl),
        out_shape=(out, out, vout, out, out),
        grid=(bsz, seq // tl),
        in_specs=_halo_specs(tl, seq, d) + [_resident(w_in.shape), _resident(sw.shape), _resident(sb.shape)],
        out_specs=(ospec, ospec, vspec, ospec, ospec),
        scratch_shapes=[pltpu.VMEM((tl + 2 * HALO, HYENA_WIDTH), F32)],
        compiler_params=_params(("parallel", "parallel")),
        name="in_proj",
    )(x, x, x, w_in, sw, sb)


N_PAIRS = N_HEADS // 2
KEY_WIN = NA_ROWS * GRID_W
N_DR = 2 * NA_ROWS - 1
N_DC = 2 * NA_COLS - 1


def _bias_kernel(rpb_ref, ea_ref, eb_ref, o_ref, t_ref):
    row = lax.broadcasted_iota(jnp.int32, (GRID_W, LANES), 0)
    lane = lax.broadcasted_iota(jnp.int32, (GRID_W, LANES), 1)
    first = lane < GRID_W
    qcol = lane & (GRID_W - 1)
    col_start = jnp.clip(qcol - NA_COLS // 2, 0, GRID_W - NA_COLS)
    valid = (row >= col_start) & (row < col_start + NA_COLS)
    wa = jnp.dot(rpb_ref[0, 0], ea_ref[...], precision=HIGHEST, preferred_element_type=F32)
    wb = jnp.dot(rpb_ref[0, 1], eb_ref[...], precision=HIGHEST, preferred_element_type=F32)

    def rotate_rows(x):
        for t in range(GRID_W.bit_length() - 1):
            x = jnp.where(((row >> t) & 1) == 1, pltpu.roll(x, 1 << t, axis=1), x)
        return x

    for dr in range(N_DR):
        ta = rotate_rows(jnp.broadcast_to(wa[dr:dr + 1], (GRID_W, LANES)))
        tb = rotate_rows(jnp.broadcast_to(wb[dr:dr + 1], (GRID_W, LANES)))
        t_ref[dr] = jnp.where(valid, jnp.where(first, ta, tb), -jnp.inf)
    for cls in range(NA_ROWS):
        for i in range(NA_ROWS):
            o_ref[0, cls, i * GRID_W:(i + 1) * GRID_W, :] = t_ref[i - cls + NA_ROWS - 1]


def _toeplitz_selectors():
    m = np.arange(LANES)
    dc = np.where(m < GRID_W, (NA_COLS - 1) - np.minimum(m, NA_COLS - 1),
                  np.minimum(LANES - m, NA_COLS - 1) + (NA_COLS - 1))
    ea = np.zeros((LANES, LANES), np.float32)
    ea[dc, m] = 1.0
    ea[:, GRID_W] = 0.0
    eb = np.roll(ea, GRID_W, axis=1)
    return ea, eb


def _attention_bias(rpb):
    ea, eb = _toeplitz_selectors()
    rp = jnp.pad(rpb.astype(F32), ((0, 0), (0, 16 - N_DR), (0, LANES - N_DC))).reshape(N_PAIRS, 2, 16, LANES)
    return pl.pallas_call(
        _bias_kernel,
        out_shape=jax.ShapeDtypeStruct((N_PAIRS, NA_ROWS, KEY_WIN, LANES), F32),
        grid=(N_PAIRS,),
        in_specs=[pl.BlockSpec((1, 2, 16, LANES), lambda p: (p, 0, 0, 0)),
                  _resident(ea.shape), _resident(eb.shape)],
        out_specs=pl.BlockSpec((1, NA_ROWS, KEY_WIN, LANES), lambda p: (p, 0, 0, 0)),
        scratch_shapes=[pltpu.VMEM((N_DR, GRID_W, LANES), F32)],
        compiler_params=_params(("parallel",)),
        name="attn_bias",
    )(rp, jnp.asarray(ea), jnp.asarray(eb))


def _attn_kernel(q_ref, k_ref, vx_ref, bias_ref, g_ref, o_ref, *, rows, rb, unroll):
    jb = pl.program_id(1)
    lane = lax.broadcasted_iota(jnp.int32, (GRID_W, LANES), 1)
    first = lane < HEAD_DIM
    keep = (first.astype(F32).astype(BF16), (~first).astype(F32).astype(BF16))
    nt = (((1,), (1,)), ((), ()))
    tn = (((0,), (0,)), ((), ()))

    def one_row(j):
        r = jb * rb + j
        start = jnp.clip(r - NA_ROWS // 2, 0, rows - NA_ROWS)
        cls = r - start
        qoff = pl.multiple_of(j * GRID_W, GRID_W)
        koff = pl.multiple_of(start * GRID_W, GRID_W)
        scores = []
        for p in range(N_PAIRS):
            sl = slice(p * LANES, (p + 1) * LANES)
            qp = q_ref[0, pl.ds(qoff, GRID_W), sl]
            qblk = jnp.concatenate([qp * keep[0], qp * keep[1]], axis=0)
            kp = k_ref[0, pl.ds(koff, KEY_WIN), sl]
            scores.append(lax.dot_general(kp, qblk, nt, preferred_element_type=F32))
        probs = []
        for p in range(N_PAIRS):
            s = scores[p] + bias_ref[p, cls]
            m = jnp.max(s, axis=0, keepdims=True)
            probs.append(jnp.exp(s - m).astype(BF16))
        outs = []
        for p in range(N_PAIRS):
            vx = vx_ref[0, pl.ds(koff, KEY_WIN), 2 * p * LANES:2 * (p + 1) * LANES]
            o = lax.dot_general(probs[p], vx, tn, preferred_element_type=F32)
            oa = o[:GRID_W, :LANES] / o[:GRID_W, LANES:]
            ob = o[GRID_W:, :LANES] / o[GRID_W:, LANES:]
            outs.append(jnp.where(first, oa, ob))
        o_ref[0, pl.ds(qoff, GRID_W), :] = _rms_norm(jnp.concatenate(outs, axis=1), g_ref[...]).astype(BF16)

    def body(jj, carry):
        for u in range(unroll):
            one_row(jj * unroll + u)
        return carry

    lax.fori_loop(0, rb // unroll, body, 0)


def _attention(q, k, vx, bias, g_attn, rb, unroll=2):
    bsz, seq, w = q.shape
    rows = seq // GRID_W
    tq = rb * GRID_W
    return pl.pallas_call(
        functools.partial(_attn_kernel, rows=rows, rb=rb, unroll=unroll),
        out_shape=jax.ShapeDtypeStruct((bsz, seq, w), BF16),
        grid=(bsz, rows // rb),
        in_specs=[
            pl.BlockSpec((1, tq, w), lambda b, i: (b, i, 0)),
            pl.BlockSpec((1, seq, w), lambda b, i: (b, 0, 0)),
            pl.BlockSpec((1, seq, 2 * w), lambda b, i: (b, 0, 0)),
            _resident(bias.shape),
            _resident(g_attn.shape),
        ],
        out_specs=pl.BlockSpec((1, tq, w), lambda b, i: (b, i, 0)),
        compiler_params=_params(("parallel", "arbitrary")),
        name="attention",
    )(q, k, vx, bias, g_attn)


def _filter_kernel(w1t_ref, w1c_ref, w1s_ref, b1_ref, fq_ref, wi_ref, bi_ref, w3_ref,
                   a_ref, d_ref, *, seq, tl):
    i = pl.program_id(0)
    row = lax.broadcasted_iota(jnp.int32, (tl, LANES), 0) + i * tl
    band = lax.broadcasted_iota(jnp.int32, (tl, LANES), 1)
    freqs = jnp.where(band < FILTER_BANDS,
                      1e-4 + band.astype(F32) * ((FILTER_BANDS - 1 - 1e-4) / (FILTER_BANDS - 1)), 0.0)
    chan = lax.broadcasted_iota(jnp.int32, (tl, HYENA_WIDTH), 1).astype(F32)
    deltas = jnp.abs(MIN_DECAY + chan * ((MAX_DECAY - MIN_DECAY) / (HYENA_WIDTH - 1)))
    fq = fq_ref[...]

    def taps(pos, cols):
        posf = pos.astype(F32)
        t = posf[:, 0:1] * (1.0 / (seq - 1))
        ang = posf * (2.0 * math.pi / seq) * freqs
        pre = (t * w1t_ref[...]
               + jnp.dot(jnp.cos(ang), w1c_ref[...], precision=HIGHEST, preferred_element_type=F32)
               - jnp.dot(jnp.sin(ang), w1s_ref[...], precision=HIGHEST, preferred_element_type=F32)
               + b1_ref[...])
        h = jnp.sin(fq * pre)
        for n in range(FILTER_INNER):
            h = jnp.sin(fq * (jnp.dot(h, wi_ref[n], precision=HIGHEST, preferred_element_type=F32)
                              + bi_ref[n]))
        out = jnp.dot(h, w3_ref[:, cols], precision=HIGHEST, preferred_element_type=F32)
        return out * jnp.exp(-t * deltas)

    h_fwd = taps(row, slice(0, HYENA_WIDTH))
    h_bwd = taps(row + 1, slice(HYENA_WIDTH, 2 * HYENA_WIDTH))
    h_bwd = jnp.where(row[:, 0:1] + 1 < seq, h_bwd, 0.0)
    a_ref[...] = (h_fwd + h_bwd).astype(BF16)
    d_ref[...] = (h_bwd - h_fwd).astype(BF16)


def _filter_taps(seq, w1, b1, freq, w_inner, b_inner, w3, tl=512):
    pad = LANES - FILTER_BANDS
    w1t = w1[0:1]
    w1c = jnp.pad(w1[1:1 + FILTER_BANDS], ((0, pad), (0, 0)))
    w1s = jnp.pad(w1[1 + FILTER_BANDS:], ((0, pad), (0, 0)))
    args = (w1t, w1c, w1s, b1.reshape(1, -1), freq.reshape(1, -1), w_inner,
            b_inner.reshape(FILTER_INNER, 1, -1), w3)
    out = jax.ShapeDtypeStruct((seq, HYENA_WIDTH), BF16)
    ospec = pl.BlockSpec((tl, HYENA_WIDTH), lambda i: (i, 0))
    return pl.pallas_call(
        functools.partial(_filter_kernel, seq=seq, tl=tl),
        out_shape=(out, out),
        grid=(seq // tl,),
        in_specs=[_resident(a.shape) for a in args],
        out_specs=(ospec, ospec),
        compiler_params=_params(("parallel",)),
        name="hyena_filter",
    )(*args)


def _dft_kernel(cs_ref, ss_ref, cphi_ref, sphi_ref, ca_ref, sa_ref, cb_ref, sb_ref, *, seq, tf):
    i = pl.program_id(0)
    n2 = 4 * seq

    @pl.when(i == 0)
    def _():
        s_odd = 2 * lax.broadcasted_iota(jnp.int32, (tf, seq), 1) + 1
        f_odd = 2 * lax.broadcasted_iota(jnp.int32, (tf, seq), 0) + 1
        beta = ((s_odd * f_odd) & (2 * n2 - 1)).astype(F32) * (math.pi / n2)
        cb_ref[...] = jnp.cos(beta)
        sb_ref[...] = jnp.sin(beta)
        nblk = seq // tf
        s_odd = 2 * lax.broadcasted_iota(jnp.int32, (nblk, seq), 1) + 1
        f_hi = 2 * tf * lax.broadcasted_iota(jnp.int32, (nblk, seq), 0)
        alpha = ((s_odd * f_hi) & (2 * n2 - 1)).astype(F32) * (math.pi / n2)
        ca_ref[...] = jnp.cos(alpha)
        sa_ref[...] = jnp.sin(alpha)

    ca = ca_ref[pl.ds(i, 1), :]
    sa = sa_ref[pl.ds(i, 1), :]
    cb = cb_ref[...]
    sb = sb_ref[...]
    cs_ref[...] = (ca * cb - sa * sb).astype(BF16)
    ss_ref[...] = (sa * cb + ca * sb).astype(BF16)
    f_odd = 2 * (lax.broadcasted_iota(jnp.int32, (tf, LANES), 0) + i * tf) + 1
    phi = f_odd.astype(F32) * (math.pi / n2)
    cphi_ref[...] = jnp.cos(phi)
    sphi_ref[...] = jnp.sin(phi)


def _dft_matrices(seq, tf=128):
    mat = jax.ShapeDtypeStruct((seq, seq), BF16)
    vec = jax.ShapeDtypeStruct((seq, LANES), F32)
    return pl.pallas_call(
        functools.partial(_dft_kernel, seq=seq, tf=tf),
        out_shape=(mat, mat, vec, vec),
        grid=(seq // tf,),
        out_specs=(pl.BlockSpec((tf, seq), lambda i: (i, 0)), pl.BlockSpec((tf, seq), lambda i: (i, 0)),
                   pl.BlockSpec((tf, LANES), lambda i: (i, 0)), pl.BlockSpec((tf, LANES), lambda i: (i, 0))),
        scratch_shapes=[pltpu.VMEM((seq // tf, seq), F32), pltpu.VMEM((seq // tf, seq), F32),
                        pltpu.VMEM((tf, seq), F32), pltpu.VMEM((tf, seq), F32)],
        compiler_params=_params(("arbitrary",)),
        name="dft_matrices",
    )()


def _spectrum_kernel(cs_ref, ss_ref, a_ref, d_ref, cphi_ref, sphi_ref, gr_ref, gi_ref, *, seq):
    kr = jnp.dot(cs_ref[...], a_ref[...], preferred_element_type=F32)
    ki = jnp.dot(ss_ref[...], d_ref[...], preferred_element_type=F32)
    reps = HYENA_WIDTH // LANES
    c = pltpu.repeat(cphi_ref[...], reps, axis=1) * (1.0 / seq)
    s = pltpu.repeat(sphi_ref[...], reps, axis=1) * (1.0 / seq)
    gr_ref[...] = c * kr - s * ki
    gi_ref[...] = c * ki + s * kr


def _filter_spectrum(cs, ss, a, d, cphi, sphi, tf):
    seq = cs.shape[0]
    out = jax.ShapeDtypeStruct((seq, HYENA_WIDTH), F32)
    row = lambda w: pl.BlockSpec((tf, w), lambda i: (i, 0))
    return pl.pallas_call(
        functools.partial(_spectrum_kernel, seq=seq),
        out_shape=(out, out),
        grid=(seq // tf,),
        in_specs=[row(seq), row(seq), _resident(a.shape), _resident(d.shape), row(LANES), row(LANES)],
        out_specs=(row(HYENA_WIDTH), row(HYENA_WIDTH)),
        compiler_params=_params(("parallel",)),
        name="hyena_spectrum",
    )(cs, ss, a, d, cphi, sphi)


def _hy_fwd_kernel(cs_ref, ss_ref, z_ref, gr_ref, gi_ref, yr_ref, yi_ref):
    z = z_ref[0]
    zr = jnp.dot(cs_ref[...], z, preferred_element_type=F32)
    wi = jnp.dot(ss_ref[...], z, preferred_element_type=F32)
    gr = gr_ref[...]
    gi = gi_ref[...]
    yr_ref[0] = (gr * zr + gi * wi).astype(BF16)
    yi_ref[0] = (gr * wi - gi * zr).astype(BF16)


def _hy_inv_kernel(cs_ref, ss_ref, yr_ref, yi_ref, z_ref, g0_ref, fb_ref, g_ref, o_ref):
    y = (jnp.dot(cs_ref[...], yr_ref[0], preferred_element_type=F32)
         + jnp.dot(ss_ref[...], yi_ref[0], preferred_element_type=F32))
    y = (y + z_ref[0].astype(F32) * fb_ref[...]) * g0_ref[0].astype(F32)
    o_ref[0] = _rms_norm(y, g_ref[...]).astype(BF16)


def _hyena_conv(zg, g0, cs, ss, gr, gi, fbias, g_hyena, tf):
    bsz, seq, w = zg.shape
    grid = (seq // tf, bsz)
    mat = pl.BlockSpec((tf, seq), lambda i, b: (i, 0))
    full = pl.BlockSpec((1, seq, w), lambda i, b: (b, 0, 0))
    tile = pl.BlockSpec((1, tf, w), lambda i, b: (b, i, 0))
    spec = pl.BlockSpec((tf, w), lambda i, b: (i, 0))
    act = jax.ShapeDtypeStruct((bsz, seq, w), BF16)
    yr, yi = pl.pallas_call(
        _hy_fwd_kernel,
        out_shape=(act, act),
        grid=grid,
        in_specs=[mat, mat, full, spec, spec],
        out_specs=(tile, tile),
        compiler_params=_params(("parallel", "parallel")),
        name="hyena_fwd",
    )(cs, ss, zg, gr, gi)
    return pl.pallas_call(
        _hy_inv_kernel,
        out_shape=act,
        grid=grid,
        in_specs=[mat, mat, full, full, tile, tile, _resident(fbias.shape), _resident(g_hyena.shape)],
        out_specs=tile,
        compiler_params=_params(("parallel", "parallel")),
        name="hyena_inv",
    )(cs, ss, yr, yi, zg, g0, fbias, g_hyena)


def _merge_kernel(ma_ref, mh_ref, x_ref, wa_ref, wh_ref, g_ref, b_ref, o_ref):
    y = (jnp.dot(ma_ref[0], wa_ref[...], preferred_element_type=F32)
         + jnp.dot(mh_ref[0], wh_ref[...], preferred_element_type=F32))
    o_ref[0] = _layer_norm(ALPHA * x_ref[0] + y, g_ref[...], b_ref[...])


def _merge(ma, mh, x, w_out, g, b, tl):
    bsz, seq, d = x.shape
    wa, wh = w_out[:ATTN_WIDTH], w_out[ATTN_WIDTH:]
    half = pl.BlockSpec((1, tl, ATTN_WIDTH), lambda bb, i: (bb, i, 0))
    full = pl.BlockSpec((1, tl, d), lambda bb, i: (bb, i, 0))
    return pl.pallas_call(
        _merge_kernel,
        out_shape=jax.ShapeDtypeStruct(x.shape, F32),
        grid=(bsz, seq // tl),
        in_specs=[half, half, full, _resident(wa.shape), _resident(wh.shape),
                  _resident(g.shape), _resident(b.shape)],
        out_specs=full,
        compiler_params=_params(("parallel", "parallel")),
        name="merge_ln1",
    )(ma, mh, x, wa, wh, g, b)


FF_CHUNK = 256
N_FF_CHUNKS = D_FF // FF_CHUNK


def _ffn_kernel(xp_ref, x_ref, xn_ref, wi_ref, cw_ref, cb_ref, wo_ref, g_ref, b_ref,
                o_ref, u_ref, hid_ref, *, tl):
    xh = _halo_block(xp_ref, x_ref, xn_ref)
    for j in range(N_FF_CHUNKS):
        conv = []
        for part in range(2):
            n = part * N_FF_CHUNKS + j
            u_ref[...] = jnp.dot(xh, wi_ref[n], preferred_element_type=F32)
            conv.append(_dwconv3(u_ref, cw_ref.at[n], cb_ref.at[n], tl))
        gate = conv[1]
        gelu = 0.5 * gate * (1.0 + lax.erf(gate * (2.0 ** -0.5)))
        hid_ref[:, j * FF_CHUNK:(j + 1) * FF_CHUNK] = (conv[0] * gelu).astype(BF16)
    y = jnp.dot(hid_ref[...], wo_ref[...], preferred_element_type=F32)
    o_ref[0] = _layer_norm(ALPHA * x_ref[0] + y, g_ref[...], b_ref[...])


def _ffn(x1, w_in, conv_w, conv_b, w_out, g, b, tl):
    bsz, seq, d = x1.shape
    nch = 2 * N_FF_CHUNKS
    wi = w_in.reshape(d, nch, FF_CHUNK).transpose(1, 0, 2)
    cw = conv_w.reshape(3, nch, FF_CHUNK).transpose(1, 0, 2)
    cb = conv_b.reshape(nch, 1, FF_CHUNK)
    return pl.pallas_call(
        functools.partial(_ffn_kernel, tl=tl),
        out_shape=jax.ShapeDtypeStruct(x1.shape, F32),
        grid=(bsz, seq // tl),
        in_specs=_halo_specs(tl, seq, d) + [_resident(wi.shape), _resident(cw.shape), _resident(cb.shape),
                                       _resident(w_out.shape), _resident(g.shape), _resident(b.shape)],
        out_specs=pl.BlockSpec((1, tl, d), lambda bb, i: (bb, i, 0)),
        scratch_shapes=[pltpu.VMEM((tl + 2 * HALO, FF_CHUNK), F32), pltpu.VMEM((tl, D_FF), BF16)],
        compiler_params=_params(("parallel", "parallel")),
        name="conv_ffn",
    )(x1, x1, x1, wi, cw, cb, w_out, g, b)


def _encoder_layer(x, p, tl=512, rb=8, tf=512):
    seq = x.shape[1]
    q, k, v, zg, g0 = _in_proj(x, p["w_in"], p["short_w"], p["short_b"], tl)
    ma = _attention(q, k, v, p["bias"], p["g_attn"], rb)
    a, d = _filter_taps(seq, p["filt_w1"], p["filt_b1"], p["filt_freq"], p["filt_w_inner"],
                        p["filt_b_inner"], p["filt_w3"])
    cs, ss, cphi, sphi = _dft_matrices(seq)
    gr, gi = _filter_spectrum(cs, ss, a, d, cphi, sphi, tf)
    mh = _hyena_conv(zg, g0, cs, ss, gr, gi, p["filt_bias"], p["g_hyena"], tf)
    x1 = _merge(ma, mh, x, p["w_out"], p["ln1_g"], p["ln1_b"], tl)
    return _ffn(x1, p["ffn_w_in"], p["ffn_conv_w"], p["ffn_conv_b"], p["ffn_w_out"],
                p["ln2_g"], p["ln2_b"], tl)


def kernel(x_prompt, x_sample, w_in, short_w, short_b, rpb, filt_w1, filt_b1, filt_freq, filt_w_inner,
           filt_b_inner, filt_w3, filt_bias, g_attn, g_hyena, w_out, ln1_g, ln1_b, ffn_w_in, ffn_conv_w,
           ffn_conv_b, ffn_w_out, ln2_g, ln2_b):
    assert w_in.shape[0] == DEPTH == 1
    row = lambda a: a[0].reshape(1, -1)
    p = dict(
        w_in=w_in[0].astype(BF16), short_w=short_w[0], short_b=short_b[0],
        bias=_attention_bias(rpb[0]), g_attn=row(g_attn),
        filt_w1=filt_w1[0], filt_b1=filt_b1[0], filt_freq=filt_freq[0], filt_w_inner=filt_w_inner[0],
        filt_b_inner=filt_b_inner[0], filt_w3=filt_w3[0], filt_bias=filt_bias[0], g_hyena=row(g_hyena),
        w_out=w_out[0].astype(BF16), ln1_g=row(ln1_g), ln1_b=row(ln1_b),
        ffn_w_in=ffn_w_in[0].astype(BF16), ffn_conv_w=ffn_conv_w[0], ffn_conv_b=ffn_conv_b[0],
        ffn_w_out=ffn_w_out[0].astype(BF16), ln2_g=row(ln2_g), ln2_b=row(ln2_b),
    )
    return (_encoder_layer(x_prompt, p), _encoder_layer(x_sample, p))
```

```python
import functools
import math

import numpy as np
import jax
import jax.numpy as jnp
from jax import lax
from jax.experimental import pallas as pl
from jax.experimental.pallas import tpu as pltpu

F32 = jnp.float32
BF16 = jnp.bfloat16

D_MODEL = 1024
GRID_W = 64
ATTN_WIDTH = 512
HYENA_WIDTH = 512
HEAD_DIM = 64
N_HEADS = ATTN_WIDTH // HEAD_DIM
NA_ROWS = 8
NA_COLS = 16
FILTER_EMB = 33
FILTER_BANDS = (FILTER_EMB - 1) // 2
FILTER_HIDDEN = 64
FILTER_INNER = 2
MAX_DECAY = math.log(1e-2) / 0.3
MIN_DECAY = math.log(1e-2) / 1.5
D_FF = 2816
DEPTH = 1
ALPHA = (2 * DEPTH) ** 0.25
LN_EPS = 1e-5
RMS_EPS = 1e-6

SUBLANES = 8
LANES = 128
HALO = SUBLANES
VMEM_LIMIT = 56 * 1024 * 1024

HIGHEST = lax.Precision.HIGHEST


def _params(sem):
    return pltpu.CompilerParams(dimension_semantics=sem, vmem_limit_bytes=VMEM_LIMIT)


def _resident(shape):
    nd = len(shape)
    return pl.BlockSpec(shape, lambda *_: (0,) * nd, pipeline_mode=pl.Buffered(1))


def _layer_norm(y, g, b):
    mu = jnp.mean(y, axis=-1, keepdims=True)
    yc = y - mu
    var = jnp.mean(yc * yc, axis=-1, keepdims=True)
    return yc * lax.rsqrt(var + LN_EPS) * g + b


def _rms_norm(y, g):
    ms = jnp.mean(y * y, axis=-1, keepdims=True)
    return y * lax.rsqrt(ms + RMS_EPS) * g


def _halo_block(xp_ref, x_ref, xn_ref):
    i = pl.program_id(1)
    last = pl.num_programs(1) - 1
    prev = jnp.where(i > 0, xp_ref[0], 0.0)
    nxt = jnp.where(i < last, xn_ref[0], 0.0)
    return jnp.concatenate([prev, x_ref[0], nxt], axis=0).astype(BF16)


def _dwconv3(u_ref, w_ref, b_ref, tl):
    up = u_ref[pl.ds(HALO - 1, tl), :]
    uc = u_ref[pl.ds(HALO, tl), :]
    un = u_ref[pl.ds(HALO + 1, tl), :]
    return up * w_ref[0:1, :] + uc * w_ref[1:2, :] + un * w_ref[2:3, :] + b_ref[...]


def _halo_specs(tl, seq, d):
    nb = tl // HALO
    last = seq // HALO - 1
    return [
        pl.BlockSpec((1, HALO, d), lambda b, i: (b, jnp.maximum(i * nb - 1, 0), 0)),
        pl.BlockSpec((1, tl, d), lambda b, i: (b, i, 0)),
        pl.BlockSpec((1, HALO, d), lambda b, i: (b, jnp.minimum((i + 1) * nb, last), 0)),
    ]


def _in_proj_kernel(xp_ref, x_ref, xn_ref, w_ref, sw_ref, sb_ref,
                    q_ref, k_ref, v_ref, zg_ref, g0_ref, u_ref, *, tl):
    xq = x_ref[0].astype(BF16)
    proj = lambda n: jnp.dot(xq, w_ref[:, n * ATTN_WIDTH:(n + 1) * ATTN_WIDTH], preferred_element_type=F32)
    q_ref[0] = (proj(0) * (HEAD_DIM ** -0.5)).astype(BF16)
    k_ref[0] = proj(1).astype(BF16)
    v = proj(2).astype(BF16)
    ones = jnp.ones((tl, LANES), BF16)
    v_ref[0] = jnp.concatenate([piece for p in range(N_HEADS // 2)
                                for piece in (v[:, p * LANES:(p + 1) * LANES], ones)], axis=1)
    xh = _halo_block(xp_ref, x_ref, xn_ref)
    conv = []
    for n in range(3):
        lo = 3 * ATTN_WIDTH + n * HYENA_WIDTH
        u_ref[...] = jnp.dot(xh, w_ref[:, lo:lo + HYENA_WIDTH], preferred_element_type=F32)
        conv.append(_dwconv3(u_ref, sw_ref.at[n], sb_ref.at[n], tl))
    g0_ref[0] = conv[0].astype(BF16)
    zg_ref[0] = (conv[2] * conv[1]).astype(BF16)


def _in_proj(x, w_in, short_w, short_b, tl):
    bsz, seq, d = x.shape
    sw = short_w.reshape(3, 3, HYENA_WIDTH).transpose(1, 0, 2)
    sb = short_b.reshape(3, 1, HYENA_WIDTH)
    out = jax.ShapeDtypeStruct((bsz, seq, ATTN_WIDTH), BF16)
    ospec = pl.BlockSpec((1, tl, ATTN_WIDTH), lambda b, i: (b, i, 0))
    vout = jax.ShapeDtypeStruct((bsz, seq, 2 * ATTN_WIDTH), BF16)
    vspec = pl.BlockSpec((1, tl, 2 * ATTN_WIDTH), lambda b, i: (b, i, 0))
    return pl.pallas_call(
        functools.partial(_in_proj_kernel, tl=tl),
        out_shape=(out, out, vout, out, out),
        grid=(bsz, seq // tl),
        in_specs=_halo_specs(tl, seq, d) + [_resident(w_in.shape), _resident(sw.shape), _resident(sb.shape)],
        out_specs=(ospec, ospec, vspec, ospec, ospec),
        scratch_shapes=[pltpu.VMEM((tl + 2 * HALO, HYENA_WIDTH), F32)],
        compiler_params=_params(("parallel", "parallel")),
        name="in_proj",
    )(x, x, x, w_in, sw, sb)


N_PAIRS = N_HEADS // 2
KEY_WIN = NA_ROWS * GRID_W
N_DR = 2 * NA_ROWS - 1
N_DC = 2 * NA_COLS - 1


def _bias_kernel(rpb_ref, ea_ref, eb_ref, o_ref, t_ref):
    row = lax.broadcasted_iota(jnp.int32, (GRID_W, LANES), 0)
    lane = lax.broadcasted_iota(jnp.int32, (GRID_W, LANES), 1)
    first = lane < GRID_W
    qcol = lane & (GRID_W - 1)
    col_start = jnp.clip(qcol - NA_COLS // 2, 0, GRID_W - NA_COLS)
    valid = (row >= col_start) & (row < col_start + NA_COLS)
    wa = jnp.dot(rpb_ref[0, 0], ea_ref[...], precision=HIGHEST, preferred_element_type=F32)
    wb = jnp.dot(rpb_ref[0, 1], eb_ref[...], precision=HIGHEST, preferred_element_type=F32)

    def rotate_rows(x):
        for t in range(GRID_W.bit_length() - 1):
            x = jnp.where(((row >> t) & 1) == 1, pltpu.roll(x, 1 << t, axis=1), x)
        return x

    for dr in range(N_DR):
        ta = rotate_rows(jnp.broadcast_to(wa[dr:dr + 1], (GRID_W, LANES)))
        tb = rotate_rows(jnp.broadcast_to(wb[dr:dr + 1], (GRID_W, LANES)))
        t_ref[dr] = jnp.where(valid, jnp.where(first, ta, tb), -jnp.inf)
    for cls in range(NA_ROWS):
        for i in range(NA_ROWS):
            o_ref[0, cls, i * GRID_W:(i + 1) * GRID_W, :] = t_ref[i - cls + NA_ROWS - 1]


def _toeplitz_selectors():
    m = np.arange(LANES)
    dc = np.where(m < GRID_W, (NA_COLS - 1) - np.minimum(m, NA_COLS - 1),
                  np.minimum(LANES - m, NA_COLS - 1) + (NA_COLS - 1))
    ea = np.zeros((LANES, LANES), np.float32)
    ea[dc, m] = 1.0
    ea[:, GRID_W] = 0.0
    eb = np.roll(ea, GRID_W, axis=1)
    return ea, eb


def _attention_bias(rpb):
    ea, eb = _toeplitz_selectors()
    rp = jnp.pad(rpb.astype(F32), ((0, 0), (0, 16 - N_DR), (0, LANES - N_DC))).reshape(N_PAIRS, 2, 16, LANES)
    return pl.pallas_call(
        _bias_kernel,
        out_shape=jax.ShapeDtypeStruct((N_PAIRS, NA_ROWS, KEY_WIN, LANES), F32),
        grid=(N_PAIRS,),
        in_specs=[pl.BlockSpec((1, 2, 16, LANES), lambda p: (p, 0, 0, 0)),
                  _resident(ea.shape), _resident(eb.shape)],
        out_specs=pl.BlockSpec((1, NA_ROWS, KEY_WIN, LANES), lambda p: (p, 0, 0, 0)),
        scratch_shapes=[pltpu.VMEM((N_DR, GRID_W, LANES), F32)],
        compiler_params=_params(("parallel",)),
        name="attn_bias",
    )(rp, jnp.asarray(ea), jnp.asarray(eb))


def _attn_kernel(q_ref, k_ref, vx_ref, bias_ref, g_ref, o_ref, *, rows, rb, unroll):
    jb = pl.program_id(1)
    lane = lax.broadcasted_iota(jnp.int32, (GRID_W, LANES), 1)
    first = lane < HEAD_DIM
    keep = (first.astype(F32).astype(BF16), (~first).astype(F32).astype(BF16))
    nt = (((1,), (1,)), ((), ()))
    tn = (((0,), (0,)), ((), ()))

    def one_row(j):
        r = jb * rb + j
        start = jnp.clip(r - NA_ROWS // 2, 0, rows - NA_ROWS)
        cls = r - start
        qoff = pl.multiple_of(j * GRID_W, GRID_W)
        koff = pl.multiple_of(start * GRID_W, GRID_W)
        scores = []
        for p in range(N_PAIRS):
            sl = slice(p * LANES, (p + 1) * LANES)
            qp = q_ref[0, pl.ds(qoff, GRID_W), sl]
            qblk = jnp.concatenate([qp * keep[0], qp * keep[1]], axis=0)
            kp = k_ref[0, pl.ds(koff, KEY_WIN), sl]
            scores.append(lax.dot_general(kp, qblk, nt, preferred_element_type=F32))
        probs = []
        for p in range(N_PAIRS):
            s = scores[p] + bias_ref[p, cls]
            m = jnp.max(s, axis=0, keepdims=True)
            probs.append(jnp.exp(s - m).astype(BF16))
        outs = []
        for p in range(N_PAIRS):
            vx = vx_ref[0, pl.ds(koff, KEY_WIN), 2 * p * LANES:2 * (p + 1) * LANES]
            o = lax.dot_general(probs[p], vx, tn, preferred_element_type=F32)
            oa = o[:GRID_W, :LANES] / o[:GRID_W, LANES:]
            ob = o[GRID_W:, :LANES] / o[GRID_W:, LANES:]
            outs.append(jnp.where(first, oa, ob))
        o_ref[0, pl.ds(qoff, GRID_W), :] = _rms_norm(jnp.concatenate(outs, axis=1), g_ref[...]).astype(BF16)

    def body(jj, carry):
        for u in range(unroll):
            one_row(jj * unroll + u)
        return carry

    lax.fori_loop(0, rb // unroll, body, 0)


def _attention(q, k, vx, bias, g_attn, rb, unroll=2):
    bsz, seq, w = q.shape
    rows = seq // GRID_W
    tq = rb * GRID_W
    return pl.pallas_call(
        functools.partial(_attn_kernel, rows=rows, rb=rb, unroll=unroll),
        out_shape=jax.ShapeDtypeStruct((bsz, seq, w), BF16),
        grid=(bsz, rows // rb),
        in_specs=[
            pl.BlockSpec((1, tq, w), lambda b, i: (b, i, 0)),
            pl.BlockSpec((1, seq, w), lambda b, i: (b, 0, 0)),
            pl.BlockSpec((1, seq, 2 * w), lambda b, i: (b, 0, 0)),
            _resident(bias.shape),
            _resident(g_attn.shape),
        ],
        out_specs=pl.BlockSpec((1, tq, w), lambda b, i: (b, i, 0)),
        compiler_params=_params(("parallel", "arbitrary")),
        name="attention",
    )(q, k, vx, bias, g_attn)


def _filter_kernel(w1t_ref, w1c_ref, w1s_ref, b1_ref, fq_ref, wi_ref, bi_ref, w3_ref,
                   a_ref, d_ref, *, seq, tl):
    i = pl.program_id(0)
    row = lax.broadcasted_iota(jnp.int32, (tl, LANES), 0) + i * tl
    band = lax.broadcasted_iota(jnp.int32, (tl, LANES), 1)
    freqs = jnp.where(band < FILTER_BANDS,
                      1e-4 + band.astype(F32) * ((FILTER_BANDS - 1 - 1e-4) / (FILTER_BANDS - 1)), 0.0)
    chan = lax.broadcasted_iota(jnp.int32, (tl, HYENA_WIDTH), 1).astype(F32)
    deltas = jnp.abs(MIN_DECAY + chan * ((MAX_DECAY - MIN_DECAY) / (HYENA_WIDTH - 1)))
    fq = fq_ref[...]

    def taps(pos, cols):
        posf = pos.astype(F32)
        t = posf[:, 0:1] * (1.0 / (seq - 1))
        ang = posf * (2.0 * math.pi / seq) * freqs
        pre = (t * w1t_ref[...]
               + jnp.dot(jnp.cos(ang), w1c_ref[...], precision=HIGHEST, preferred_element_type=F32)
               - jnp.dot(jnp.sin(ang), w1s_ref[...], precision=HIGHEST, preferred_element_type=F32)
               + b1_ref[...])
        h = jnp.sin(fq * pre)
        for n in range(FILTER_INNER):
            h = jnp.sin(fq * (jnp.dot(h, wi_ref[n], precision=HIGHEST, preferred_element_type=F32)
                              + bi_ref[n]))
        out = jnp.dot(h, w3_ref[:, cols], precision=HIGHEST, preferred_element_type=F32)
        return out * jnp.exp(-t * deltas)

    h_fwd = taps(row, slice(0, HYENA_WIDTH))
    h_bwd = taps(row + 1, slice(HYENA_WIDTH, 2 * HYENA_WIDTH))
    h_bwd = jnp.where(row[:, 0:1] + 1 < seq, h_bwd, 0.0)
    a_ref[...] = (h_fwd + h_bwd).astype(BF16)
    d_ref[...] = (h_bwd - h_fwd).astype(BF16)


def _filter_taps(seq, w1, b1, freq, w_inner, b_inner, w3, tl=512):
    pad = LANES - FILTER_BANDS
    w1t = w1[0:1]
    w1c = jnp.pad(w1[1:1 + FILTER_BANDS], ((0, pad), (0, 0)))
    w1s = jnp.pad(w1[1 + FILTER_BANDS:], ((0, pad), (0, 0)))
    args = (w1t, w1c, w1s, b1.reshape(1, -1), freq.reshape(1, -1), w_inner,
            b_inner.reshape(FILTER_INNER, 1, -1), w3)
    out = jax.ShapeDtypeStruct((seq, HYENA_WIDTH), BF16)
    ospec = pl.BlockSpec((tl, HYENA_WIDTH), lambda i: (i, 0))
    return pl.pallas_call(
        functools.partial(_filter_kernel, seq=seq, tl=tl),
        out_shape=(out, out),
        grid=(seq // tl,),
        in_specs=[_resident(a.shape) for a in args],
        out_specs=(ospec, ospec),
        compiler_params=_params(("parallel",)),
        name="hyena_filter",
    )(*args)


P_BLK = 512
F_HALF = P_BLK // 2


def _pair_coefficients(seq):
    s1n = seq // P_BLK
    g = np.arange(s1n)[:, None]
    s = np.arange(s1n)[None, :]
    ang = np.pi * (2 * g + 1) * s / (2 * s1n)
    return jnp.asarray(np.cos(ang), F32), jnp.asarray(np.sin(ang), F32)


def _dft_kernel(t_ref, tt_ref, cb_ref, sb_ref, *, seq):
    g = pl.program_id(0)
    s1n = seq // P_BLK

    @pl.when(g == 0)
    def _():
        f2 = lax.broadcasted_iota(jnp.int32, (F_HALF, P_BLK), 0)
        s_odd = 2 * lax.broadcasted_iota(jnp.int32, (F_HALF, P_BLK), 1) + 1
        beta = ((f2 * s_odd) & (2 * P_BLK - 1)).astype(F32) * (math.pi / P_BLK)
        cb_ref[...] = jnp.cos(beta)
        sb_ref[...] = jnp.sin(beta)

    cb = cb_ref[...]
    sb = sb_ref[...]
    s_odd = 2 * lax.broadcasted_iota(jnp.int32, (SUBLANES, P_BLK), 1) + 1

    def block(f1):
        alpha = ((s_odd * (2 * f1 + 1)) & (8 * seq - 1)).astype(F32) * (math.pi / (4 * seq))
        ca = jnp.cos(alpha)[0:1]
        sa = jnp.sin(alpha)[0:1]
        return ca * cb - sa * sb, sa * cb + ca * sb

    mra, mia = block(g)
    mrb, mib = block(2 * s1n - 1 - g)
    for r, (left, right) in enumerate(((mra, -mia), (mia, mra), (mrb, mib), (mib, -mrb))):
        rows = slice(r * F_HALF, (r + 1) * F_HALF)
        t_ref[0, rows, :P_BLK] = left.astype(BF16)
        t_ref[0, rows, P_BLK:] = right.astype(BF16)
        tt_ref[0, :P_BLK, rows] = left.T.astype(BF16)
        tt_ref[0, P_BLK:, rows] = right.T.astype(BF16)


def _dft_matrices(seq):
    s1n = seq // P_BLK
    mat = jax.ShapeDtypeStruct((s1n, 2 * P_BLK, 2 * P_BLK), BF16)
    spec = pl.BlockSpec((1, 2 * P_BLK, 2 * P_BLK), lambda g: (g, 0, 0))
    return pl.pallas_call(
        functools.partial(_dft_kernel, seq=seq),
        out_shape=(mat, mat),
        grid=(s1n,),
        out_specs=(spec, spec),
        scratch_shapes=[pltpu.VMEM((F_HALF, P_BLK), F32), pltpu.VMEM((F_HALF, P_BLK), F32)],
        compiler_params=_params(("arbitrary",)),
        name="dft_matrices",
    )()


_SMEM = pl.BlockSpec(memory_space=pltpu.SMEM)


def _stacked_blocks(cr_ref, sr_ref, g, block, s1n):
    ar = cr_ref[g, 0] * block(0)
    bi = sr_ref[g, 0] * block(0)
    for s in range(1, s1n):
        ar = ar + cr_ref[g, s] * block(s)
        bi = bi + sr_ref[g, s] * block(s)
    return jnp.concatenate([ar, bi], axis=0).astype(BF16)


def _spectrum_kernel(cr_ref, sr_ref, t_ref, a_ref, d_ref, g_ref, *, seq):
    g = pl.program_id(0)
    s1n = seq // P_BLK
    time_block = lambda ref: (lambda s: ref[s * P_BLK:(s + 1) * P_BLK, :].astype(F32))
    ka = jnp.dot(t_ref[0], _stacked_blocks(cr_ref, sr_ref, g, time_block(a_ref), s1n),
                 preferred_element_type=F32)
    kd = jnp.dot(t_ref[0], _stacked_blocks(cr_ref, sr_ref, g, time_block(d_ref), s1n),
                 preferred_element_type=F32)
    f2 = lax.broadcasted_iota(jnp.int32, (F_HALF, LANES), 0)
    reps = HYENA_WIDTH // LANES
    for blk, f1 in ((0, g), (1, 2 * s1n - 1 - g)):
        lo = blk * P_BLK
        phi = (2 * (f1 + 2 * s1n * f2) + 1).astype(F32) * (math.pi / (4 * seq))
        c = pltpu.repeat(jnp.cos(phi), reps, axis=1) * (1.0 / seq)
        s = pltpu.repeat(jnp.sin(phi), reps, axis=1) * (1.0 / seq)
        kr = ka[lo:lo + F_HALF]
        ki = kd[lo + F_HALF:lo + P_BLK]
        g_ref[0, lo:lo + F_HALF] = c * kr - s * ki
        g_ref[0, lo + F_HALF:lo + P_BLK] = c * ki + s * kr


def _filter_spectrum(cr, sr, t, a, d):
    seq = a.shape[0]
    s1n = seq // P_BLK
    return pl.pallas_call(
        functools.partial(_spectrum_kernel, seq=seq),
        out_shape=jax.ShapeDtypeStruct((s1n, 2 * P_BLK, HYENA_WIDTH), F32),
        grid=(s1n,),
        in_specs=[_SMEM, _SMEM, pl.BlockSpec((1, 2 * P_BLK, 2 * P_BLK), lambda g: (g, 0, 0)),
                  _resident(a.shape), _resident(d.shape)],
        out_specs=pl.BlockSpec((1, 2 * P_BLK, HYENA_WIDTH), lambda g: (g, 0, 0)),
        compiler_params=_params(("parallel",)),
        name="hyena_spectrum",
    )(cr, sr, t, a, d)


def _hy_fwd_kernel(cr_ref, sr_ref, t_ref, g_ref, z_ref, y_ref, *, s1n):
    g = pl.program_id(0)
    x = _stacked_blocks(cr_ref, sr_ref, g,
                        lambda s: z_ref[0, s * P_BLK:(s + 1) * P_BLK, :].astype(F32), s1n)
    res = jnp.dot(t_ref[0], x, preferred_element_type=F32)
    for blk in range(2):
        lo = blk * P_BLK
        zr = res[lo:lo + F_HALF]
        wi = res[lo + F_HALF:lo + P_BLK]
        gr = g_ref[0, lo:lo + F_HALF]
        gi = g_ref[0, lo + F_HALF:lo + P_BLK]
        y_ref[0, 0, lo:lo + F_HALF] = (gr * zr + gi * wi).astype(BF16)
        y_ref[0, 0, lo + F_HALF:lo + P_BLK] = (gr * wi - gi * zr).astype(BF16)


def _hy_inv_kernel(cr_ref, sr_ref, tt_ref, y_ref, z_ref, g0_ref, fb_ref, gn_ref, o_ref,
                   acc_ref, zs_ref, g0s_ref, *, s1n):
    g = pl.program_id(1)

    @pl.when(g == 0)
    def _():
        acc_ref[...] = jnp.zeros_like(acc_ref)

    zs_ref[pl.ds(pl.multiple_of(g * P_BLK, P_BLK), P_BLK), :] = z_ref[0]
    g0s_ref[pl.ds(pl.multiple_of(g * P_BLK, P_BLK), P_BLK), :] = g0_ref[0]
    uv = jnp.dot(tt_ref[g], y_ref[0, 0], preferred_element_type=F32)
    u = uv[:P_BLK]
    vn = uv[P_BLK:]
    for t1 in range(s1n):
        rows = slice(t1 * P_BLK, (t1 + 1) * P_BLK)
        acc_ref[rows, :] += cr_ref[g, t1] * u + sr_ref[g, t1] * vn

    @pl.when(g == s1n - 1)
    def _():
        for t1 in range(s1n):
            rows = slice(t1 * P_BLK, (t1 + 1) * P_BLK)
            y = (acc_ref[rows, :] + zs_ref[rows, :].astype(F32) * fb_ref[...]) * g0s_ref[rows, :].astype(F32)
            o_ref[0, rows, :] = _rms_norm(y, gn_ref[...]).astype(BF16)


def _hyena_conv(zg, g0, cr, sr, t, tt, gspec, fbias, g_hyena):
    bsz, seq, w = zg.shape
    s1n = seq // P_BLK
    mat = pl.BlockSpec((1, 2 * P_BLK, 2 * P_BLK), lambda g, b: (g, 0, 0))
    spec = pl.BlockSpec((1, 2 * P_BLK, w), lambda g, b: (g, 0, 0))
    y = pl.pallas_call(
        functools.partial(_hy_fwd_kernel, s1n=s1n),
        out_shape=jax.ShapeDtypeStruct((bsz, s1n, 2 * P_BLK, w), BF16),
        grid=(s1n, bsz),
        in_specs=[_SMEM, _SMEM, mat, spec, pl.BlockSpec((1, seq, w), lambda g, b: (b, 0, 0))],
        out_specs=pl.BlockSpec((1, 1, 2 * P_BLK, w), lambda g, b: (b, g, 0, 0)),
        compiler_params=_params(("parallel", "parallel")),
        name="hyena_fwd",
    )(cr, sr, t, gspec, zg)
    blk = pl.BlockSpec((1, P_BLK, w), lambda b, g: (b, g, 0))
    return pl.pallas_call(
        functools.partial(_hy_inv_kernel, s1n=s1n),
        out_shape=jax.ShapeDtypeStruct((bsz, seq, w), BF16),
        grid=(bsz, s1n),
        in_specs=[_SMEM, _SMEM, _resident(tt.shape),
                  pl.BlockSpec((1, 1, 2 * P_BLK, w), lambda b, g: (b, g, 0, 0)),
                  blk, blk, _resident(fbias.shape), _resident(g_hyena.shape)],
        out_specs=pl.BlockSpec((1, seq, w), lambda b, g: (b, 0, 0)),
        scratch_shapes=[pltpu.VMEM((seq, w), F32), pltpu.VMEM((seq, w), BF16), pltpu.VMEM((seq, w), BF16)],
        compiler_params=_params(("parallel", "arbitrary")),
        name="hyena_inv",
    )(cr, sr, tt, y, zg, g0, fbias, g_hyena)


def _merge_kernel(ma_ref, mh_ref, x_ref, wa_ref, wh_ref, g_ref, b_ref, o_ref):
    y = (jnp.dot(ma_ref[0], wa_ref[...], preferred_element_type=F32)
         + jnp.dot(mh_ref[0], wh_ref[...], preferred_element_type=F32))
    o_ref[0] = _layer_norm(ALPHA * x_ref[0] + y, g_ref[...], b_ref[...])


def _merge(ma, mh, x, w_out, g, b, tl):
    bsz, seq, d = x.shape
    wa, wh = w_out[:ATTN_WIDTH], w_out[ATTN_WIDTH:]
    half = pl.BlockSpec((1, tl, ATTN_WIDTH), lambda bb, i: (bb, i, 0))
    full = pl.BlockSpec((1, tl, d), lambda bb, i: (bb, i, 0))
    return pl.pallas_call(
        _merge_kernel,
        out_shape=jax.ShapeDtypeStruct(x.shape, F32),
        grid=(bsz, seq // tl),
        in_specs=[half, half, full, _resident(wa.shape), _resident(wh.shape),
                  _resident(g.shape), _resident(b.shape)],
        out_specs=full,
        compiler_params=_params(("parallel", "parallel")),
        name="merge_ln1",
    )(ma, mh, x, wa, wh, g, b)


FF_CHUNK = 256
N_FF_CHUNKS = D_FF // FF_CHUNK


def _ffn_kernel(xp_ref, x_ref, xn_ref, wi_ref, cw_ref, cb_ref, wo_ref, g_ref, b_ref,
                o_ref, u_ref, hid_ref, *, tl):
    xh = _halo_block(xp_ref, x_ref, xn_ref)
    for j in range(N_FF_CHUNKS):
        conv = []
        for part in range(2):
            n = part * N_FF_CHUNKS + j
            u_ref[...] = jnp.dot(xh, wi_ref[n], preferred_element_type=F32)
            conv.append(_dwconv3(u_ref, cw_ref.at[n], cb_ref.at[n], tl))
        gate = conv[1]
        gelu = 0.5 * gate * (1.0 + lax.erf(gate * (2.0 ** -0.5)))
        hid_ref[:, j * FF_CHUNK:(j + 1) * FF_CHUNK] = (conv[0] * gelu).astype(BF16)
    y = jnp.dot(hid_ref[...], wo_ref[...], preferred_element_type=F32)
    o_ref[0] = _layer_norm(ALPHA * x_ref[0] + y, g_ref[...], b_ref[...])


def _ffn(x1, w_in, conv_w, conv_b, w_out, g, b, tl):
    bsz, seq, d = x1.shape
    nch = 2 * N_FF_CHUNKS
    wi = w_in.reshape(d, nch, FF_CHUNK).transpose(1, 0, 2)
    cw = conv_w.reshape(3, nch, FF_CHUNK).transpose(1, 0, 2)
    cb = conv_b.reshape(nch, 1, FF_CHUNK)
    return pl.pallas_call(
        functools.partial(_ffn_kernel, tl=tl),
        out_shape=jax.ShapeDtypeStruct(x1.shape, F32),
        grid=(bsz, seq // tl),
        in_specs=_halo_specs(tl, seq, d) + [_resident(wi.shape), _resident(cw.shape), _resident(cb.shape),
                                       _resident(w_out.shape), _resident(g.shape), _resident(b.shape)],
        out_specs=pl.BlockSpec((1, tl, d), lambda bb, i: (bb, i, 0)),
        scratch_shapes=[pltpu.VMEM((tl + 2 * HALO, FF_CHUNK), F32), pltpu.VMEM((tl, D_FF), BF16)],
        compiler_params=_params(("parallel", "parallel")),
        name="conv_ffn",
    )(x1, x1, x1, wi, cw, cb, w_out, g, b)


def _encoder_layer(x, p, tl=512, rb=8):
    seq = x.shape[1]
    q, k, v, zg, g0 = _in_proj(x, p["w_in"], p["short_w"], p["short_b"], tl)
    ma = _attention(q, k, v, p["bias"], p["g_attn"], rb)
    a, d = _filter_taps(seq, p["filt_w1"], p["filt_b1"], p["filt_freq"], p["filt_w_inner"],
                        p["filt_b_inner"], p["filt_w3"])
    cr, sr = _pair_coefficients(seq)
    t, tt = _dft_matrices(seq)
    gspec = _filter_spectrum(cr, sr, t, a, d)
    mh = _hyena_conv(zg, g0, cr, sr, t, tt, gspec, p["filt_bias"], p["g_hyena"])
    x1 = _merge(ma, mh, x, p["w_out"], p["ln1_g"], p["ln1_b"], tl)
    return _ffn(x1, p["ffn_w_in"], p["ffn_conv_w"], p["ffn_conv_b"], p["ffn_w_out"],
                p["ln2_g"], p["ln2_b"], tl)


def kernel(x_prompt, x_sample, w_in, short_w, short_b, rpb, filt_w1, filt_b1, filt_freq, filt_w_inner,
           filt_b_inner, filt_w3, filt_bias, g_attn, g_hyena, w_out, ln1_g, ln1_b, ffn_w_in, ffn_conv_w,
           ffn_conv_b, ffn_w_out, ln2_g, ln2_b):
    assert w_in.shape[0] == DEPTH == 1
    row = lambda a: a[0].reshape(1, -1)
    p = dict(
        w_in=w_in[0].astype(BF16), short_w=short_w[0], short_b=short_b[0],
        bias=_attention_bias(rpb[0]), g_attn=row(g_attn),
        filt_w1=filt_w1[0], filt_b1=filt_b1[0], filt_freq=filt_freq[0], filt_w_inner=filt_w_inner[0],
        filt_b_inner=filt_b_inner[0], filt_w3=filt_w3[0], filt_bias=filt_bias[0], g_hyena=row(g_hyena),
        w_out=w_out[0].astype(BF16), ln1_g=row(ln1_g), ln1_b=row(ln1_b),
        ffn_w_in=ffn_w_in[0].astype(BF16), ffn_conv_w=ffn_conv_w[0], ffn_conv_b=ffn_conv_b[0],
        ffn_w_out=ffn_w_out[0].astype(BF16), ln2_g=row(ln2_g), ln2_b=row(ln2_b),
    )
    return (_encoder_layer(x_prompt, p), _encoder_layer(x_sample, p))
```

```python
import functools
import math

import numpy as np
import jax
import jax.numpy as jnp
from jax import lax
from jax.experimental import pallas as pl
from jax.experimental.pallas import tpu as pltpu

F32 = jnp.float32
BF16 = jnp.bfloat16

D_MODEL = 1024
GRID_W = 64
ATTN_WIDTH = 512
HYENA_WIDTH = 512
HEAD_DIM = 64
N_HEADS = ATTN_WIDTH // HEAD_DIM
NA_ROWS = 8
NA_COLS = 16
FILTER_EMB = 33
FILTER_BANDS = (FILTER_EMB - 1) // 2
FILTER_HIDDEN = 64
FILTER_INNER = 2
MAX_DECAY = math.log(1e-2) / 0.3
MIN_DECAY = math.log(1e-2) / 1.5
D_FF = 2816
DEPTH = 1
ALPHA = (2 * DEPTH) ** 0.25
LN_EPS = 1e-5
RMS_EPS = 1e-6
LOG2E = math.log2(math.e)

SUBLANES = 8
LANES = 128
HALO = SUBLANES
VMEM_LIMIT = 56 * 1024 * 1024

HIGHEST = lax.Precision.HIGHEST


def _params(sem):
    return pltpu.CompilerParams(dimension_semantics=sem, vmem_limit_bytes=VMEM_LIMIT)


def _resident(shape):
    nd = len(shape)
    return pl.BlockSpec(shape, lambda *_: (0,) * nd, pipeline_mode=pl.Buffered(1))


def _layer_norm(y, g, b):
    mu = jnp.mean(y, axis=-1, keepdims=True)
    yc = y - mu
    var = jnp.mean(yc * yc, axis=-1, keepdims=True)
    return yc * lax.rsqrt(var + LN_EPS) * g + b


def _rms_norm(y, g):
    ms = jnp.mean(y * y, axis=-1, keepdims=True)
    return y * lax.rsqrt(ms + RMS_EPS) * g


def _halo_block(xp_ref, x_ref, xn_ref):
    i = pl.program_id(1)
    last = pl.num_programs(1) - 1
    prev = jnp.where(i > 0, xp_ref[0], 0.0)
    nxt = jnp.where(i < last, xn_ref[0], 0.0)
    return jnp.concatenate([prev, x_ref[0], nxt], axis=0).astype(BF16)


def _dwconv3(u_ref, w_ref, b_ref, tl):
    up = u_ref[pl.ds(HALO - 1, tl), :]
    uc = u_ref[pl.ds(HALO, tl), :]
    un = u_ref[pl.ds(HALO + 1, tl), :]
    return up * w_ref[0:1, :] + uc * w_ref[1:2, :] + un * w_ref[2:3, :] + b_ref[...]


def _halo_specs(tl, seq, d):
    nb = tl // HALO
    last = seq // HALO - 1
    return [
        pl.BlockSpec((1, HALO, d), lambda b, i: (b, jnp.maximum(i * nb - 1, 0), 0)),
        pl.BlockSpec((1, tl, d), lambda b, i: (b, i, 0)),
        pl.BlockSpec((1, HALO, d), lambda b, i: (b, jnp.minimum((i + 1) * nb, last), 0)),
    ]


def _in_proj_kernel(xp_ref, x_ref, xn_ref, w_ref, sw_ref, sb_ref,
                    q_ref, k_ref, v_ref, zg_ref, g0_ref, u_ref, *, tl):
    xq = x_ref[0].astype(BF16)
    proj = lambda n: jnp.dot(xq, w_ref[:, n * ATTN_WIDTH:(n + 1) * ATTN_WIDTH], preferred_element_type=F32)
    q_ref[0] = (proj(0) * (HEAD_DIM ** -0.5 * LOG2E)).astype(BF16)
    k_ref[0] = proj(1).astype(BF16)
    v = proj(2).astype(BF16)
    ones = jnp.ones((tl, LANES), BF16)
    v_ref[0] = jnp.concatenate([piece for p in range(N_HEADS // 2)
                                for piece in (v[:, p * LANES:(p + 1) * LANES], ones)], axis=1)
    xh = _halo_block(xp_ref, x_ref, xn_ref)
    conv = []
    for n in range(3):
        lo = 3 * ATTN_WIDTH + n * HYENA_WIDTH
        u_ref[...] = jnp.dot(xh, w_ref[:, lo:lo + HYENA_WIDTH], preferred_element_type=F32)
        conv.append(_dwconv3(u_ref, sw_ref.at[n], sb_ref.at[n], tl))
    g0_ref[0] = conv[0].astype(BF16)
    zg_ref[0] = (conv[2] * conv[1]).astype(BF16)


def _in_proj(x, w_in, short_w, short_b, tl):
    bsz, seq, d = x.shape
    sw = short_w.reshape(3, 3, HYENA_WIDTH).transpose(1, 0, 2)
    sb = short_b.reshape(3, 1, HYENA_WIDTH)
    out = jax.ShapeDtypeStruct((bsz, seq, ATTN_WIDTH), BF16)
    ospec = pl.BlockSpec((1, tl, ATTN_WIDTH), lambda b, i: (b, i, 0))
    vout = jax.ShapeDtypeStruct((bsz, seq, 2 * ATTN_WIDTH), BF16)
    vspec = pl.BlockSpec((1, tl, 2 * ATTN_WIDTH), lambda b, i: (b, i, 0))
    return pl.pallas_call(
        functools.partial(_in_proj_kernel, tl=tl),
        out_shape=(out, out, vout, out, out),
        grid=(bsz, seq // tl),
        in_specs=_halo_specs(tl, seq, d) + [_resident(w_in.shape), _resident(sw.shape), _resident(sb.shape)],
        out_specs=(ospec, ospec, vspec, ospec, ospec),
        scratch_shapes=[pltpu.VMEM((tl + 2 * HALO, HYENA_WIDTH), F32)],
        compiler_params=_params(("parallel", "parallel")),
        name="in_proj",
    )(x, x, x, w_in, sw, sb)


N_PAIRS = N_HEADS // 2
KEY_WIN = NA_ROWS * GRID_W
N_DR = 2 * NA_ROWS - 1
N_DC = 2 * NA_COLS - 1


def _bias_kernel(rpb_ref, ea_ref, eb_ref, o_ref, t_ref):
    row = lax.broadcasted_iota(jnp.int32, (GRID_W, LANES), 0)
    lane = lax.broadcasted_iota(jnp.int32, (GRID_W, LANES), 1)
    first = lane < GRID_W
    qcol = lane & (GRID_W - 1)
    col_start = jnp.clip(qcol - NA_COLS // 2, 0, GRID_W - NA_COLS)
    valid = (row >= col_start) & (row < col_start + NA_COLS)
    wa = jnp.dot(rpb_ref[0, 0], ea_ref[...], precision=HIGHEST, preferred_element_type=F32)
    wb = jnp.dot(rpb_ref[0, 1], eb_ref[...], precision=HIGHEST, preferred_element_type=F32)

    def rotate_rows(x):
        for t in range(GRID_W.bit_length() - 1):
            x = jnp.where(((row >> t) & 1) == 1, pltpu.roll(x, 1 << t, axis=1), x)
        return x

    for dr in range(N_DR):
        ta = rotate_rows(jnp.broadcast_to(wa[dr:dr + 1], (GRID_W, LANES)))
        tb = rotate_rows(jnp.broadcast_to(wb[dr:dr + 1], (GRID_W, LANES)))
        t_ref[dr] = jnp.where(valid, jnp.where(first, ta, tb) * LOG2E, -jnp.inf)
    for cls in range(NA_ROWS):
        for i in range(NA_ROWS):
            o_ref[0, cls, i * GRID_W:(i + 1) * GRID_W, :] = t_ref[i - cls + NA_ROWS - 1]


def _toeplitz_selectors():
    m = np.arange(LANES)
    dc = np.where(m < GRID_W, (NA_COLS - 1) - np.minimum(m, NA_COLS - 1),
                  np.minimum(LANES - m, NA_COLS - 1) + (NA_COLS - 1))
    ea = np.zeros((LANES, LANES), np.float32)
    ea[dc, m] = 1.0
    ea[:, GRID_W] = 0.0
    eb = np.roll(ea, GRID_W, axis=1)
    return ea, eb


def _attention_bias(rpb):
    ea, eb = _toeplitz_selectors()
    rp = jnp.pad(rpb.astype(F32), ((0, 0), (0, 16 - N_DR), (0, LANES - N_DC))).reshape(N_PAIRS, 2, 16, LANES)
    return pl.pallas_call(
        _bias_kernel,
        out_shape=jax.ShapeDtypeStruct((N_PAIRS, NA_ROWS, KEY_WIN, LANES), F32),
        grid=(N_PAIRS,),
        in_specs=[pl.BlockSpec((1, 2, 16, LANES), lambda p: (p, 0, 0, 0)),
                  _resident(ea.shape), _resident(eb.shape)],
        out_specs=pl.BlockSpec((1, NA_ROWS, KEY_WIN, LANES), lambda p: (p, 0, 0, 0)),
        scratch_shapes=[pltpu.VMEM((N_DR, GRID_W, LANES), F32)],
        compiler_params=_params(("parallel",)),
        name="attn_bias",
    )(rp, jnp.asarray(ea), jnp.asarray(eb))


def _attn_kernel(q_ref, k_ref, vx_ref, bias_ref, g_ref, o_ref, *, rows, rb, unroll):
    jb = pl.program_id(1)
    lane = lax.broadcasted_iota(jnp.int32, (GRID_W, LANES), 1)
    first = lane < HEAD_DIM
    keep = (first.astype(F32).astype(BF16), (~first).astype(F32).astype(BF16))
    nt = (((1,), (1,)), ((), ()))
    tn = (((0,), (0,)), ((), ()))

    def one_row(j):
        r = jb * rb + j
        start = jnp.clip(r - NA_ROWS // 2, 0, rows - NA_ROWS)
        cls = r - start
        qoff = pl.multiple_of(j * GRID_W, GRID_W)
        koff = pl.multiple_of(start * GRID_W, GRID_W)
        scores = []
        for p in range(N_PAIRS):
            sl = slice(p * LANES, (p + 1) * LANES)
            qp = q_ref[0, pl.ds(qoff, GRID_W), sl]
            qblk = jnp.concatenate([qp * keep[0], qp * keep[1]], axis=0)
            kp = k_ref[0, pl.ds(koff, KEY_WIN), sl]
            scores.append(lax.dot_general(kp, qblk, nt, preferred_element_type=F32))
        probs = []
        for p in range(N_PAIRS):
            s = scores[p] + bias_ref[p, cls]
            m = jnp.max(s, axis=0, keepdims=True)
            probs.append(jnp.exp2(s - m).astype(BF16))
        outs = []
        for p in range(N_PAIRS):
            vx = vx_ref[0, pl.ds(koff, KEY_WIN), 2 * p * LANES:2 * (p + 1) * LANES]
            o = lax.dot_general(probs[p], vx, tn, preferred_element_type=F32)
            oa = o[:GRID_W, :LANES] / o[:GRID_W, LANES:]
            ob = o[GRID_W:, :LANES] / o[GRID_W:, LANES:]
            outs.append(jnp.where(first, oa, ob))
        o_ref[0, pl.ds(qoff, GRID_W), :] = _rms_norm(jnp.concatenate(outs, axis=1), g_ref[...]).astype(BF16)

    def body(jj, carry):
        for u in range(unroll):
            one_row(jj * unroll + u)
        return carry

    lax.fori_loop(0, rb // unroll, body, 0)


def _attention(q, k, vx, bias, g_attn, rb, unroll=8):
    bsz, seq, w = q.shape
    rows = seq // GRID_W
    tq = rb * GRID_W
    return pl.pallas_call(
        functools.partial(_attn_kernel, rows=rows, rb=rb, unroll=unroll),
        out_shape=jax.ShapeDtypeStruct((bsz, seq, w), BF16),
        grid=(bsz, rows // rb),
        in_specs=[
            pl.BlockSpec((1, tq, w), lambda b, i: (b, i, 0)),
            pl.BlockSpec((1, seq, w), lambda b, i: (b, 0, 0)),
            pl.BlockSpec((1, seq, 2 * w), lambda b, i: (b, 0, 0)),
            _resident(bias.shape),
            _resident(g_attn.shape),
        ],
        out_specs=pl.BlockSpec((1, tq, w), lambda b, i: (b, i, 0)),
        compiler_params=_params(("parallel", "arbitrary")),
        name="attention",
    )(q, k, vx, bias, g_attn)


def _filter_kernel(w1t_ref, w1c_ref, w1s_ref, b1_ref, fq_ref, wi_ref, bi_ref, w3_ref,
                   a_ref, d_ref, cb_ref, sb_ref, h_ref, *, seq, tl):
    i = pl.program_id(0)
    rows = tl + SUBLANES
    band = lax.broadcasted_iota(jnp.int32, (rows, LANES), 1)
    freqs = jnp.where(band < FILTER_BANDS,
                      1e-4 + band.astype(F32) * ((FILTER_BANDS - 1 - 1e-4) / (FILTER_BANDS - 1)), 0.0)
    rad = freqs * (2.0 * math.pi / seq)
    local = lax.broadcasted_iota(jnp.int32, (rows, LANES), 0)

    @pl.when(i == 0)
    def _():
        cb_ref[...] = jnp.cos(local.astype(F32) * rad)
        sb_ref[...] = jnp.sin(local.astype(F32) * rad)

    base = (i * tl).astype(F32) * rad[0:SUBLANES]
    ca = jnp.cos(base)[0:1]
    sa = jnp.sin(base)[0:1]
    cos_ang = ca * cb_ref[...] - sa * sb_ref[...]
    sin_ang = sa * cb_ref[...] + ca * sb_ref[...]
    pos = local[:, 0:1] + i * tl
    t = pos.astype(F32) * (1.0 / (seq - 1))
    fq = fq_ref[...]
    pre = (t * w1t_ref[...]
           + jnp.dot(cos_ang, w1c_ref[...], precision=HIGHEST, preferred_element_type=F32)
           - jnp.dot(sin_ang, w1s_ref[...], precision=HIGHEST, preferred_element_type=F32)
           + b1_ref[...])
    h = jnp.sin(fq * pre)
    for n in range(FILTER_INNER):
        h = jnp.sin(fq * (jnp.dot(h, wi_ref[n], precision=HIGHEST, preferred_element_type=F32) + bi_ref[n]))
    chan = lax.broadcasted_iota(jnp.int32, (rows, HYENA_WIDTH), 1).astype(F32)
    deltas = jnp.abs(MIN_DECAY + chan * ((MAX_DECAY - MIN_DECAY) / (HYENA_WIDTH - 1)))
    decay = jnp.exp(-t * deltas)
    for n in range(2):
        cols = slice(n * HYENA_WIDTH, (n + 1) * HYENA_WIDTH)
        h_ref[:, cols] = jnp.dot(h, w3_ref[:, cols], precision=HIGHEST, preferred_element_type=F32) * decay
    h_fwd = h_ref[0:tl, 0:HYENA_WIDTH]
    h_bwd = jnp.where(pos[0:tl] + 1 < seq, h_ref[pl.ds(1, tl), HYENA_WIDTH:2 * HYENA_WIDTH], 0.0)
    a_ref[...] = (h_fwd + h_bwd).astype(BF16)
    d_ref[...] = (h_bwd - h_fwd).astype(BF16)


def _filter_taps(seq, w1, b1, freq, w_inner, b_inner, w3, tl=512):
    pad = LANES - FILTER_BANDS
    w1t = w1[0:1]
    w1c = jnp.pad(w1[1:1 + FILTER_BANDS], ((0, pad), (0, 0)))
    w1s = jnp.pad(w1[1 + FILTER_BANDS:], ((0, pad), (0, 0)))
    args = (w1t, w1c, w1s, b1.reshape(1, -1), freq.reshape(1, -1), w_inner,
            b_inner.reshape(FILTER_INNER, 1, -1), w3)
    out = jax.ShapeDtypeStruct((seq, HYENA_WIDTH), BF16)
    ospec = pl.BlockSpec((tl, HYENA_WIDTH), lambda i: (i, 0))
    return pl.pallas_call(
        functools.partial(_filter_kernel, seq=seq, tl=tl),
        out_shape=(out, out),
        grid=(seq // tl,),
        in_specs=[_resident(a.shape) for a in args],
        out_specs=(ospec, ospec),
        scratch_shapes=[pltpu.VMEM((tl + SUBLANES, LANES), F32), pltpu.VMEM((tl + SUBLANES, LANES), F32),
                        pltpu.VMEM((tl + SUBLANES, 2 * HYENA_WIDTH), F32)],
        compiler_params=_params(("arbitrary",)),
        name="hyena_filter",
    )(*args)


P_BLK = 512
F_HALF = P_BLK // 2


def _pair_coefficients(seq):
    s1n = seq // P_BLK
    g = np.arange(s1n)[:, None]
    s = np.arange(s1n)[None, :]
    ang = np.pi * (2 * g + 1) * s / (2 * s1n)
    return jnp.asarray(np.cos(ang), F32), jnp.asarray(np.sin(ang), F32)


def _dft_kernel(t_ref, tt_ref, cb_ref, sb_ref, *, seq):
    g = pl.program_id(0)
    s1n = seq // P_BLK

    @pl.when(g == 0)
    def _():
        f2 = lax.broadcasted_iota(jnp.int32, (F_HALF, P_BLK), 0)
        s_odd = 2 * lax.broadcasted_iota(jnp.int32, (F_HALF, P_BLK), 1) + 1
        beta = ((f2 * s_odd) & (2 * P_BLK - 1)).astype(F32) * (math.pi / P_BLK)
        cb_ref[...] = jnp.cos(beta)
        sb_ref[...] = jnp.sin(beta)

    cb = cb_ref[...]
    sb = sb_ref[...]
    s_odd = 2 * lax.broadcasted_iota(jnp.int32, (SUBLANES, P_BLK), 1) + 1

    def block(f1):
        alpha = ((s_odd * (2 * f1 + 1)) & (8 * seq - 1)).astype(F32) * (math.pi / (4 * seq))
        ca = jnp.cos(alpha)[0:1]
        sa = jnp.sin(alpha)[0:1]
        return ca * cb - sa * sb, sa * cb + ca * sb

    mra, mia = block(g)
    mrb, mib = block(2 * s1n - 1 - g)
    for r, (left, right) in enumerate(((mra, -mia), (mia, mra), (mrb, mib), (mib, -mrb))):
        rows = slice(r * F_HALF, (r + 1) * F_HALF)
        t_ref[0, rows, :P_BLK] = left.astype(BF16)
        t_ref[0, rows, P_BLK:] = right.astype(BF16)
        tt_ref[0, :P_BLK, rows] = left.T.astype(BF16)
        tt_ref[0, P_BLK:, rows] = right.T.astype(BF16)


def _dft_matrices(seq):
    s1n = seq // P_BLK
    mat = jax.ShapeDtypeStruct((s1n, 2 * P_BLK, 2 * P_BLK), BF16)
    spec = pl.BlockSpec((1, 2 * P_BLK, 2 * P_BLK), lambda g: (g, 0, 0))
    return pl.pallas_call(
        functools.partial(_dft_kernel, seq=seq),
        out_shape=(mat, mat),
        grid=(s1n,),
        out_specs=(spec, spec),
        scratch_shapes=[pltpu.VMEM((F_HALF, P_BLK), F32), pltpu.VMEM((F_HALF, P_BLK), F32)],
        compiler_params=_params(("arbitrary",)),
        name="dft_matrices",
    )()


_SMEM = pl.BlockSpec(memory_space=pltpu.SMEM)


def _stacked_blocks(cr_ref, sr_ref, g, block, s1n):
    ar = cr_ref[g, 0] * block(0)
    bi = sr_ref[g, 0] * block(0)
    for s in range(1, s1n):
        ar = ar + cr_ref[g, s] * block(s)
        bi = bi + sr_ref[g, s] * block(s)
    return jnp.concatenate([ar, bi], axis=0).astype(BF16)


def _spectrum_kernel(cr_ref, sr_ref, t_ref, a_ref, d_ref, g_ref, *, seq):
    g = pl.program_id(0)
    s1n = seq // P_BLK
    time_block = lambda ref: (lambda s: ref[s * P_BLK:(s + 1) * P_BLK, :].astype(F32))
    t_re = jnp.concatenate([t_ref[0, 0:F_HALF], t_ref[0, P_BLK:P_BLK + F_HALF]], axis=0)
    t_im = jnp.concatenate([t_ref[0, F_HALF:P_BLK], t_ref[0, P_BLK + F_HALF:2 * P_BLK]], axis=0)
    ka = jnp.dot(t_re, _stacked_blocks(cr_ref, sr_ref, g, time_block(a_ref), s1n),
                 preferred_element_type=F32)
    kd = jnp.dot(t_im, _stacked_blocks(cr_ref, sr_ref, g, time_block(d_ref), s1n),
                 preferred_element_type=F32)
    f2 = lax.broadcasted_iota(jnp.int32, (F_HALF, LANES), 0)
    reps = HYENA_WIDTH // LANES
    for blk, f1 in ((0, g), (1, 2 * s1n - 1 - g)):
        lo = blk * P_BLK
        phi = (2 * (f1 + 2 * s1n * f2) + 1).astype(F32) * (math.pi / (4 * seq))
        c = pltpu.repeat(jnp.cos(phi), reps, axis=1) * (1.0 / seq)
        s = pltpu.repeat(jnp.sin(phi), reps, axis=1) * (1.0 / seq)
        kr = ka[blk * F_HALF:(blk + 1) * F_HALF]
        ki = kd[blk * F_HALF:(blk + 1) * F_HALF]
        g_ref[0, lo:lo + F_HALF] = c * kr - s * ki
        g_ref[0, lo + F_HALF:lo + P_BLK] = c * ki + s * kr


def _filter_spectrum(cr, sr, t, a, d):
    seq = a.shape[0]
    s1n = seq // P_BLK
    return pl.pallas_call(
        functools.partial(_spectrum_kernel, seq=seq),
        out_shape=jax.ShapeDtypeStruct((s1n, 2 * P_BLK, HYENA_WIDTH), F32),
        grid=(s1n,),
        in_specs=[_SMEM, _SMEM, pl.BlockSpec((1, 2 * P_BLK, 2 * P_BLK), lambda g: (g, 0, 0)),
                  _resident(a.shape), _resident(d.shape)],
        out_specs=pl.BlockSpec((1, 2 * P_BLK, HYENA_WIDTH), lambda g: (g, 0, 0)),
        compiler_params=_params(("parallel",)),
        name="hyena_spectrum",
    )(cr, sr, t, a, d)


def _hy_fwd_kernel(cr_ref, sr_ref, t_ref, g_ref, z_ref, y_ref, *, s1n):
    g = pl.program_id(0)
    x = _stacked_blocks(cr_ref, sr_ref, g,
                        lambda s: z_ref[0, s * P_BLK:(s + 1) * P_BLK, :].astype(F32), s1n)
    res = jnp.dot(t_ref[0], x, preferred_element_type=F32)
    for blk in range(2):
        lo = blk * P_BLK
        zr = res[lo:lo + F_HALF]
        wi = res[lo + F_HALF:lo + P_BLK]
        gr = g_ref[0, lo:lo + F_HALF]
        gi = g_ref[0, lo + F_HALF:lo + P_BLK]
        y_ref[0, 0, lo:lo + F_HALF] = (gr * zr + gi * wi).astype(BF16)
        y_ref[0, 0, lo + F_HALF:lo + P_BLK] = (gr * wi - gi * zr).astype(BF16)


def _hy_inv_kernel(cr_ref, sr_ref, tt_ref, y_ref, z_ref, g0_ref, fb_ref, gn_ref, o_ref,
                   acc_ref, zs_ref, g0s_ref, *, s1n):
    g = pl.program_id(1)

    @pl.when(g == 0)
    def _():
        acc_ref[...] = jnp.zeros_like(acc_ref)

    zs_ref[pl.ds(pl.multiple_of(g * P_BLK, P_BLK), P_BLK), :] = z_ref[0]
    g0s_ref[pl.ds(pl.multiple_of(g * P_BLK, P_BLK), P_BLK), :] = g0_ref[0]
    uv = jnp.dot(tt_ref[g], y_ref[0, 0], preferred_element_type=F32)
    u = uv[:P_BLK]
    vn = uv[P_BLK:]
    for t1 in range(s1n):
        rows = slice(t1 * P_BLK, (t1 + 1) * P_BLK)
        acc_ref[rows, :] += cr_ref[g, t1] * u + sr_ref[g, t1] * vn

    @pl.when(g == s1n - 1)
    def _():
        for t1 in range(s1n):
            rows = slice(t1 * P_BLK, (t1 + 1) * P_BLK)
            y = (acc_ref[rows, :] + zs_ref[rows, :].astype(F32) * fb_ref[...]) * g0s_ref[rows, :].astype(F32)
            o_ref[0, rows, :] = _rms_norm(y, gn_ref[...]).astype(BF16)


def _hyena_conv(zg, g0, cr, sr, t, tt, gspec, fbias, g_hyena):
    bsz, seq, w = zg.shape
    s1n = seq // P_BLK
    mat = pl.BlockSpec((1, 2 * P_BLK, 2 * P_BLK), lambda g, b: (g, 0, 0))
    spec = pl.BlockSpec((1, 2 * P_BLK, w), lambda g, b: (g, 0, 0))
    y = pl.pallas_call(
        functools.partial(_hy_fwd_kernel, s1n=s1n),
        out_shape=jax.ShapeDtypeStruct((bsz, s1n, 2 * P_BLK, w), BF16),
        grid=(s1n, bsz),
        in_specs=[_SMEM, _SMEM, mat, spec, pl.BlockSpec((1, seq, w), lambda g, b: (b, 0, 0))],
        out_specs=pl.BlockSpec((1, 1, 2 * P_BLK, w), lambda g, b: (b, g, 0, 0)),
        compiler_params=_params(("parallel", "parallel")),
        name="hyena_fwd",
    )(cr, sr, t, gspec, zg)
    blk = pl.BlockSpec((1, P_BLK, w), lambda b, g: (b, g, 0))
    return pl.pallas_call(
        functools.partial(_hy_inv_kernel, s1n=s1n),
        out_shape=jax.ShapeDtypeStruct((bsz, seq, w), BF16),
        grid=(bsz, s1n),
        in_specs=[_SMEM, _SMEM, _resident(tt.shape),
                  pl.BlockSpec((1, 1, 2 * P_BLK, w), lambda b, g: (b, g, 0, 0)),
                  blk, blk, _resident(fbias.shape), _resident(g_hyena.shape)],
        out_specs=pl.BlockSpec((1, seq, w), lambda b, g: (b, 0, 0)),
        scratch_shapes=[pltpu.VMEM((seq, w), F32), pltpu.VMEM((seq, w), BF16), pltpu.VMEM((seq, w), BF16)],
        compiler_params=_params(("parallel", "arbitrary")),
        name="hyena_inv",
    )(cr, sr, tt, y, zg, g0, fbias, g_hyena)


def _merge_kernel(ma_ref, mh_ref, x_ref, wa_ref, wh_ref, g_ref, b_ref, o_ref):
    y = (jnp.dot(ma_ref[0], wa_ref[...], preferred_element_type=F32)
         + jnp.dot(mh_ref[0], wh_ref[...], preferred_element_type=F32))
    o_ref[0] = _layer_norm(ALPHA * x_ref[0] + y, g_ref[...], b_ref[...])


def _merge(ma, mh, x, w_out, g, b, tl):
    bsz, seq, d = x.shape
    wa, wh = w_out[:ATTN_WIDTH], w_out[ATTN_WIDTH:]
    half = pl.BlockSpec((1, tl, ATTN_WIDTH), lambda bb, i: (bb, i, 0))
    full = pl.BlockSpec((1, tl, d), lambda bb, i: (bb, i, 0))
    return pl.pallas_call(
        _merge_kernel,
        out_shape=jax.ShapeDtypeStruct(x.shape, F32),
        grid=(bsz, seq // tl),
        in_specs=[half, half, full, _resident(wa.shape), _resident(wh.shape),
                  _resident(g.shape), _resident(b.shape)],
        out_specs=full,
        compiler_params=_params(("parallel", "parallel")),
        name="merge_ln1",
    )(ma, mh, x, wa, wh, g, b)


FF_CHUNK = 256
N_FF_CHUNKS = D_FF // FF_CHUNK


def _ffn_kernel(xp_ref, x_ref, xn_ref, wi_ref, cw_ref, cb_ref, wo_ref, g_ref, b_ref,
                o_ref, u_ref, hid_ref, *, tl):
    xh = _halo_block(xp_ref, x_ref, xn_ref)
    for j in range(N_FF_CHUNKS):
        conv = []
        for part in range(2):
            n = part * N_FF_CHUNKS + j
            u_ref[...] = jnp.dot(xh, wi_ref[n], preferred_element_type=F32)
            conv.append(_dwconv3(u_ref, cw_ref.at[n], cb_ref.at[n], tl))
        gate = conv[1]
        gelu = 0.5 * gate * (1.0 + lax.erf(gate * (2.0 ** -0.5)))
        hid_ref[:, j * FF_CHUNK:(j + 1) * FF_CHUNK] = (conv[0] * gelu).astype(BF16)
    y = jnp.dot(hid_ref[...], wo_ref[...], preferred_element_type=F32)
    o_ref[0] = _layer_norm(ALPHA * x_ref[0] + y, g_ref[...], b_ref[...])


def _ffn(x1, w_in, conv_w, conv_b, w_out, g, b, tl):
    bsz, seq, d = x1.shape
    nch = 2 * N_FF_CHUNKS
    wi = w_in.reshape(d, nch, FF_CHUNK).transpose(1, 0, 2)
    cw = conv_w.reshape(3, nch, FF_CHUNK).transpose(1, 0, 2)
    cb = conv_b.reshape(nch, 1, FF_CHUNK)
    return pl.pallas_call(
        functools.partial(_ffn_kernel, tl=tl),
        out_shape=jax.ShapeDtypeStruct(x1.shape, F32),
        grid=(bsz, seq // tl),
        in_specs=_halo_specs(tl, seq, d) + [_resident(wi.shape), _resident(cw.shape), _resident(cb.shape),
                                       _resident(w_out.shape), _resident(g.shape), _resident(b.shape)],
        out_specs=pl.BlockSpec((1, tl, d), lambda bb, i: (bb, i, 0)),
        scratch_shapes=[pltpu.VMEM((tl + 2 * HALO, FF_CHUNK), F32), pltpu.VMEM((tl, D_FF), BF16)],
        compiler_params=_params(("parallel", "parallel")),
        name="conv_ffn",
    )(x1, x1, x1, wi, cw, cb, w_out, g, b)


def _encoder_layer(x, p, tl=512, rb=8):
    seq = x.shape[1]
    q, k, v, zg, g0 = _in_proj(x, p["w_in"], p["short_w"], p["short_b"], tl)
    ma = _attention(q, k, v, p["bias"], p["g_attn"], rb)
    a, d = _filter_taps(seq, p["filt_w1"], p["filt_b1"], p["filt_freq"], p["filt_w_inner"],
                        p["filt_b_inner"], p["filt_w3"])
    cr, sr = _pair_coefficients(seq)
    t, tt = _dft_matrices(seq)
    gspec = _filter_spectrum(cr, sr, t, a, d)
    mh = _hyena_conv(zg, g0, cr, sr, t, tt, gspec, p["filt_bias"], p["g_hyena"])
    x1 = _merge(ma, mh, x, p["w_out"], p["ln1_g"], p["ln1_b"], tl)
    return _ffn(x1, p["ffn_w_in"], p["ffn_conv_w"], p["ffn_conv_b"], p["ffn_w_out"],
                p["ln2_g"], p["ln2_b"], tl)


def kernel(x_prompt, x_sample, w_in, short_w, short_b, rpb, filt_w1, filt_b1, filt_freq, filt_w_inner,
           filt_b_inner, filt_w3, filt_bias, g_attn, g_hyena, w_out, ln1_g, ln1_b, ffn_w_in, ffn_conv_w,
           ffn_conv_b, ffn_w_out, ln2_g, ln2_b):
    assert w_in.shape[0] == DEPTH == 1
    row = lambda a: a[0].reshape(1, -1)
    p = dict(
        w_in=w_in[0].astype(BF16), short_w=short_w[0], short_b=short_b[0],
        bias=_attention_bias(rpb[0]), g_attn=row(g_attn),
        filt_w1=filt_w1[0], filt_b1=filt_b1[0], filt_freq=filt_freq[0], filt_w_inner=filt_w_inner[0],
        filt_b_inner=filt_b_inner[0], filt_w3=filt_w3[0], filt_bias=filt_bias[0], g_hyena=row(g_hyena),
        w_out=w_out[0].astype(BF16), ln1_g=row(ln1_g), ln1_b=row(ln1_b),
        ffn_w_in=ffn_w_in[0].astype(BF16), ffn_conv_w=ffn_conv_w[0], ffn_conv_b=ffn_conv_b[0],
        ffn_w_out=ffn_w_out[0].astype(BF16), ln2_g=row(ln2_g), ln2_b=row(ln2_b),
    )
    return (_encoder_layer(x_prompt, p), _encoder_layer(x_sample, p))
```

```python
import functools
import math

import numpy as np
import jax
import jax.numpy as jnp
from jax import lax
from jax.experimental import pallas as pl
from jax.experimental.pallas import tpu as pltpu

F32 = jnp.float32
BF16 = jnp.bfloat16

D_MODEL = 1024
GRID_W = 64
ATTN_WIDTH = 512
HYENA_WIDTH = 512
HEAD_DIM = 64
N_HEADS = ATTN_WIDTH // HEAD_DIM
NA_ROWS = 8
NA_COLS = 16
FILTER_EMB = 33
FILTER_BANDS = (FILTER_EMB - 1) // 2
FILTER_HIDDEN = 64
FILTER_INNER = 2
MAX_DECAY = math.log(1e-2) / 0.3
MIN_DECAY = math.log(1e-2) / 1.5
D_FF = 2816
DEPTH = 1
ALPHA = (2 * DEPTH) ** 0.25
LN_EPS = 1e-5
RMS_EPS = 1e-6
LOG2E = math.log2(math.e)

SUBLANES = 8
LANES = 128
HALO = SUBLANES
VMEM_LIMIT = 56 * 1024 * 1024

HIGHEST = lax.Precision.HIGHEST


def _params(sem):
    return pltpu.CompilerParams(dimension_semantics=sem, vmem_limit_bytes=VMEM_LIMIT)


def _resident(shape):
    nd = len(shape)
    return pl.BlockSpec(shape, lambda *_: (0,) * nd, pipeline_mode=pl.Buffered(1))


def _layer_norm(y, g, b):
    mu = jnp.mean(y, axis=-1, keepdims=True)
    yc = y - mu
    var = jnp.mean(yc * yc, axis=-1, keepdims=True)
    return yc * lax.rsqrt(var + LN_EPS) * g + b


def _rms_norm(y, g):
    ms = jnp.mean(y * y, axis=-1, keepdims=True)
    return y * lax.rsqrt(ms + RMS_EPS) * g


def _halo_block(xp_ref, x_ref, xn_ref):
    i = pl.program_id(1)
    last = pl.num_programs(1) - 1
    prev = jnp.where(i > 0, xp_ref[0], 0.0)
    nxt = jnp.where(i < last, xn_ref[0], 0.0)
    return jnp.concatenate([prev, x_ref[0], nxt], axis=0).astype(BF16)


def _dwconv3(u_ref, w_ref, b_ref, tl):
    up = u_ref[pl.ds(HALO - 1, tl), :]
    uc = u_ref[pl.ds(HALO, tl), :]
    un = u_ref[pl.ds(HALO + 1, tl), :]
    return up * w_ref[0:1, :] + uc * w_ref[1:2, :] + un * w_ref[2:3, :] + b_ref[...]


def _halo_specs(tl, seq, d):
    nb = tl // HALO
    last = seq // HALO - 1
    return [
        pl.BlockSpec((1, HALO, d), lambda b, i: (b, jnp.maximum(i * nb - 1, 0), 0)),
        pl.BlockSpec((1, tl, d), lambda b, i: (b, i, 0)),
        pl.BlockSpec((1, HALO, d), lambda b, i: (b, jnp.minimum((i + 1) * nb, last), 0)),
    ]


def _in_proj_kernel(xp_ref, x_ref, xn_ref, w_ref, sw_ref, sb_ref,
                    q_ref, k_ref, v_ref, zg_ref, g0_ref, u_ref, *, tl):
    xq = x_ref[0].astype(BF16)
    proj = lambda n: jnp.dot(xq, w_ref[:, n * ATTN_WIDTH:(n + 1) * ATTN_WIDTH], preferred_element_type=F32)
    q_ref[0] = (proj(0) * (HEAD_DIM ** -0.5 * LOG2E)).astype(BF16)
    k_ref[0] = proj(1).astype(BF16)
    v = proj(2).astype(BF16)
    ones = jnp.ones((tl, LANES), BF16)
    v_ref[0] = jnp.concatenate([piece for p in range(N_HEADS // 2)
                                for piece in (v[:, p * LANES:(p + 1) * LANES], ones)], axis=1)
    xh = _halo_block(xp_ref, x_ref, xn_ref)
    conv = []
    for n in range(3):
        lo = 3 * ATTN_WIDTH + n * HYENA_WIDTH
        u_ref[...] = jnp.dot(xh, w_ref[:, lo:lo + HYENA_WIDTH], preferred_element_type=F32)
        conv.append(_dwconv3(u_ref, sw_ref.at[n], sb_ref.at[n], tl))
    g0_ref[0] = conv[0].astype(BF16)
    zg_ref[0] = (conv[2] * conv[1]).astype(BF16)


def _in_proj(x, w_in, short_w, short_b, tl):
    bsz, seq, d = x.shape
    sw = short_w.reshape(3, 3, HYENA_WIDTH).transpose(1, 0, 2)
    sb = short_b.reshape(3, 1, HYENA_WIDTH)
    out = jax.ShapeDtypeStruct((bsz, seq, ATTN_WIDTH), BF16)
    ospec = pl.BlockSpec((1, tl, ATTN_WIDTH), lambda b, i: (b, i, 0))
    vout = jax.ShapeDtypeStruct((bsz, seq, 2 * ATTN_WIDTH), BF16)
    vspec = pl.BlockSpec((1, tl, 2 * ATTN_WIDTH), lambda b, i: (b, i, 0))
    return pl.pallas_call(
        functools.partial(_in_proj_kernel, tl=tl),
        out_shape=(out, out, vout, out, out),
        grid=(bsz, seq // tl),
        in_specs=_halo_specs(tl, seq, d) + [_resident(w_in.shape), _resident(sw.shape), _resident(sb.shape)],
        out_specs=(ospec, ospec, vspec, ospec, ospec),
        scratch_shapes=[pltpu.VMEM((tl + 2 * HALO, HYENA_WIDTH), F32)],
        compiler_params=_params(("parallel", "parallel")),
        name="in_proj",
    )(x, x, x, w_in, sw, sb)


N_PAIRS = N_HEADS // 2
KEY_WIN = NA_ROWS * GRID_W
N_DR = 2 * NA_ROWS - 1
N_DC = 2 * NA_COLS - 1


def _bias_kernel(rpb_ref, ea_ref, eb_ref, o_ref, t_ref):
    row = lax.broadcasted_iota(jnp.int32, (GRID_W, LANES), 0)
    lane = lax.broadcasted_iota(jnp.int32, (GRID_W, LANES), 1)
    first = lane < GRID_W
    qcol = lane & (GRID_W - 1)
    col_start = jnp.clip(qcol - NA_COLS // 2, 0, GRID_W - NA_COLS)
    valid = (row >= col_start) & (row < col_start + NA_COLS)
    wa = jnp.dot(rpb_ref[0, 0], ea_ref[...], precision=HIGHEST, preferred_element_type=F32)
    wb = jnp.dot(rpb_ref[0, 1], eb_ref[...], precision=HIGHEST, preferred_element_type=F32)

    def rotate_rows(x):
        for t in range(GRID_W.bit_length() - 1):
            x = jnp.where(((row >> t) & 1) == 1, pltpu.roll(x, 1 << t, axis=1), x)
        return x

    for dr in range(N_DR):
        ta = rotate_rows(jnp.broadcast_to(wa[dr:dr + 1], (GRID_W, LANES)))
        tb = rotate_rows(jnp.broadcast_to(wb[dr:dr + 1], (GRID_W, LANES)))
        t_ref[dr] = jnp.where(valid, jnp.where(first, ta, tb) * LOG2E, -jnp.inf)
    for cls in range(NA_ROWS):
        for i in range(NA_ROWS):
            o_ref[0, cls, i * GRID_W:(i + 1) * GRID_W, :] = t_ref[i - cls + NA_ROWS - 1]


def _toeplitz_selectors():
    m = np.arange(LANES)
    dc = np.where(m < GRID_W, (NA_COLS - 1) - np.minimum(m, NA_COLS - 1),
                  np.minimum(LANES - m, NA_COLS - 1) + (NA_COLS - 1))
    ea = np.zeros((LANES, LANES), np.float32)
    ea[dc, m] = 1.0
    ea[:, GRID_W] = 0.0
    eb = np.roll(ea, GRID_W, axis=1)
    return ea, eb


def _attention_bias(rpb):
    ea, eb = _toeplitz_selectors()
    rp = jnp.pad(rpb.astype(F32), ((0, 0), (0, 16 - N_DR), (0, LANES - N_DC))).reshape(N_PAIRS, 2, 16, LANES)
    return pl.pallas_call(
        _bias_kernel,
        out_shape=jax.ShapeDtypeStruct((N_PAIRS, NA_ROWS, KEY_WIN, LANES), F32),
        grid=(N_PAIRS,),
        in_specs=[pl.BlockSpec((1, 2, 16, LANES), lambda p: (p, 0, 0, 0)),
                  _resident(ea.shape), _resident(eb.shape)],
        out_specs=pl.BlockSpec((1, NA_ROWS, KEY_WIN, LANES), lambda p: (p, 0, 0, 0)),
        scratch_shapes=[pltpu.VMEM((N_DR, GRID_W, LANES), F32)],
        compiler_params=_params(("parallel",)),
        name="attn_bias",
    )(rp, jnp.asarray(ea), jnp.asarray(eb))


def _attn_kernel(q_ref, k_ref, vx_ref, bias_ref, g_ref, o_ref, *, rows, rb, unroll):
    jb = pl.program_id(1)
    lane = lax.broadcasted_iota(jnp.int32, (GRID_W, LANES), 1)
    first = lane < HEAD_DIM
    keep = (first.astype(F32).astype(BF16), (~first).astype(F32).astype(BF16))
    nt = (((1,), (1,)), ((), ()))
    tn = (((0,), (0,)), ((), ()))

    def one_row(j):
        r = jb * rb + j
        start = jnp.clip(r - NA_ROWS // 2, 0, rows - NA_ROWS)
        cls = r - start
        qoff = pl.multiple_of(j * GRID_W, GRID_W)
        koff = pl.multiple_of(start * GRID_W, GRID_W)
        scores = []
        for p in range(N_PAIRS):
            sl = slice(p * LANES, (p + 1) * LANES)
            qp = q_ref[0, pl.ds(qoff, GRID_W), sl]
            qblk = jnp.concatenate([qp * keep[0], qp * keep[1]], axis=0)
            kp = k_ref[0, pl.ds(koff, KEY_WIN), sl]
            scores.append(lax.dot_general(kp, qblk, nt, preferred_element_type=F32))
        probs = []
        for p in range(N_PAIRS):
            s = scores[p] + bias_ref[p, cls]
            m = jnp.max(s, axis=0, keepdims=True)
            probs.append(jnp.exp2(s - m).astype(BF16))
        outs = []
        for p in range(N_PAIRS):
            vx = vx_ref[0, pl.ds(koff, KEY_WIN), 2 * p * LANES:2 * (p + 1) * LANES]
            o = lax.dot_general(probs[p], vx, tn, preferred_element_type=F32)
            oa = o[:GRID_W, :LANES] / o[:GRID_W, LANES:]
            ob = o[GRID_W:, :LANES] / o[GRID_W:, LANES:]
            outs.append(jnp.where(first, oa, ob))
        o_ref[0, pl.ds(qoff, GRID_W), :] = _rms_norm(jnp.concatenate(outs, axis=1), g_ref[...]).astype(BF16)

    def body(jj, carry):
        for u in range(unroll):
            one_row(jj * unroll + u)
        return carry

    lax.fori_loop(0, rb // unroll, body, 0)


def _attention(q, k, vx, bias, g_attn, rb, unroll=8):
    bsz, seq, w = q.shape
    rows = seq // GRID_W
    tq = rb * GRID_W
    return pl.pallas_call(
        functools.partial(_attn_kernel, rows=rows, rb=rb, unroll=unroll),
        out_shape=jax.ShapeDtypeStruct((bsz, seq, w), BF16),
        grid=(bsz, rows // rb),
        in_specs=[
            pl.BlockSpec((1, tq, w), lambda b, i: (b, i, 0)),
            pl.BlockSpec((1, seq, w), lambda b, i: (b, 0, 0)),
            pl.BlockSpec((1, seq, 2 * w), lambda b, i: (b, 0, 0)),
            _resident(bias.shape),
            _resident(g_attn.shape),
        ],
        out_specs=pl.BlockSpec((1, tq, w), lambda b, i: (b, i, 0)),
        compiler_params=_params(("parallel", "arbitrary")),
        name="attention",
    )(q, k, vx, bias, g_attn)


def _filter_kernel(w1t_ref, w1c_ref, w1s_ref, b1_ref, fq_ref, wi_ref, bi_ref, w3_ref,
                   a_ref, d_ref, cb_ref, sb_ref, h_ref, *, seq, tl):
    i = pl.program_id(0)
    rows = tl + SUBLANES
    band = lax.broadcasted_iota(jnp.int32, (rows, LANES), 1)
    freqs = jnp.where(band < FILTER_BANDS,
                      1e-4 + band.astype(F32) * ((FILTER_BANDS - 1 - 1e-4) / (FILTER_BANDS - 1)), 0.0)
    rad = freqs * (2.0 * math.pi / seq)
    local = lax.broadcasted_iota(jnp.int32, (rows, LANES), 0)

    @pl.when(i == 0)
    def _():
        cb_ref[...] = jnp.cos(local.astype(F32) * rad)
        sb_ref[...] = jnp.sin(local.astype(F32) * rad)

    base = (i * tl).astype(F32) * rad[0:SUBLANES]
    ca = jnp.cos(base)[0:1]
    sa = jnp.sin(base)[0:1]
    cos_ang = ca * cb_ref[...] - sa * sb_ref[...]
    sin_ang = sa * cb_ref[...] + ca * sb_ref[...]
    pos = local[:, 0:1] + i * tl
    t = pos.astype(F32) * (1.0 / (seq - 1))
    fq = fq_ref[...]
    pre = (t * w1t_ref[...]
           + jnp.dot(cos_ang, w1c_ref[...], precision=HIGHEST, preferred_element_type=F32)
           - jnp.dot(sin_ang, w1s_ref[...], precision=HIGHEST, preferred_element_type=F32)
           + b1_ref[...])
    h = jnp.sin(fq * pre)
    for n in range(FILTER_INNER):
        h = jnp.sin(fq * (jnp.dot(h, wi_ref[n], precision=HIGHEST, preferred_element_type=F32) + bi_ref[n]))
    chan = lax.broadcasted_iota(jnp.int32, (rows, HYENA_WIDTH), 1).astype(F32)
    deltas = jnp.abs(MIN_DECAY + chan * ((MAX_DECAY - MIN_DECAY) / (HYENA_WIDTH - 1)))
    decay = jnp.exp(-t * deltas)
    for n in range(2):
        cols = slice(n * HYENA_WIDTH, (n + 1) * HYENA_WIDTH)
        h_ref[:, cols] = jnp.dot(h, w3_ref[:, cols], precision=HIGHEST, preferred_element_type=F32) * decay
    h_fwd = h_ref[0:tl, 0:HYENA_WIDTH]
    h_bwd = jnp.where(pos[0:tl] + 1 < seq, h_ref[pl.ds(1, tl), HYENA_WIDTH:2 * HYENA_WIDTH], 0.0)
    a_ref[...] = (h_fwd + h_bwd).astype(BF16)
    d_ref[...] = (h_bwd - h_fwd).astype(BF16)


def _filter_taps(seq, w1, b1, freq, w_inner, b_inner, w3, tl=512):
    pad = LANES - FILTER_BANDS
    w1t = w1[0:1]
    w1c = jnp.pad(w1[1:1 + FILTER_BANDS], ((0, pad), (0, 0)))
    w1s = jnp.pad(w1[1 + FILTER_BANDS:], ((0, pad), (0, 0)))
    args = (w1t, w1c, w1s, b1.reshape(1, -1), freq.reshape(1, -1), w_inner,
            b_inner.reshape(FILTER_INNER, 1, -1), w3)
    out = jax.ShapeDtypeStruct((seq, HYENA_WIDTH), BF16)
    ospec = pl.BlockSpec((tl, HYENA_WIDTH), lambda i: (i, 0))
    return pl.pallas_call(
        functools.partial(_filter_kernel, seq=seq, tl=tl),
        out_shape=(out, out),
        grid=(seq // tl,),
        in_specs=[_resident(a.shape) for a in args],
        out_specs=(ospec, ospec),
        scratch_shapes=[pltpu.VMEM((tl + SUBLANES, LANES), F32), pltpu.VMEM((tl + SUBLANES, LANES), F32),
                        pltpu.VMEM((tl + SUBLANES, 2 * HYENA_WIDTH), F32)],
        compiler_params=_params(("arbitrary",)),
        name="hyena_filter",
    )(*args)


P_BLK = 512
F_HALF = P_BLK // 2


def _pair_coefficients(seq):
    s1n = seq // P_BLK
    g = np.arange(s1n)[:, None]
    s = np.arange(s1n)[None, :]
    ang = np.pi * (2 * g + 1) * s / (2 * s1n)
    return jnp.asarray(np.cos(ang), F32), jnp.asarray(np.sin(ang), F32)


def _dft_kernel(t_ref, tt_ref, cb_ref, sb_ref, *, seq):
    g = pl.program_id(0)
    s1n = seq // P_BLK

    @pl.when(g == 0)
    def _():
        f2 = lax.broadcasted_iota(jnp.int32, (F_HALF, P_BLK), 0)
        s_odd = 2 * lax.broadcasted_iota(jnp.int32, (F_HALF, P_BLK), 1) + 1
        beta = ((f2 * s_odd) & (2 * P_BLK - 1)).astype(F32) * (math.pi / P_BLK)
        cb_ref[...] = jnp.cos(beta)
        sb_ref[...] = jnp.sin(beta)

    cb = cb_ref[...]
    sb = sb_ref[...]
    s_odd = 2 * lax.broadcasted_iota(jnp.int32, (SUBLANES, P_BLK), 1) + 1

    def block(f1):
        alpha = ((s_odd * (2 * f1 + 1)) & (8 * seq - 1)).astype(F32) * (math.pi / (4 * seq))
        ca = jnp.cos(alpha)[0:1]
        sa = jnp.sin(alpha)[0:1]
        return ca * cb - sa * sb, sa * cb + ca * sb

    mra, mia = block(g)
    mrb, mib = block(2 * s1n - 1 - g)
    for r, (left, right) in enumerate(((mra, -mia), (mia, mra), (mrb, mib), (mib, -mrb))):
        rows = slice(r * F_HALF, (r + 1) * F_HALF)
        t_ref[0, rows, :P_BLK] = left.astype(BF16)
        t_ref[0, rows, P_BLK:] = right.astype(BF16)
        tt_ref[0, :P_BLK, rows] = left.T.astype(BF16)
        tt_ref[0, P_BLK:, rows] = right.T.astype(BF16)


def _dft_matrices(seq):
    s1n = seq // P_BLK
    mat = jax.ShapeDtypeStruct((s1n, 2 * P_BLK, 2 * P_BLK), BF16)
    spec = pl.BlockSpec((1, 2 * P_BLK, 2 * P_BLK), lambda g: (g, 0, 0))
    return pl.pallas_call(
        functools.partial(_dft_kernel, seq=seq),
        out_shape=(mat, mat),
        grid=(s1n,),
        out_specs=(spec, spec),
        scratch_shapes=[pltpu.VMEM((F_HALF, P_BLK), F32), pltpu.VMEM((F_HALF, P_BLK), F32)],
        compiler_params=_params(("arbitrary",)),
        name="dft_matrices",
    )()


_SMEM = pl.BlockSpec(memory_space=pltpu.SMEM)
_COL_HALVES = (slice(0, HYENA_WIDTH // 2), slice(HYENA_WIDTH // 2, HYENA_WIDTH))


def _stacked_blocks(cr_ref, sr_ref, g, block, s1n):
    ar = block(0)
    bi = None
    for s in range(1, s1n):
        b = block(s)
        ar = ar + cr_ref[g, s] * b
        bi = sr_ref[g, s] * b if bi is None else bi + sr_ref[g, s] * b
    return jnp.concatenate([ar, bi], axis=0).astype(BF16)


def _spectrum_kernel(cr_ref, sr_ref, t_ref, a_ref, d_ref, g_ref, *, seq):
    g = pl.program_id(0)
    s1n = seq // P_BLK
    time_block = lambda ref: (lambda s: ref[s * P_BLK:(s + 1) * P_BLK, :].astype(F32))
    t_re = jnp.concatenate([t_ref[0, 0:F_HALF], t_ref[0, P_BLK:P_BLK + F_HALF]], axis=0)
    t_im = jnp.concatenate([t_ref[0, F_HALF:P_BLK], t_ref[0, P_BLK + F_HALF:2 * P_BLK]], axis=0)
    ka = jnp.dot(t_re, _stacked_blocks(cr_ref, sr_ref, g, time_block(a_ref), s1n),
                 preferred_element_type=F32)
    kd = jnp.dot(t_im, _stacked_blocks(cr_ref, sr_ref, g, time_block(d_ref), s1n),
                 preferred_element_type=F32)
    f2 = lax.broadcasted_iota(jnp.int32, (F_HALF, LANES), 0)
    reps = HYENA_WIDTH // LANES
    for blk, f1 in ((0, g), (1, 2 * s1n - 1 - g)):
        lo = blk * P_BLK
        phi = (2 * (f1 + 2 * s1n * f2) + 1).astype(F32) * (math.pi / (4 * seq))
        c = pltpu.repeat(jnp.cos(phi), reps, axis=1) * (1.0 / seq)
        s = pltpu.repeat(jnp.sin(phi), reps, axis=1) * (1.0 / seq)
        kr = ka[blk * F_HALF:(blk + 1) * F_HALF]
        ki = kd[blk * F_HALF:(blk + 1) * F_HALF]
        g_ref[0, lo:lo + F_HALF] = c * kr - s * ki
        g_ref[0, lo + F_HALF:lo + P_BLK] = c * ki + s * kr


def _filter_spectrum(cr, sr, t, a, d):
    seq = a.shape[0]
    s1n = seq // P_BLK
    return pl.pallas_call(
        functools.partial(_spectrum_kernel, seq=seq),
        out_shape=jax.ShapeDtypeStruct((s1n, 2 * P_BLK, HYENA_WIDTH), F32),
        grid=(s1n,),
        in_specs=[_SMEM, _SMEM, pl.BlockSpec((1, 2 * P_BLK, 2 * P_BLK), lambda g: (g, 0, 0)),
                  _resident(a.shape), _resident(d.shape)],
        out_specs=pl.BlockSpec((1, 2 * P_BLK, HYENA_WIDTH), lambda g: (g, 0, 0)),
        compiler_params=_params(("parallel",)),
        name="hyena_spectrum",
    )(cr, sr, t, a, d)


def _hy_fwd_kernel(cr_ref, sr_ref, t_ref, g_ref, z_ref, y_ref, *, s1n, nb):
    g = pl.program_id(0)
    for bi in range(nb):
        for cols in _COL_HALVES:
            x = _stacked_blocks(cr_ref, sr_ref, g,
                                lambda s: z_ref[bi, s * P_BLK:(s + 1) * P_BLK, cols].astype(F32), s1n)
            res = jnp.dot(t_ref[0], x, preferred_element_type=F32)
            for blk in range(2):
                lo = blk * P_BLK
                zr = res[lo:lo + F_HALF]
                wi = res[lo + F_HALF:lo + P_BLK]
                gr = g_ref[0, lo:lo + F_HALF, cols]
                gi = g_ref[0, lo + F_HALF:lo + P_BLK, cols]
                y_ref[bi, 0, lo:lo + F_HALF, cols] = (gr * zr + gi * wi).astype(BF16)
                y_ref[bi, 0, lo + F_HALF:lo + P_BLK, cols] = (gr * wi - gi * zr).astype(BF16)


def _hy_inv_kernel(cr_ref, sr_ref, tt_ref, y_ref, z_ref, g0_ref, fb_ref, gn_ref, o_ref,
                   acc_ref, zs_ref, g0s_ref, *, s1n, nb):
    g = pl.program_id(1)

    @pl.when(g == 0)
    def _():
        acc_ref[...] = jnp.zeros_like(acc_ref)

    tblk = pl.ds(pl.multiple_of(g * P_BLK, P_BLK), P_BLK)
    zs_ref[:, tblk, :] = z_ref[...]
    g0s_ref[:, tblk, :] = g0_ref[...]
    for bi in range(nb):
        for cols in _COL_HALVES:
            uv = jnp.dot(tt_ref[g], y_ref[bi, 0, :, cols], preferred_element_type=F32)
            u = uv[:P_BLK]
            vn = uv[P_BLK:]
            acc_ref[bi, 0:P_BLK, cols] += u
            for t1 in range(1, s1n):
                rows = slice(t1 * P_BLK, (t1 + 1) * P_BLK)
                acc_ref[bi, rows, cols] += cr_ref[g, t1] * u + sr_ref[g, t1] * vn

    @pl.when(g == s1n - 1)
    def _():
        for bi in range(nb):
            for t1 in range(s1n):
                rows = slice(t1 * P_BLK, (t1 + 1) * P_BLK)
                y = ((acc_ref[bi, rows, :] + zs_ref[bi, rows, :].astype(F32) * fb_ref[...])
                     * g0s_ref[bi, rows, :].astype(F32))
                o_ref[bi, rows, :] = _rms_norm(y, gn_ref[...]).astype(BF16)


_INV_STATE_BUDGET = 28 * 1024 * 1024


def _hyena_conv(zg, g0, cr, sr, t, tt, gspec, fbias, g_hyena):
    bsz, seq, w = zg.shape
    s1n = seq // P_BLK
    nb = 2 if bsz % 2 == 0 else 1
    mat = pl.BlockSpec((1, 2 * P_BLK, 2 * P_BLK), lambda g, b: (g, 0, 0))
    spec = pl.BlockSpec((1, 2 * P_BLK, w), lambda g, b: (g, 0, 0))
    y = pl.pallas_call(
        functools.partial(_hy_fwd_kernel, s1n=s1n, nb=nb),
        out_shape=jax.ShapeDtypeStruct((bsz, s1n, 2 * P_BLK, w), BF16),
        grid=(s1n, bsz // nb),
        in_specs=[_SMEM, _SMEM, mat, spec, pl.BlockSpec((nb, seq, w), lambda g, b: (b, 0, 0))],
        out_specs=pl.BlockSpec((nb, 1, 2 * P_BLK, w), lambda g, b: (b, g, 0, 0)),
        compiler_params=_params(("parallel", "parallel")),
        name="hyena_fwd",
    )(cr, sr, t, gspec, zg)
    state_bytes = seq * w * (4 + 2 + 2 + 2 * 2)
    nb = 2 if bsz % 2 == 0 and 2 * state_bytes <= _INV_STATE_BUDGET else 1
    blk = pl.BlockSpec((nb, P_BLK, w), lambda b, g: (b, g, 0))
    return pl.pallas_call(
        functools.partial(_hy_inv_kernel, s1n=s1n, nb=nb),
        out_shape=jax.ShapeDtypeStruct((bsz, seq, w), BF16),
        grid=(bsz // nb, s1n),
        in_specs=[_SMEM, _SMEM, _resident(tt.shape),
                  pl.BlockSpec((nb, 1, 2 * P_BLK, w), lambda b, g: (b, g, 0, 0)),
                  blk, blk, _resident(fbias.shape), _resident(g_hyena.shape)],
        out_specs=pl.BlockSpec((nb, seq, w), lambda b, g: (b, 0, 0)),
        scratch_shapes=[pltpu.VMEM((nb, seq, w), F32), pltpu.VMEM((nb, seq, w), BF16),
                        pltpu.VMEM((nb, seq, w), BF16)],
        compiler_params=_params(("parallel", "arbitrary")),
        name="hyena_inv",
    )(cr, sr, tt, y, zg, g0, fbias, g_hyena)


def _merge_kernel(ma_ref, mh_ref, x_ref, wa_ref, wh_ref, g_ref, b_ref, o_ref):
    y = (jnp.dot(ma_ref[0], wa_ref[...], preferred_element_type=F32)
         + jnp.dot(mh_ref[0], wh_ref[...], preferred_element_type=F32))
    o_ref[0] = _layer_norm(ALPHA * x_ref[0] + y, g_ref[...], b_ref[...])


def _merge(ma, mh, x, w_out, g, b, tl):
    bsz, seq, d = x.shape
    wa, wh = w_out[:ATTN_WIDTH], w_out[ATTN_WIDTH:]
    half = pl.BlockSpec((1, tl, ATTN_WIDTH), lambda bb, i: (bb, i, 0))
    full = pl.BlockSpec((1, tl, d), lambda bb, i: (bb, i, 0))
    return pl.pallas_call(
        _merge_kernel,
        out_shape=jax.ShapeDtypeStruct(x.shape, F32),
        grid=(bsz, seq // tl),
        in_specs=[half, half, full, _resident(wa.shape), _resident(wh.shape),
                  _resident(g.shape), _resident(b.shape)],
        out_specs=full,
        compiler_params=_params(("parallel", "parallel")),
        name="merge_ln1",
    )(ma, mh, x, wa, wh, g, b)


FF_CHUNK = 256
N_FF_CHUNKS = D_FF // FF_CHUNK


def _ffn_kernel(xp_ref, x_ref, xn_ref, wi_ref, cw_ref, cb_ref, wo_ref, g_ref, b_ref,
                o_ref, u_ref, hid_ref, *, tl):
    xh = _halo_block(xp_ref, x_ref, xn_ref)
    for j in range(N_FF_CHUNKS):
        conv = []
        for part in range(2):
            n = part * N_FF_CHUNKS + j
            u_ref[...] = jnp.dot(xh, wi_ref[n], preferred_element_type=F32)
            conv.append(_dwconv3(u_ref, cw_ref.at[n], cb_ref.at[n], tl))
        gate = conv[1]
        gelu = 0.5 * gate * (1.0 + lax.erf(gate * (2.0 ** -0.5)))
        hid_ref[:, j * FF_CHUNK:(j + 1) * FF_CHUNK] = (conv[0] * gelu).astype(BF16)
    y = jnp.dot(hid_ref[...], wo_ref[...], preferred_element_type=F32)
    o_ref[0] = _layer_norm(ALPHA * x_ref[0] + y, g_ref[...], b_ref[...])


def _ffn(x1, w_in, conv_w, conv_b, w_out, g, b, tl):
    bsz, seq, d = x1.shape
    nch = 2 * N_FF_CHUNKS
    wi = w_in.reshape(d, nch, FF_CHUNK).transpose(1, 0, 2)
    cw = conv_w.reshape(3, nch, FF_CHUNK).transpose(1, 0, 2)
    cb = conv_b.reshape(nch, 1, FF_CHUNK)
    return pl.pallas_call(
        functools.partial(_ffn_kernel, tl=tl),
        out_shape=jax.ShapeDtypeStruct(x1.shape, F32),
        grid=(bsz, seq // tl),
        in_specs=_halo_specs(tl, seq, d) + [_resident(wi.shape), _resident(cw.shape), _resident(cb.shape),
                                       _resident(w_out.shape), _resident(g.shape), _resident(b.shape)],
        out_specs=pl.BlockSpec((1, tl, d), lambda bb, i: (bb, i, 0)),
        scratch_shapes=[pltpu.VMEM((tl + 2 * HALO, FF_CHUNK), F32), pltpu.VMEM((tl, D_FF), BF16)],
        compiler_params=_params(("parallel", "parallel")),
        name="conv_ffn",
    )(x1, x1, x1, wi, cw, cb, w_out, g, b)


def _encoder_layer(x, p, tl=512, rb=8):
    seq = x.shape[1]
    q, k, v, zg, g0 = _in_proj(x, p["w_in"], p["short_w"], p["short_b"], tl)
    ma = _attention(q, k, v, p["bias"], p["g_attn"], rb)
    a, d = _filter_taps(seq, p["filt_w1"], p["filt_b1"], p["filt_freq"], p["filt_w_inner"],
                        p["filt_b_inner"], p["filt_w3"])
    cr, sr = _pair_coefficients(seq)
    t, tt = _dft_matrices(seq)
    gspec = _filter_spectrum(cr, sr, t, a, d)
    mh = _hyena_conv(zg, g0, cr, sr, t, tt, gspec, p["filt_bias"], p["g_hyena"])
    x1 = _merge(ma, mh, x, p["w_out"], p["ln1_g"], p["ln1_b"], tl)
    return _ffn(x1, p["ffn_w_in"], p["ffn_conv_w"], p["ffn_conv_b"], p["ffn_w_out"],
                p["ln2_g"], p["ln2_b"], tl)


def kernel(x_prompt, x_sample, w_in, short_w, short_b, rpb, filt_w1, filt_b1, filt_freq, filt_w_inner,
           filt_b_inner, filt_w3, filt_bias, g_attn, g_hyena, w_out, ln1_g, ln1_b, ffn_w_in, ffn_conv_w,
           ffn_conv_b, ffn_w_out, ln2_g, ln2_b):
    assert w_in.shape[0] == DEPTH == 1
    row = lambda a: a[0].reshape(1, -1)
    p = dict(
        w_in=w_in[0].astype(BF16), short_w=short_w[0], short_b=short_b[0],
        bias=_attention_bias(rpb[0]), g_attn=row(g_attn),
        filt_w1=filt_w1[0], filt_b1=filt_b1[0], filt_freq=filt_freq[0], filt_w_inner=filt_w_inner[0],
        filt_b_inner=filt_b_inner[0], filt_w3=filt_w3[0], filt_bias=filt_bias[0], g_hyena=row(g_hyena),
        w_out=w_out[0].astype(BF16), ln1_g=row(ln1_g), ln1_b=row(ln1_b),
        ffn_w_in=ffn_w_in[0].astype(BF16), ffn_conv_w=ffn_conv_w[0], ffn_conv_b=ffn_conv_b[0],
        ffn_w_out=ffn_w_out[0].astype(BF16), ln2_g=row(ln2_g), ln2_b=row(ln2_b),
    )
    return (_encoder_layer(x_prompt, p), _encoder_layer(x_sample, p))
```

```python
import functools
import math

import numpy as np
import jax
import jax.numpy as jnp
from jax import lax
from jax.experimental import pallas as pl
from jax.experimental.pallas import tpu as pltpu

F32 = jnp.float32
BF16 = jnp.bfloat16

D_MODEL = 1024
GRID_W = 64
ATTN_WIDTH = 512
HYENA_WIDTH = 512
HEAD_DIM = 64
N_HEADS = ATTN_WIDTH // HEAD_DIM
NA_ROWS = 8
NA_COLS = 16
FILTER_EMB = 33
FILTER_BANDS = (FILTER_EMB - 1) // 2
FILTER_HIDDEN = 64
FILTER_INNER = 2
MAX_DECAY = math.log(1e-2) / 0.3
MIN_DECAY = math.log(1e-2) / 1.5
D_FF = 2816
DEPTH = 1
ALPHA = (2 * DEPTH) ** 0.25
LN_EPS = 1e-5
RMS_EPS = 1e-6
LOG2E = math.log2(math.e)

SUBLANES = 8
LANES = 128
HALO = SUBLANES
VMEM_LIMIT = 56 * 1024 * 1024

HIGHEST = lax.Precision.HIGHEST


def _params(sem):
    return pltpu.CompilerParams(dimension_semantics=sem, vmem_limit_bytes=VMEM_LIMIT)


def _resident(shape):
    nd = len(shape)
    return pl.BlockSpec(shape, lambda *_: (0,) * nd, pipeline_mode=pl.Buffered(1))


def _layer_norm(y, g, b):
    mu = jnp.mean(y, axis=-1, keepdims=True)
    yc = y - mu
    var = jnp.mean(yc * yc, axis=-1, keepdims=True)
    return yc * lax.rsqrt(var + LN_EPS) * g + b


def _rms_norm(y, g):
    ms = jnp.mean(y * y, axis=-1, keepdims=True)
    return y * lax.rsqrt(ms + RMS_EPS) * g


def _fill_halo_slabs(xs_ref, xp_ref, x_ref, xn_ref, tl):
    i = pl.program_id(1)
    last = pl.num_programs(1) - 1
    prev = jnp.where(i > 0, xp_ref[0], 0.0)
    nxt = jnp.where(i < last, xn_ref[0], 0.0)
    for k in range(xs_ref.shape[0]):
        sl = slice(k * LANES, (k + 1) * LANES)
        xs_ref[k, 0:HALO, :] = prev[:, sl]
        xs_ref[k, HALO:HALO + tl, :] = x_ref[0, :, sl]
        xs_ref[k, HALO + tl:HALO + tl + HALO, :] = nxt[:, sl]


def _load_interleaved(xs_ref):
    nslab, rows, _ = xs_ref.shape
    nv = rows // SUBLANES
    assert rows % SUBLANES == 0 and nv % SUBLANES != 0
    return jnp.concatenate(
        [jnp.concatenate([xs_ref[k, pl.ds(j, SUBLANES, stride=nv), :] for j in range(nv)], axis=0)
         for k in range(nslab)], axis=1)


def _store_natural(os_ref, first, val):
    nv = os_ref.shape[1] // SUBLANES
    for k in range(val.shape[1] // LANES):
        for j in range(nv):
            os_ref[first + k, pl.ds(j, SUBLANES, stride=nv), :] = val[j * SUBLANES:(j + 1) * SUBLANES,
                                                                      k * LANES:(k + 1) * LANES]


def _dwconv3_interleaved(u, w_ref, b_ref):
    head = pltpu.roll(u[-SUBLANES:], 1, axis=0)
    tail = pltpu.roll(u[:SUBLANES], SUBLANES - 1, axis=0)
    up = jnp.concatenate([head, u[:-SUBLANES]], axis=0)
    un = jnp.concatenate([u[SUBLANES:], tail], axis=0)
    return up * w_ref[0:1, :] + u * w_ref[1:2, :] + un * w_ref[2:3, :] + b_ref[...]


def _halo_specs(tl, seq, d):
    nb = tl // HALO
    last = seq // HALO - 1
    return [
        pl.BlockSpec((1, HALO, d), lambda b, i: (b, jnp.maximum(i * nb - 1, 0), 0)),
        pl.BlockSpec((1, tl, d), lambda b, i: (b, i, 0)),
        pl.BlockSpec((1, HALO, d), lambda b, i: (b, jnp.minimum((i + 1) * nb, last), 0)),
    ]


def _in_proj_kernel(xp_ref, x_ref, xn_ref, w_ref, sw_ref, sb_ref,
                    q_ref, k_ref, v_ref, zg_ref, g0_ref, xs_ref, os_ref, *, tl):
    xq = x_ref[0].astype(BF16)
    proj = lambda n: jnp.dot(xq, w_ref[:, n * ATTN_WIDTH:(n + 1) * ATTN_WIDTH], preferred_element_type=F32)
    q_ref[0] = (proj(0) * (HEAD_DIM ** -0.5 * LOG2E)).astype(BF16)
    k_ref[0] = proj(1).astype(BF16)
    v = proj(2).astype(BF16)
    ones = jnp.ones((tl, LANES), BF16)
    v_ref[0] = jnp.concatenate([piece for p in range(N_HEADS // 2)
                                for piece in (v[:, p * LANES:(p + 1) * LANES], ones)], axis=1)
    _fill_halo_slabs(xs_ref, xp_ref, x_ref, xn_ref, tl)
    xh = _load_interleaved(xs_ref).astype(BF16)
    conv = []
    for n in range(3):
        lo = 3 * ATTN_WIDTH + n * HYENA_WIDTH
        u = jnp.dot(xh, w_ref[:, lo:lo + HYENA_WIDTH], preferred_element_type=F32)
        conv.append(_dwconv3_interleaved(u, sw_ref.at[n], sb_ref.at[n]))
    nslab = HYENA_WIDTH // LANES
    for first, ref, val in ((0, g0_ref, conv[0]), (nslab, zg_ref, conv[2] * conv[1])):
        _store_natural(os_ref, first, val)
        for k in range(nslab):
            ref[0, :, k * LANES:(k + 1) * LANES] = os_ref[first + k, HALO:HALO + tl, :].astype(BF16)


def _in_proj(x, w_in, short_w, short_b, tl):
    bsz, seq, d = x.shape
    sw = short_w.reshape(3, 3, HYENA_WIDTH).transpose(1, 0, 2)
    sb = short_b.reshape(3, 1, HYENA_WIDTH)
    out = jax.ShapeDtypeStruct((bsz, seq, ATTN_WIDTH), BF16)
    ospec = pl.BlockSpec((1, tl, ATTN_WIDTH), lambda b, i: (b, i, 0))
    vout = jax.ShapeDtypeStruct((bsz, seq, 2 * ATTN_WIDTH), BF16)
    vspec = pl.BlockSpec((1, tl, 2 * ATTN_WIDTH), lambda b, i: (b, i, 0))
    return pl.pallas_call(
        functools.partial(_in_proj_kernel, tl=tl),
        out_shape=(out, out, vout, out, out),
        grid=(bsz, seq // tl),
        in_specs=_halo_specs(tl, seq, d) + [_resident(w_in.shape), _resident(sw.shape), _resident(sb.shape)],
        out_specs=(ospec, ospec, vspec, ospec, ospec),
        scratch_shapes=[pltpu.VMEM((d // LANES, tl + 2 * HALO, LANES), F32),
                        pltpu.VMEM((2 * HYENA_WIDTH // LANES, tl + 2 * HALO, LANES), F32)],
        compiler_params=_params(("parallel", "parallel")),
        name="in_proj",
    )(x, x, x, w_in, sw, sb)


N_PAIRS = N_HEADS // 2
KEY_WIN = NA_ROWS * GRID_W
N_DR = 2 * NA_ROWS - 1
N_DC = 2 * NA_COLS - 1


def _bias_kernel(rpb_ref, ea_ref, eb_ref, o_ref, t_ref):
    row = lax.broadcasted_iota(jnp.int32, (GRID_W, LANES), 0)
    lane = lax.broadcasted_iota(jnp.int32, (GRID_W, LANES), 1)
    first = lane < GRID_W
    qcol = lane & (GRID_W - 1)
    col_start = jnp.clip(qcol - NA_COLS // 2, 0, GRID_W - NA_COLS)
    valid = (row >= col_start) & (row < col_start + NA_COLS)
    wa = jnp.dot(rpb_ref[0, 0], ea_ref[...], precision=HIGHEST, preferred_element_type=F32)
    wb = jnp.dot(rpb_ref[0, 1], eb_ref[...], precision=HIGHEST, preferred_element_type=F32)

    def rotate_rows(x):
        for t in range(GRID_W.bit_length() - 1):
            x = jnp.where(((row >> t) & 1) == 1, pltpu.roll(x, 1 << t, axis=1), x)
        return x

    for dr in range(N_DR):
        ta = rotate_rows(jnp.broadcast_to(wa[dr:dr + 1], (GRID_W, LANES)))
        tb = rotate_rows(jnp.broadcast_to(wb[dr:dr + 1], (GRID_W, LANES)))
        t_ref[dr] = jnp.where(valid, jnp.where(first, ta, tb) * LOG2E, -jnp.inf)
    for cls in range(NA_ROWS):
        for i in range(NA_ROWS):
            o_ref[0, cls, i * GRID_W:(i + 1) * GRID_W, :] = t_ref[i - cls + NA_ROWS - 1]


def _toeplitz_selectors():
    m = np.arange(LANES)
    dc = np.where(m < GRID_W, (NA_COLS - 1) - np.minimum(m, NA_COLS - 1),
                  np.minimum(LANES - m, NA_COLS - 1) + (NA_COLS - 1))
    ea = np.zeros((LANES, LANES), np.float32)
    ea[dc, m] = 1.0
    ea[:, GRID_W] = 0.0
    eb = np.roll(ea, GRID_W, axis=1)
    return ea, eb


def _attention_bias(rpb):
    ea, eb = _toeplitz_selectors()
    rp = jnp.pad(rpb.astype(F32), ((0, 0), (0, 16 - N_DR), (0, LANES - N_DC))).reshape(N_PAIRS, 2, 16, LANES)
    return pl.pallas_call(
        _bias_kernel,
        out_shape=jax.ShapeDtypeStruct((N_PAIRS, NA_ROWS, KEY_WIN, LANES), F32),
        grid=(N_PAIRS,),
        in_specs=[pl.BlockSpec((1, 2, 16, LANES), lambda p: (p, 0, 0, 0)),
                  _resident(ea.shape), _resident(eb.shape)],
        out_specs=pl.BlockSpec((1, NA_ROWS, KEY_WIN, LANES), lambda p: (p, 0, 0, 0)),
        scratch_shapes=[pltpu.VMEM((N_DR, GRID_W, LANES), F32)],
        compiler_params=_params(("parallel",)),
        name="attn_bias",
    )(rp, jnp.asarray(ea), jnp.asarray(eb))


def _attn_kernel(q_ref, k_ref, vx_ref, bias_ref, g_ref, o_ref, *, rows, rb, unroll):
    jb = pl.program_id(1)
    lane = lax.broadcasted_iota(jnp.int32, (GRID_W, LANES), 1)
    first = lane < HEAD_DIM
    keep = (first.astype(F32).astype(BF16), (~first).astype(F32).astype(BF16))
    nt = (((1,), (1,)), ((), ()))
    tn = (((0,), (0,)), ((), ()))

    def one_row(j):
        r = jb * rb + j
        start = jnp.clip(r - NA_ROWS // 2, 0, rows - NA_ROWS)
        cls = r - start
        qoff = pl.multiple_of(j * GRID_W, GRID_W)
        koff = pl.multiple_of(start * GRID_W, GRID_W)
        scores = []
        for p in range(N_PAIRS):
            sl = slice(p * LANES, (p + 1) * LANES)
            qp = q_ref[0, pl.ds(qoff, GRID_W), sl]
            qblk = jnp.concatenate([qp * keep[0], qp * keep[1]], axis=0)
            kp = k_ref[0, pl.ds(koff, KEY_WIN), sl]
            scores.append(lax.dot_general(kp, qblk, nt, preferred_element_type=F32))
        probs = []
        for p in range(N_PAIRS):
            s = scores[p] + bias_ref[p, cls]
            m = jnp.max(s, axis=0, keepdims=True)
            probs.append(jnp.exp2(s - m).astype(BF16))
        outs = []
        for p in range(N_PAIRS):
            vx = vx_ref[0, pl.ds(koff, KEY_WIN), 2 * p * LANES:2 * (p + 1) * LANES]
            o = lax.dot_general(probs[p], vx, tn, preferred_element_type=F32)
            oa = o[:GRID_W, :LANES] / o[:GRID_W, LANES:]
            ob = o[GRID_W:, :LANES] / o[GRID_W:, LANES:]
            outs.append(jnp.where(first, oa, ob))
        o_ref[0, pl.ds(qoff, GRID_W), :] = _rms_norm(jnp.concatenate(outs, axis=1), g_ref[...]).astype(BF16)

    def body(jj, carry):
        for u in range(unroll):
            one_row(jj * unroll + u)
        return carry

    lax.fori_loop(0, rb // unroll, body, 0)


def _attention(q, k, vx, bias, g_attn, rb, unroll=8):
    bsz, seq, w = q.shape
    rows = seq // GRID_W
    tq = rb * GRID_W
    return pl.pallas_call(
        functools.partial(_attn_kernel, rows=rows, rb=rb, unroll=unroll),
        out_shape=jax.ShapeDtypeStruct((bsz, seq, w), BF16),
        grid=(bsz, rows // rb),
        in_specs=[
            pl.BlockSpec((1, tq, w), lambda b, i: (b, i, 0)),
            pl.BlockSpec((1, seq, w), lambda b, i: (b, 0, 0)),
            pl.BlockSpec((1, seq, 2 * w), lambda b, i: (b, 0, 0)),
            _resident(bias.shape),
            _resident(g_attn.shape),
        ],
        out_specs=pl.BlockSpec((1, tq, w), lambda b, i: (b, i, 0)),
        compiler_params=_params(("parallel", "arbitrary")),
        name="attention",
    )(q, k, vx, bias, g_attn)


def _filter_kernel(w1t_ref, w1c_ref, w1s_ref, b1_ref, fq_ref, wi_ref, bi_ref, w3_ref,
                   a_ref, d_ref, cb_ref, sb_ref, h_ref, *, seq, tl):
    i = pl.program_id(0)
    rows = tl + SUBLANES
    band = lax.broadcasted_iota(jnp.int32, (rows, LANES), 1)
    freqs = jnp.where(band < FILTER_BANDS,
                      1e-4 + band.astype(F32) * ((FILTER_BANDS - 1 - 1e-4) / (FILTER_BANDS - 1)), 0.0)
    rad = freqs * (2.0 * math.pi / seq)
    local = lax.broadcasted_iota(jnp.int32, (rows, LANES), 0)

    @pl.when(i == 0)
    def _():
        cb_ref[...] = jnp.cos(local.astype(F32) * rad)
        sb_ref[...] = jnp.sin(local.astype(F32) * rad)

    base = (i * tl).astype(F32) * rad[0:SUBLANES]
    ca = jnp.cos(base)[0:1]
    sa = jnp.sin(base)[0:1]
    cos_ang = ca * cb_ref[...] - sa * sb_ref[...]
    sin_ang = sa * cb_ref[...] + ca * sb_ref[...]
    pos = local[:, 0:1] + i * tl
    t = pos.astype(F32) * (1.0 / (seq - 1))
    fq = fq_ref[...]
    pre = (t * w1t_ref[...]
           + jnp.dot(cos_ang, w1c_ref[...], precision=HIGHEST, preferred_element_type=F32)
           - jnp.dot(sin_ang, w1s_ref[...], precision=HIGHEST, preferred_element_type=F32)
           + b1_ref[...])
    h = jnp.sin(fq * pre)
    for n in range(FILTER_INNER):
        h = jnp.sin(fq * (jnp.dot(h, wi_ref[n], precision=HIGHEST, preferred_element_type=F32) + bi_ref[n]))
    chan = lax.broadcasted_iota(jnp.int32, (rows, HYENA_WIDTH), 1).astype(F32)
    deltas = jnp.abs(MIN_DECAY + chan * ((MAX_DECAY - MIN_DECAY) / (HYENA_WIDTH - 1)))
    decay = jnp.exp(-t * deltas)
    for n in range(2):
        cols = slice(n * HYENA_WIDTH, (n + 1) * HYENA_WIDTH)
        h_ref[:, cols] = jnp.dot(h, w3_ref[:, cols], precision=HIGHEST, preferred_element_type=F32) * decay
    h_fwd = h_ref[0:tl, 0:HYENA_WIDTH]
    h_bwd = jnp.where(pos[0:tl] + 1 < seq, h_ref[pl.ds(1, tl), HYENA_WIDTH:2 * HYENA_WIDTH], 0.0)
    a_ref[...] = (h_fwd + h_bwd).astype(BF16)
    d_ref[...] = (h_bwd - h_fwd).astype(BF16)


def _filter_taps(seq, w1, b1, freq, w_inner, b_inner, w3, tl=512):
    pad = LANES - FILTER_BANDS
    w1t = w1[0:1]
    w1c = jnp.pad(w1[1:1 + FILTER_BANDS], ((0, pad), (0, 0)))
    w1s = jnp.pad(w1[1 + FILTER_BANDS:], ((0, pad), (0, 0)))
    args = (w1t, w1c, w1s, b1.reshape(1, -1), freq.reshape(1, -1), w_inner,
            b_inner.reshape(FILTER_INNER, 1, -1), w3)
    out = jax.ShapeDtypeStruct((seq, HYENA_WIDTH), BF16)
    ospec = pl.BlockSpec((tl, HYENA_WIDTH), lambda i: (i, 0))
    return pl.pallas_call(
        functools.partial(_filter_kernel, seq=seq, tl=tl),
        out_shape=(out, out),
        grid=(seq // tl,),
        in_specs=[_resident(a.shape) for a in args],
        out_specs=(ospec, ospec),
        scratch_shapes=[pltpu.VMEM((tl + SUBLANES, LANES), F32), pltpu.VMEM((tl + SUBLANES, LANES), F32),
                        pltpu.VMEM((tl + SUBLANES, 2 * HYENA_WIDTH), F32)],
        compiler_params=_params(("arbitrary",)),
        name="hyena_filter",
    )(*args)


P_BLK = 512
F_HALF = P_BLK // 2


def _pair_coefficients(seq):
    s1n = seq // P_BLK
    g = np.arange(s1n)[:, None]
    s = np.arange(s1n)[None, :]
    ang = np.pi * (2 * g + 1) * s / (2 * s1n)
    return jnp.asarray(np.cos(ang), F32), jnp.asarray(np.sin(ang), F32)


def _dft_kernel(t_ref, tt_ref, cb_ref, sb_ref, *, seq):
    g = pl.program_id(0)
    s1n = seq // P_BLK

    @pl.when(g == 0)
    def _():
        f2 = lax.broadcasted_iota(jnp.int32, (F_HALF, P_BLK), 0)
        s_odd = 2 * lax.broadcasted_iota(jnp.int32, (F_HALF, P_BLK), 1) + 1
        beta = ((f2 * s_odd) & (2 * P_BLK - 1)).astype(F32) * (math.pi / P_BLK)
        cb_ref[...] = jnp.cos(beta)
        sb_ref[...] = jnp.sin(beta)

    cb = cb_ref[...]
    sb = sb_ref[...]
    s_odd = 2 * lax.broadcasted_iota(jnp.int32, (SUBLANES, P_BLK), 1) + 1

    def block(f1):
        alpha = ((s_odd * (2 * f1 + 1)) & (8 * seq - 1)).astype(F32) * (math.pi / (4 * seq))
        ca = jnp.cos(alpha)[0:1]
        sa = jnp.sin(alpha)[0:1]
        return ca * cb - sa * sb, sa * cb + ca * sb

    mra, mia = block(g)
    mrb, mib = block(2 * s1n - 1 - g)
    for r, (left, right) in enumerate(((mra, -mia), (mia, mra), (mrb, mib), (mib, -mrb))):
        rows = slice(r * F_HALF, (r + 1) * F_HALF)
        t_ref[0, rows, :P_BLK] = left.astype(BF16)
        t_ref[0, rows, P_BLK:] = right.astype(BF16)
        tt_ref[0, :P_BLK, rows] = left.T.astype(BF16)
        tt_ref[0, P_BLK:, rows] = right.T.astype(BF16)


def _dft_matrices(seq):
    s1n = seq // P_BLK
    mat = jax.ShapeDtypeStruct((s1n, 2 * P_BLK, 2 * P_BLK), BF16)
    spec = pl.BlockSpec((1, 2 * P_BLK, 2 * P_BLK), lambda g: (g, 0, 0))
    return pl.pallas_call(
        functools.partial(_dft_kernel, seq=seq),
        out_shape=(mat, mat),
        grid=(s1n,),
        out_specs=(spec, spec),
        scratch_shapes=[pltpu.VMEM((F_HALF, P_BLK), F32), pltpu.VMEM((F_HALF, P_BLK), F32)],
        compiler_params=_params(("arbitrary",)),
        name="dft_matrices",
    )()


_SMEM = pl.BlockSpec(memory_space=pltpu.SMEM)
_COL_HALVES = (slice(0, HYENA_WIDTH // 2), slice(HYENA_WIDTH // 2, HYENA_WIDTH))


def _stacked_blocks(cr_ref, sr_ref, g, block, s1n):
    ar = block(0)
    bi = None
    for s in range(1, s1n):
        b = block(s)
        ar = ar + cr_ref[g, s] * b
        bi = sr_ref[g, s] * b if bi is None else bi + sr_ref[g, s] * b
    return jnp.concatenate([ar, bi], axis=0).astype(BF16)


def _spectrum_kernel(cr_ref, sr_ref, t_ref, a_ref, d_ref, g_ref, *, seq):
    g = pl.program_id(0)
    s1n = seq // P_BLK
    time_block = lambda ref: (lambda s: ref[s * P_BLK:(s + 1) * P_BLK, :].astype(F32))
    t_re = jnp.concatenate([t_ref[0, 0:F_HALF], t_ref[0, P_BLK:P_BLK + F_HALF]], axis=0)
    t_im = jnp.concatenate([t_ref[0, F_HALF:P_BLK], t_ref[0, P_BLK + F_HALF:2 * P_BLK]], axis=0)
    ka = jnp.dot(t_re, _stacked_blocks(cr_ref, sr_ref, g, time_block(a_ref), s1n),
                 preferred_element_type=F32)
    kd = jnp.dot(t_im, _stacked_blocks(cr_ref, sr_ref, g, time_block(d_ref), s1n),
                 preferred_element_type=F32)
    f2 = lax.broadcasted_iota(jnp.int32, (F_HALF, LANES), 0)
    reps = HYENA_WIDTH // LANES
    for blk, f1 in ((0, g), (1, 2 * s1n - 1 - g)):
        lo = blk * P_BLK
        phi = (2 * (f1 + 2 * s1n * f2) + 1).astype(F32) * (math.pi / (4 * seq))
        c = pltpu.repeat(jnp.cos(phi), reps, axis=1) * (1.0 / seq)
        s = pltpu.repeat(jnp.sin(phi), reps, axis=1) * (1.0 / seq)
        kr = ka[blk * F_HALF:(blk + 1) * F_HALF]
        ki = kd[blk * F_HALF:(blk + 1) * F_HALF]
        g_ref[0, lo:lo + F_HALF] = c * kr - s * ki
        g_ref[0, lo + F_HALF:lo + P_BLK] = c * ki + s * kr


def _filter_spectrum(cr, sr, t, a, d):
    seq = a.shape[0]
    s1n = seq // P_BLK
    return pl.pallas_call(
        functools.partial(_spectrum_kernel, seq=seq),
        out_shape=jax.ShapeDtypeStruct((s1n, 2 * P_BLK, HYENA_WIDTH), F32),
        grid=(s1n,),
        in_specs=[_SMEM, _SMEM, pl.BlockSpec((1, 2 * P_BLK, 2 * P_BLK), lambda g: (g, 0, 0)),
                  _resident(a.shape), _resident(d.shape)],
        out_specs=pl.BlockSpec((1, 2 * P_BLK, HYENA_WIDTH), lambda g: (g, 0, 0)),
        compiler_params=_params(("parallel",)),
        name="hyena_spectrum",
    )(cr, sr, t, a, d)


def _hy_fwd_kernel(cr_ref, sr_ref, t_ref, g_ref, z_ref, y_ref, *, s1n, nb):
    g = pl.program_id(0)
    for bi in range(nb):
        for cols in _COL_HALVES:
            x = _stacked_blocks(cr_ref, sr_ref, g,
                                lambda s: z_ref[bi, s * P_BLK:(s + 1) * P_BLK, cols].astype(F32), s1n)
            res = jnp.dot(t_ref[0], x, preferred_element_type=F32)
            for blk in range(2):
                lo = blk * P_BLK
                zr = res[lo:lo + F_HALF]
                wi = res[lo + F_HALF:lo + P_BLK]
                gr = g_ref[0, lo:lo + F_HALF, cols]
                gi = g_ref[0, lo + F_HALF:lo + P_BLK, cols]
                y_ref[bi, 0, lo:lo + F_HALF, cols] = (gr * zr + gi * wi).astype(BF16)
                y_ref[bi, 0, lo + F_HALF:lo + P_BLK, cols] = (gr * wi - gi * zr).astype(BF16)


def _hy_inv_kernel(cr_ref, sr_ref, tt_ref, y_ref, z_ref, g0_ref, fb_ref, gn_ref, o_ref,
                   acc_ref, zs_ref, g0s_ref, *, s1n, nb):
    g = pl.program_id(1)

    @pl.when(g == 0)
    def _():
        acc_ref[...] = jnp.zeros_like(acc_ref)

    tblk = pl.ds(pl.multiple_of(g * P_BLK, P_BLK), P_BLK)
    zs_ref[:, tblk, :] = z_ref[...]
    g0s_ref[:, tblk, :] = g0_ref[...]
    for bi in range(nb):
        for cols in _COL_HALVES:
            uv = jnp.dot(tt_ref[g], y_ref[bi, 0, :, cols], preferred_element_type=F32)
            u = uv[:P_BLK]
            vn = uv[P_BLK:]
            acc_ref[bi, 0:P_BLK, cols] += u
            for t1 in range(1, s1n):
                rows = slice(t1 * P_BLK, (t1 + 1) * P_BLK)
                acc_ref[bi, rows, cols] += cr_ref[g, t1] * u + sr_ref[g, t1] * vn

    @pl.when(g == s1n - 1)
    def _():
        for bi in range(nb):
            for t1 in range(s1n):
                rows = slice(t1 * P_BLK, (t1 + 1) * P_BLK)
                y = ((acc_ref[bi, rows, :] + zs_ref[bi, rows, :].astype(F32) * fb_ref[...])
                     * g0s_ref[bi, rows, :].astype(F32))
                o_ref[bi, rows, :] = _rms_norm(y, gn_ref[...]).astype(BF16)


_INV_STATE_BUDGET = 28 * 1024 * 1024


def _hyena_conv(zg, g0, cr, sr, t, tt, gspec, fbias, g_hyena):
    bsz, seq, w = zg.shape
    s1n = seq // P_BLK
    nb = 2 if bsz % 2 == 0 else 1
    mat = pl.BlockSpec((1, 2 * P_BLK, 2 * P_BLK), lambda g, b: (g, 0, 0))
    spec = pl.BlockSpec((1, 2 * P_BLK, w), lambda g, b: (g, 0, 0))
    y = pl.pallas_call(
        functools.partial(_hy_fwd_kernel, s1n=s1n, nb=nb),
        out_shape=jax.ShapeDtypeStruct((bsz, s1n, 2 * P_BLK, w), BF16),
        grid=(s1n, bsz // nb),
        in_specs=[_SMEM, _SMEM, mat, spec, pl.BlockSpec((nb, seq, w), lambda g, b: (b, 0, 0))],
        out_specs=pl.BlockSpec((nb, 1, 2 * P_BLK, w), lambda g, b: (b, g, 0, 0)),
        compiler_params=_params(("parallel", "parallel")),
        name="hyena_fwd",
    )(cr, sr, t, gspec, zg)
    state_bytes = seq * w * (4 + 2 + 2 + 2 * 2)
    nb = 2 if bsz % 2 == 0 and 2 * state_bytes <= _INV_STATE_BUDGET else 1
    blk = pl.BlockSpec((nb, P_BLK, w), lambda b, g: (b, g, 0))
    return pl.pallas_call(
        functools.partial(_hy_inv_kernel, s1n=s1n, nb=nb),
        out_shape=jax.ShapeDtypeStruct((bsz, seq, w), BF16),
        grid=(bsz // nb, s1n),
        in_specs=[_SMEM, _SMEM, _resident(tt.shape),
                  pl.BlockSpec((nb, 1, 2 * P_BLK, w), lambda b, g: (b, g, 0, 0)),
                  blk, blk, _resident(fbias.shape), _resident(g_hyena.shape)],
        out_specs=pl.BlockSpec((nb, seq, w), lambda b, g: (b, 0, 0)),
        scratch_shapes=[pltpu.VMEM((nb, seq, w), F32), pltpu.VMEM((nb, seq, w), BF16),
                        pltpu.VMEM((nb, seq, w), BF16)],
        compiler_params=_params(("parallel", "arbitrary")),
        name="hyena_inv",
    )(cr, sr, tt, y, zg, g0, fbias, g_hyena)


def _merge_kernel(ma_ref, mh_ref, x_ref, wa_ref, wh_ref, g_ref, b_ref, o_ref):
    y = (jnp.dot(ma_ref[0], wa_ref[...], preferred_element_type=F32)
         + jnp.dot(mh_ref[0], wh_ref[...], preferred_element_type=F32))
    o_ref[0] = _layer_norm(ALPHA * x_ref[0] + y, g_ref[...], b_ref[...])


def _merge(ma, mh, x, w_out, g, b, tl):
    bsz, seq, d = x.shape
    wa, wh = w_out[:ATTN_WIDTH], w_out[ATTN_WIDTH:]
    half = pl.BlockSpec((1, tl, ATTN_WIDTH), lambda bb, i: (bb, i, 0))
    full = pl.BlockSpec((1, tl, d), lambda bb, i: (bb, i, 0))
    return pl.pallas_call(
        _merge_kernel,
        out_shape=jax.ShapeDtypeStruct(x.shape, F32),
        grid=(bsz, seq // tl),
        in_specs=[half, half, full, _resident(wa.shape), _resident(wh.shape),
                  _resident(g.shape), _resident(b.shape)],
        out_specs=full,
        compiler_params=_params(("parallel", "parallel")),
        name="merge_ln1",
    )(ma, mh, x, wa, wh, g, b)


FF_CHUNK = 256
N_FF_CHUNKS = D_FF // FF_CHUNK


def _ffn_kernel(xp_ref, x_ref, xn_ref, wi_ref, cw_ref, cb_ref, wo_ref, g_ref, b_ref,
                o_ref, xs_ref, hid_ref, os_ref, *, tl):
    _fill_halo_slabs(xs_ref, xp_ref, x_ref, xn_ref, tl)
    xh = _load_interleaved(xs_ref).astype(BF16)
    for j in range(N_FF_CHUNKS):
        conv = []
        for part in range(2):
            n = part * N_FF_CHUNKS + j
            u = jnp.dot(xh, wi_ref[n], preferred_element_type=F32)
            conv.append(_dwconv3_interleaved(u, cw_ref.at[n], cb_ref.at[n]))
        gate = conv[1]
        gelu = 0.5 * gate * (1.0 + lax.erf(gate * (2.0 ** -0.5)))
        hid_ref[:, j * FF_CHUNK:(j + 1) * FF_CHUNK] = (conv[0] * gelu).astype(BF16)
    y = jnp.dot(hid_ref[...], wo_ref[...], preferred_element_type=F32)
    out = _layer_norm(ALPHA * _load_interleaved(xs_ref) + y, g_ref[...], b_ref[...])
    _store_natural(os_ref, 0, out)
    for k in range(os_ref.shape[0]):
        o_ref[0, :, k * LANES:(k + 1) * LANES] = os_ref[k, HALO:HALO + tl, :]


def _ffn(x1, w_in, conv_w, conv_b, w_out, g, b, tl):
    bsz, seq, d = x1.shape
    nch = 2 * N_FF_CHUNKS
    wi = w_in.reshape(d, nch, FF_CHUNK).transpose(1, 0, 2)
    cw = conv_w.reshape(3, nch, FF_CHUNK).transpose(1, 0, 2)
    cb = conv_b.reshape(nch, 1, FF_CHUNK)
    return pl.pallas_call(
        functools.partial(_ffn_kernel, tl=tl),
        out_shape=jax.ShapeDtypeStruct(x1.shape, F32),
        grid=(bsz, seq // tl),
        in_specs=_halo_specs(tl, seq, d) + [_resident(wi.shape), _resident(cw.shape), _resident(cb.shape),
                                       _resident(w_out.shape), _resident(g.shape), _resident(b.shape)],
        out_specs=pl.BlockSpec((1, tl, d), lambda bb, i: (bb, i, 0)),
        scratch_shapes=[pltpu.VMEM((d // LANES, tl + 2 * HALO, LANES), F32),
                        pltpu.VMEM((tl + 2 * HALO, D_FF), BF16),
                        pltpu.VMEM((d // LANES, tl + 2 * HALO, LANES), F32)],
        compiler_params=_params(("parallel", "parallel")),
        name="conv_ffn",
    )(x1, x1, x1, wi, cw, cb, w_out, g, b)


def _encoder_layer(x, p, tl=512, rb=8):
    seq = x.shape[1]
    q, k, v, zg, g0 = _in_proj(x, p["w_in"], p["short_w"], p["short_b"], tl)
    ma = _attention(q, k, v, p["bias"], p["g_attn"], rb)
    a, d = _filter_taps(seq, p["filt_w1"], p["filt_b1"], p["filt_freq"], p["filt_w_inner"],
                        p["filt_b_inner"], p["filt_w3"])
    cr, sr = _pair_coefficients(seq)
    t, tt = _dft_matrices(seq)
    gspec = _filter_spectrum(cr, sr, t, a, d)
    mh = _hyena_conv(zg, g0, cr, sr, t, tt, gspec, p["filt_bias"], p["g_hyena"])
    x1 = _merge(ma, mh, x, p["w_out"], p["ln1_g"], p["ln1_b"], tl)
    return _ffn(x1, p["ffn_w_in"], p["ffn_conv_w"], p["ffn_conv_b"], p["ffn_w_out"],
                p["ln2_g"], p["ln2_b"], tl)


def kernel(x_prompt, x_sample, w_in, short_w, short_b, rpb, filt_w1, filt_b1, filt_freq, filt_w_inner,
           filt_b_inner, filt_w3, filt_bias, g_attn, g_hyena, w_out, ln1_g, ln1_b, ffn_w_in, ffn_conv_w,
           ffn_conv_b, ffn_w_out, ln2_g, ln2_b):
    assert w_in.shape[0] == DEPTH == 1
    row = lambda a: a[0].reshape(1, -1)
    p = dict(
        w_in=w_in[0].astype(BF16), short_w=short_w[0], short_b=short_b[0],
        bias=_attention_bias(rpb[0]), g_attn=row(g_attn),
        filt_w1=filt_w1[0], filt_b1=filt_b1[0], filt_freq=filt_freq[0], filt_w_inner=filt_w_inner[0],
        filt_b_inner=filt_b_inner[0], filt_w3=filt_w3[0], filt_bias=filt_bias[0], g_hyena=row(g_hyena),
        w_out=w_out[0].astype(BF16), ln1_g=row(ln1_g), ln1_b=row(ln1_b),
        ffn_w_in=ffn_w_in[0].astype(BF16), ffn_conv_w=ffn_conv_w[0], ffn_conv_b=ffn_conv_b[0],
        ffn_w_out=ffn_w_out[0].astype(BF16), ln2_g=row(ln2_g), ln2_b=row(ln2_b),
    )
    return (_encoder_layer(x_prompt, p), _encoder_layer(x_sample, p))
```

```python
import functools
import math

import numpy as np
import jax
import jax.numpy as jnp
from jax import lax
from jax.experimental import pallas as pl
from jax.experimental.pallas import tpu as pltpu

F32 = jnp.float32
BF16 = jnp.bfloat16

D_MODEL = 1024
GRID_W = 64
ATTN_WIDTH = 512
HYENA_WIDTH = 512
HEAD_DIM = 64
N_HEADS = ATTN_WIDTH // HEAD_DIM
NA_ROWS = 8
NA_COLS = 16
FILTER_EMB = 33
FILTER_BANDS = (FILTER_EMB - 1) // 2
FILTER_HIDDEN = 64
FILTER_INNER = 2
MAX_DECAY = math.log(1e-2) / 0.3
MIN_DECAY = math.log(1e-2) / 1.5
D_FF = 2816
DEPTH = 1
ALPHA = (2 * DEPTH) ** 0.25
LN_EPS = 1e-5
RMS_EPS = 1e-6
LOG2E = math.log2(math.e)

SUBLANES = 8
LANES = 128
HALO = SUBLANES
VMEM_LIMIT = 56 * 1024 * 1024

HIGHEST = lax.Precision.HIGHEST


def _params(sem):
    return pltpu.CompilerParams(dimension_semantics=sem, vmem_limit_bytes=VMEM_LIMIT)


def _resident(shape):
    nd = len(shape)
    return pl.BlockSpec(shape, lambda *_: (0,) * nd, pipeline_mode=pl.Buffered(1))


def _layer_norm(y, g, b):
    mu = jnp.mean(y, axis=-1, keepdims=True)
    yc = y - mu
    var = jnp.mean(yc * yc, axis=-1, keepdims=True)
    return yc * lax.rsqrt(var + LN_EPS) * g + b


def _rms_norm(y, g):
    ms = jnp.mean(y * y, axis=-1, keepdims=True)
    return y * lax.rsqrt(ms + RMS_EPS) * g


def _fill_halo_slabs(xs_ref, xp_ref, x_ref, xn_ref, tl):
    i = pl.program_id(1)
    last = pl.num_programs(1) - 1
    prev = jnp.where(i > 0, xp_ref[0], 0.0)
    nxt = jnp.where(i < last, xn_ref[0], 0.0)
    for k in range(xs_ref.shape[0]):
        sl = slice(k * LANES, (k + 1) * LANES)
        xs_ref[k, 0:HALO, :] = prev[:, sl]
        xs_ref[k, HALO:HALO + tl, :] = x_ref[0, :, sl]
        xs_ref[k, HALO + tl:HALO + tl + HALO, :] = nxt[:, sl]


def _load_interleaved(xs_ref):
    nslab, rows, _ = xs_ref.shape
    nv = rows // SUBLANES
    assert rows % SUBLANES == 0 and nv % SUBLANES != 0
    return jnp.concatenate(
        [jnp.concatenate([xs_ref[k, pl.ds(j, SUBLANES, stride=nv), :] for j in range(nv)], axis=0)
         for k in range(nslab)], axis=1)


def _store_natural(os_ref, first, val):
    nv = os_ref.shape[1] // SUBLANES
    for k in range(val.shape[1] // LANES):
        for j in range(nv):
            os_ref[first + k, pl.ds(j, SUBLANES, stride=nv), :] = val[j * SUBLANES:(j + 1) * SUBLANES,
                                                                      k * LANES:(k + 1) * LANES]


def _dwconv3_interleaved(u, w, b):
    head = pltpu.roll(u[-SUBLANES:], 1, axis=0)
    tail = pltpu.roll(u[:SUBLANES], SUBLANES - 1, axis=0)
    up = jnp.concatenate([head, u[:-SUBLANES]], axis=0)
    un = jnp.concatenate([u[SUBLANES:], tail], axis=0)
    return up * w[0:1] + u * w[1:2] + un * w[2:3] + b


def _halo_specs(tl, seq, d):
    nb = tl // HALO
    last = seq // HALO - 1
    return [
        pl.BlockSpec((1, HALO, d), lambda b, i: (b, jnp.maximum(i * nb - 1, 0), 0)),
        pl.BlockSpec((1, tl, d), lambda b, i: (b, i, 0)),
        pl.BlockSpec((1, HALO, d), lambda b, i: (b, jnp.minimum((i + 1) * nb, last), 0)),
    ]


def _in_proj_kernel(xp_ref, x_ref, xn_ref, w_ref, sw_ref, sb_ref,
                    q_ref, k_ref, v_ref, zg_ref, g0_ref, xs_ref, os_ref, *, tl):
    xq = x_ref[0].astype(BF16)
    proj = lambda n: jnp.dot(xq, w_ref[:, n * ATTN_WIDTH:(n + 1) * ATTN_WIDTH], preferred_element_type=F32)
    q_ref[0] = (proj(0) * (HEAD_DIM ** -0.5 * LOG2E)).astype(BF16)
    k_ref[0] = proj(1).astype(BF16)
    v = proj(2).astype(BF16)
    ones = jnp.ones((tl, LANES), BF16)
    v_ref[0] = jnp.concatenate([piece for p in range(N_HEADS // 2)
                                for piece in (v[:, p * LANES:(p + 1) * LANES], ones)], axis=1)
    _fill_halo_slabs(xs_ref, xp_ref, x_ref, xn_ref, tl)
    xh = _load_interleaved(xs_ref).astype(BF16)
    conv = []
    for n in range(3):
        lo = 3 * ATTN_WIDTH + n * HYENA_WIDTH
        u = jnp.dot(xh, w_ref[:, lo:lo + HYENA_WIDTH], preferred_element_type=F32)
        conv.append(_dwconv3_interleaved(u, sw_ref[n], sb_ref[n]))
    nslab = HYENA_WIDTH // LANES
    for first, ref, val in ((0, g0_ref, conv[0]), (nslab, zg_ref, conv[2] * conv[1])):
        _store_natural(os_ref, first, val)
        for k in range(nslab):
            ref[0, :, k * LANES:(k + 1) * LANES] = os_ref[first + k, HALO:HALO + tl, :].astype(BF16)


def _in_proj(x, w_in, short_w, short_b, tl):
    bsz, seq, d = x.shape
    sw = short_w.reshape(3, 3, HYENA_WIDTH).transpose(1, 0, 2)
    sb = short_b.reshape(3, 1, HYENA_WIDTH)
    out = jax.ShapeDtypeStruct((bsz, seq, ATTN_WIDTH), BF16)
    ospec = pl.BlockSpec((1, tl, ATTN_WIDTH), lambda b, i: (b, i, 0))
    vout = jax.ShapeDtypeStruct((bsz, seq, 2 * ATTN_WIDTH), BF16)
    vspec = pl.BlockSpec((1, tl, 2 * ATTN_WIDTH), lambda b, i: (b, i, 0))
    return pl.pallas_call(
        functools.partial(_in_proj_kernel, tl=tl),
        out_shape=(out, out, vout, out, out),
        grid=(bsz, seq // tl),
        in_specs=_halo_specs(tl, seq, d) + [_resident(w_in.shape), _resident(sw.shape), _resident(sb.shape)],
        out_specs=(ospec, ospec, vspec, ospec, ospec),
        scratch_shapes=[pltpu.VMEM((d // LANES, tl + 2 * HALO, LANES), F32),
                        pltpu.VMEM((2 * HYENA_WIDTH // LANES, tl + 2 * HALO, LANES), F32)],
        compiler_params=_params(("parallel", "parallel")),
        name="in_proj",
    )(x, x, x, w_in, sw, sb)


N_PAIRS = N_HEADS // 2
KEY_WIN = NA_ROWS * GRID_W
N_DR = 2 * NA_ROWS - 1
N_DC = 2 * NA_COLS - 1


def _bias_kernel(rpb_ref, ea_ref, eb_ref, o_ref, t_ref):
    row = lax.broadcasted_iota(jnp.int32, (GRID_W, LANES), 0)
    lane = lax.broadcasted_iota(jnp.int32, (GRID_W, LANES), 1)
    first = lane < GRID_W
    qcol = lane & (GRID_W - 1)
    col_start = jnp.clip(qcol - NA_COLS // 2, 0, GRID_W - NA_COLS)
    valid = (row >= col_start) & (row < col_start + NA_COLS)
    wa = jnp.dot(rpb_ref[0, 0], ea_ref[...], precision=HIGHEST, preferred_element_type=F32)
    wb = jnp.dot(rpb_ref[0, 1], eb_ref[...], precision=HIGHEST, preferred_element_type=F32)

    def rotate_rows(x):
        for t in range(GRID_W.bit_length() - 1):
            x = jnp.where(((row >> t) & 1) == 1, pltpu.roll(x, 1 << t, axis=1), x)
        return x

    for dr in range(N_DR):
        ta = rotate_rows(jnp.broadcast_to(wa[dr:dr + 1], (GRID_W, LANES)))
        tb = rotate_rows(jnp.broadcast_to(wb[dr:dr + 1], (GRID_W, LANES)))
        t_ref[dr] = jnp.where(valid, jnp.where(first, ta, tb) * LOG2E, -jnp.inf)
    for cls in range(NA_ROWS):
        for i in range(NA_ROWS):
            o_ref[0, cls, i * GRID_W:(i + 1) * GRID_W, :] = t_ref[i - cls + NA_ROWS - 1]


def _toeplitz_selectors():
    m = np.arange(LANES)
    dc = np.where(m < GRID_W, (NA_COLS - 1) - np.minimum(m, NA_COLS - 1),
                  np.minimum(LANES - m, NA_COLS - 1) + (NA_COLS - 1))
    ea = np.zeros((LANES, LANES), np.float32)
    ea[dc, m] = 1.0
    ea[:, GRID_W] = 0.0
    eb = np.roll(ea, GRID_W, axis=1)
    return ea, eb


def _attention_bias(rpb):
    ea, eb = _toeplitz_selectors()
    rp = jnp.pad(rpb.astype(F32), ((0, 0), (0, 16 - N_DR), (0, LANES - N_DC))).reshape(N_PAIRS, 2, 16, LANES)
    return pl.pallas_call(
        _bias_kernel,
        out_shape=jax.ShapeDtypeStruct((N_PAIRS, NA_ROWS, KEY_WIN, LANES), F32),
        grid=(N_PAIRS,),
        in_specs=[pl.BlockSpec((1, 2, 16, LANES), lambda p: (p, 0, 0, 0)),
                  _resident(ea.shape), _resident(eb.shape)],
        out_specs=pl.BlockSpec((1, NA_ROWS, KEY_WIN, LANES), lambda p: (p, 0, 0, 0)),
        scratch_shapes=[pltpu.VMEM((N_DR, GRID_W, LANES), F32)],
        compiler_params=_params(("parallel",)),
        name="attn_bias",
    )(rp, jnp.asarray(ea), jnp.asarray(eb))


def _attn_kernel(q_ref, k_ref, vx_ref, bias_ref, g_ref, o_ref, *, rows, rb, unroll):
    jb = pl.program_id(1)
    lane = lax.broadcasted_iota(jnp.int32, (GRID_W, LANES), 1)
    first = lane < HEAD_DIM
    keep = (first.astype(F32).astype(BF16), (~first).astype(F32).astype(BF16))
    nt = (((1,), (1,)), ((), ()))
    tn = (((0,), (0,)), ((), ()))

    def score_stage(j):
        r = jb * rb + j
        start = jnp.clip(r - NA_ROWS // 2, 0, rows - NA_ROWS)
        qoff = pl.multiple_of(j * GRID_W, GRID_W)
        koff = pl.multiple_of(start * GRID_W, GRID_W)
        scores = []
        for p in range(N_PAIRS):
            sl = slice(p * LANES, (p + 1) * LANES)
            qp = q_ref[0, pl.ds(qoff, GRID_W), sl]
            qblk = jnp.concatenate([qp * keep[0], qp * keep[1]], axis=0)
            kp = k_ref[0, pl.ds(koff, KEY_WIN), sl]
            scores.append(lax.dot_general(kp, qblk, nt, preferred_element_type=F32))
        return r - start, qoff, koff, scores

    def output_stage(cls, qoff, koff, scores):
        probs = []
        for p in range(N_PAIRS):
            s = scores[p] + bias_ref[p, cls]
            m = jnp.max(s, axis=0, keepdims=True)
            probs.append(jnp.exp2(s - m).astype(BF16))
        outs = []
        for p in range(N_PAIRS):
            vx = vx_ref[0, pl.ds(koff, KEY_WIN), 2 * p * LANES:2 * (p + 1) * LANES]
            o = lax.dot_general(probs[p], vx, tn, preferred_element_type=F32)
            oa = o[:GRID_W, :LANES] / o[:GRID_W, LANES:]
            ob = o[GRID_W:, :LANES] / o[GRID_W:, LANES:]
            outs.append(jnp.where(first, oa, ob))
        o_ref[0, pl.ds(qoff, GRID_W), :] = _rms_norm(jnp.concatenate(outs, axis=1), g_ref[...]).astype(BF16)

    def body(jj, carry):
        state = score_stage(jj * unroll)
        for u in range(unroll):
            ahead = score_stage(jj * unroll + u + 1) if u + 1 < unroll else None
            output_stage(*state)
            state = ahead
        return carry

    lax.fori_loop(0, rb // unroll, body, 0)


def _attention(q, k, vx, bias, g_attn, rb, unroll=8):
    bsz, seq, w = q.shape
    rows = seq // GRID_W
    tq = rb * GRID_W
    return pl.pallas_call(
        functools.partial(_attn_kernel, rows=rows, rb=rb, unroll=unroll),
        out_shape=jax.ShapeDtypeStruct((bsz, seq, w), BF16),
        grid=(bsz, rows // rb),
        in_specs=[
            pl.BlockSpec((1, tq, w), lambda b, i: (b, i, 0)),
            pl.BlockSpec((1, seq, w), lambda b, i: (b, 0, 0)),
            pl.BlockSpec((1, seq, 2 * w), lambda b, i: (b, 0, 0)),
            _resident(bias.shape),
            _resident(g_attn.shape),
        ],
        out_specs=pl.BlockSpec((1, tq, w), lambda b, i: (b, i, 0)),
        compiler_params=_params(("parallel", "arbitrary")),
        name="attention",
    )(q, k, vx, bias, g_attn)


def _filter_kernel(w1t_ref, w1c_ref, w1s_ref, b1_ref, fq_ref, wi_ref, bi_ref, w3_ref,
                   a_ref, d_ref, cb_ref, sb_ref, h_ref, *, seq, tl):
    i = pl.program_id(0)
    rows = tl + SUBLANES
    band = lax.broadcasted_iota(jnp.int32, (rows, LANES), 1)
    freqs = jnp.where(band < FILTER_BANDS,
                      1e-4 + band.astype(F32) * ((FILTER_BANDS - 1 - 1e-4) / (FILTER_BANDS - 1)), 0.0)
    rad = freqs * (2.0 * math.pi / seq)
    local = lax.broadcasted_iota(jnp.int32, (rows, LANES), 0)

    @pl.when(i == 0)
    def _():
        cb_ref[...] = jnp.cos(local.astype(F32) * rad)
        sb_ref[...] = jnp.sin(local.astype(F32) * rad)

    base = (i * tl).astype(F32) * rad[0:SUBLANES]
    ca = jnp.cos(base)[0:1]
    sa = jnp.sin(base)[0:1]
    cos_ang = ca * cb_ref[...] - sa * sb_ref[...]
    sin_ang = sa * cb_ref[...] + ca * sb_ref[...]
    pos = local[:, 0:1] + i * tl
    t = pos.astype(F32) * (1.0 / (seq - 1))
    fq = fq_ref[...]
    pre = (t * w1t_ref[...]
           + jnp.dot(cos_ang, w1c_ref[...], precision=HIGHEST, preferred_element_type=F32)
           - jnp.dot(sin_ang, w1s_ref[...], precision=HIGHEST, preferred_element_type=F32)
           + b1_ref[...])
    h = jnp.sin(fq * pre)
    for n in range(FILTER_INNER):
        h = jnp.sin(fq * (jnp.dot(h, wi_ref[n], precision=HIGHEST, preferred_element_type=F32) + bi_ref[n]))
    chan = lax.broadcasted_iota(jnp.int32, (rows, HYENA_WIDTH), 1).astype(F32)
    deltas = jnp.abs(MIN_DECAY + chan * ((MAX_DECAY - MIN_DECAY) / (HYENA_WIDTH - 1)))
    decay = jnp.exp(-t * deltas)
    for n in range(2):
        cols = slice(n * HYENA_WIDTH, (n + 1) * HYENA_WIDTH)
        h_ref[:, cols] = jnp.dot(h, w3_ref[:, cols], precision=HIGHEST, preferred_element_type=F32) * decay
    h_fwd = h_ref[0:tl, 0:HYENA_WIDTH]
    h_bwd = jnp.where(pos[0:tl] + 1 < seq, h_ref[pl.ds(1, tl), HYENA_WIDTH:2 * HYENA_WIDTH], 0.0)
    a_ref[...] = (h_fwd + h_bwd).astype(BF16)
    d_ref[...] = (h_bwd - h_fwd).astype(BF16)


def _filter_taps(seq, w1, b1, freq, w_inner, b_inner, w3, tl=512):
    pad = LANES - FILTER_BANDS
    w1t = w1[0:1]
    w1c = jnp.pad(w1[1:1 + FILTER_BANDS], ((0, pad), (0, 0)))
    w1s = jnp.pad(w1[1 + FILTER_BANDS:], ((0, pad), (0, 0)))
    args = (w1t, w1c, w1s, b1.reshape(1, -1), freq.reshape(1, -1), w_inner,
            b_inner.reshape(FILTER_INNER, 1, -1), w3)
    out = jax.ShapeDtypeStruct((seq, HYENA_WIDTH), BF16)
    ospec = pl.BlockSpec((tl, HYENA_WIDTH), lambda i: (i, 0))
    return pl.pallas_call(
        functools.partial(_filter_kernel, seq=seq, tl=tl),
        out_shape=(out, out),
        grid=(seq // tl,),
        in_specs=[_resident(a.shape) for a in args],
        out_specs=(ospec, ospec),
        scratch_shapes=[pltpu.VMEM((tl + SUBLANES, LANES), F32), pltpu.VMEM((tl + SUBLANES, LANES), F32),
                        pltpu.VMEM((tl + SUBLANES, 2 * HYENA_WIDTH), F32)],
        compiler_params=_params(("arbitrary",)),
        name="hyena_filter",
    )(*args)


P_BLK = 512
F_HALF = P_BLK // 2


def _pair_coefficients(seq):
    s1n = seq // P_BLK
    g = np.arange(s1n)[:, None]
    s = np.arange(s1n)[None, :]
    ang = np.pi * (2 * g + 1) * s / (2 * s1n)
    return jnp.asarray(np.cos(ang), F32), jnp.asarray(np.sin(ang), F32)


def _dft_kernel(t_ref, tt_ref, cb_ref, sb_ref, *, seq):
    g = pl.program_id(0)
    s1n = seq // P_BLK

    @pl.when(g == 0)
    def _():
        f2 = lax.broadcasted_iota(jnp.int32, (F_HALF, P_BLK), 0)
        s_odd = 2 * lax.broadcasted_iota(jnp.int32, (F_HALF, P_BLK), 1) + 1
        beta = ((f2 * s_odd) & (2 * P_BLK - 1)).astype(F32) * (math.pi / P_BLK)
        cb_ref[...] = jnp.cos(beta)
        sb_ref[...] = jnp.sin(beta)

    cb = cb_ref[...]
    sb = sb_ref[...]
    s_odd = 2 * lax.broadcasted_iota(jnp.int32, (SUBLANES, P_BLK), 1) + 1

    def block(f1):
        alpha = ((s_odd * (2 * f1 + 1)) & (8 * seq - 1)).astype(F32) * (math.pi / (4 * seq))
        ca = jnp.cos(alpha)[0:1]
        sa = jnp.sin(alpha)[0:1]
        return ca * cb - sa * sb, sa * cb + ca * sb

    mra, mia = block(g)
    mrb, mib = block(2 * s1n - 1 - g)
    for r, (left, right) in enumerate(((mra, -mia), (mia, mra), (mrb, mib), (mib, -mrb))):
        rows = slice(r * F_HALF, (r + 1) * F_HALF)
        t_ref[0, rows, :P_BLK] = left.astype(BF16)
        t_ref[0, rows, P_BLK:] = right.astype(BF16)
        tt_ref[0, :P_BLK, rows] = left.T.astype(BF16)
        tt_ref[0, P_BLK:, rows] = right.T.astype(BF16)


def _dft_matrices(seq):
    s1n = seq // P_BLK
    mat = jax.ShapeDtypeStruct((s1n, 2 * P_BLK, 2 * P_BLK), BF16)
    spec = pl.BlockSpec((1, 2 * P_BLK, 2 * P_BLK), lambda g: (g, 0, 0))
    return pl.pallas_call(
        functools.partial(_dft_kernel, seq=seq),
        out_shape=(mat, mat),
        grid=(s1n,),
        out_specs=(spec, spec),
        scratch_shapes=[pltpu.VMEM((F_HALF, P_BLK), F32), pltpu.VMEM((F_HALF, P_BLK), F32)],
        compiler_params=_params(("arbitrary",)),
        name="dft_matrices",
    )()


_SMEM = pl.BlockSpec(memory_space=pltpu.SMEM)
_COL_HALVES = (slice(0, HYENA_WIDTH // 2), slice(HYENA_WIDTH // 2, HYENA_WIDTH))


def _stacked_blocks(cr_ref, sr_ref, g, block, s1n):
    ar = block(0)
    bi = None
    for s in range(1, s1n):
        b = block(s)
        ar = ar + cr_ref[g, s] * b
        bi = sr_ref[g, s] * b if bi is None else bi + sr_ref[g, s] * b
    return jnp.concatenate([ar, bi], axis=0).astype(BF16)


def _spectrum_kernel(cr_ref, sr_ref, t_ref, a_ref, d_ref, g_ref, *, seq):
    g = pl.program_id(0)
    s1n = seq // P_BLK
    time_block = lambda ref: (lambda s: ref[s * P_BLK:(s + 1) * P_BLK, :].astype(F32))
    t_re = jnp.concatenate([t_ref[0, 0:F_HALF], t_ref[0, P_BLK:P_BLK + F_HALF]], axis=0)
    t_im = jnp.concatenate([t_ref[0, F_HALF:P_BLK], t_ref[0, P_BLK + F_HALF:2 * P_BLK]], axis=0)
    ka = jnp.dot(t_re, _stacked_blocks(cr_ref, sr_ref, g, time_block(a_ref), s1n),
                 preferred_element_type=F32)
    kd = jnp.dot(t_im, _stacked_blocks(cr_ref, sr_ref, g, time_block(d_ref), s1n),
                 preferred_element_type=F32)
    f2 = lax.broadcasted_iota(jnp.int32, (F_HALF, LANES), 0)
    reps = HYENA_WIDTH // LANES
    for blk, f1 in ((0, g), (1, 2 * s1n - 1 - g)):
        lo = blk * P_BLK
        phi = (2 * (f1 + 2 * s1n * f2) + 1).astype(F32) * (math.pi / (4 * seq))
        c = pltpu.repeat(jnp.cos(phi), reps, axis=1) * (1.0 / seq)
        s = pltpu.repeat(jnp.sin(phi), reps, axis=1) * (1.0 / seq)
        kr = ka[blk * F_HALF:(blk + 1) * F_HALF]
        ki = kd[blk * F_HALF:(blk + 1) * F_HALF]
        g_ref[0, lo:lo + F_HALF] = c * kr - s * ki
        g_ref[0, lo + F_HALF:lo + P_BLK] = c * ki + s * kr


def _filter_spectrum(cr, sr, t, a, d):
    seq = a.shape[0]
    s1n = seq // P_BLK
    return pl.pallas_call(
        functools.partial(_spectrum_kernel, seq=seq),
        out_shape=jax.ShapeDtypeStruct((s1n, 2 * P_BLK, HYENA_WIDTH), F32),
        grid=(s1n,),
        in_specs=[_SMEM, _SMEM, pl.BlockSpec((1, 2 * P_BLK, 2 * P_BLK), lambda g: (g, 0, 0)),
                  _resident(a.shape), _resident(d.shape)],
        out_specs=pl.BlockSpec((1, 2 * P_BLK, HYENA_WIDTH), lambda g: (g, 0, 0)),
        compiler_params=_params(("parallel",)),
        name="hyena_spectrum",
    )(cr, sr, t, a, d)


def _hy_fwd_kernel(cr_ref, sr_ref, t_ref, g_ref, z_ref, y_ref, *, s1n, nb):
    g = pl.program_id(0)
    for bi in range(nb):
        for cols in _COL_HALVES:
            x = _stacked_blocks(cr_ref, sr_ref, g,
                                lambda s: z_ref[bi, s * P_BLK:(s + 1) * P_BLK, cols].astype(F32), s1n)
            res = jnp.dot(t_ref[0], x, preferred_element_type=F32)
            for blk in range(2):
                lo = blk * P_BLK
                zr = res[lo:lo + F_HALF]
                wi = res[lo + F_HALF:lo + P_BLK]
                gr = g_ref[0, lo:lo + F_HALF, cols]
                gi = g_ref[0, lo + F_HALF:lo + P_BLK, cols]
                y_ref[bi, 0, lo:lo + F_HALF, cols] = (gr * zr + gi * wi).astype(BF16)
                y_ref[bi, 0, lo + F_HALF:lo + P_BLK, cols] = (gr * wi - gi * zr).astype(BF16)


def _hy_inv_kernel(cr_ref, sr_ref, tt_ref, y_ref, z_ref, g0_ref, fb_ref, gn_ref, o_ref,
                   acc_ref, zs_ref, g0s_ref, *, s1n, nb):
    g = pl.program_id(1)

    @pl.when(g == 0)
    def _():
        acc_ref[...] = jnp.zeros_like(acc_ref)

    tblk = pl.ds(pl.multiple_of(g * P_BLK, P_BLK), P_BLK)
    zs_ref[:, tblk, :] = z_ref[...]
    g0s_ref[:, tblk, :] = g0_ref[...]
    for bi in range(nb):
        for cols in _COL_HALVES:
            uv = jnp.dot(tt_ref[g], y_ref[bi, 0, :, cols], preferred_element_type=F32)
            u = uv[:P_BLK]
            vn = uv[P_BLK:]
            acc_ref[bi, 0:P_BLK, cols] += u
            for t1 in range(1, s1n):
                rows = slice(t1 * P_BLK, (t1 + 1) * P_BLK)
                acc_ref[bi, rows, cols] += cr_ref[g, t1] * u + sr_ref[g, t1] * vn

    @pl.when(g == s1n - 1)
    def _():
        for bi in range(nb):
            for t1 in range(s1n):
                rows = slice(t1 * P_BLK, (t1 + 1) * P_BLK)
                y = ((acc_ref[bi, rows, :] + zs_ref[bi, rows, :].astype(F32) * fb_ref[...])
                     * g0s_ref[bi, rows, :].astype(F32))
                o_ref[bi, rows, :] = _rms_norm(y, gn_ref[...]).astype(BF16)


_INV_STATE_BUDGET = 28 * 1024 * 1024


def _hyena_conv(zg, g0, cr, sr, t, tt, gspec, fbias, g_hyena):
    bsz, seq, w = zg.shape
    s1n = seq // P_BLK
    nb = 2 if bsz % 2 == 0 else 1
    mat = pl.BlockSpec((1, 2 * P_BLK, 2 * P_BLK), lambda g, b: (g, 0, 0))
    spec = pl.BlockSpec((1, 2 * P_BLK, w), lambda g, b: (g, 0, 0))
    y = pl.pallas_call(
        functools.partial(_hy_fwd_kernel, s1n=s1n, nb=nb),
        out_shape=jax.ShapeDtypeStruct((bsz, s1n, 2 * P_BLK, w), BF16),
        grid=(s1n, bsz // nb),
        in_specs=[_SMEM, _SMEM, mat, spec, pl.BlockSpec((nb, seq, w), lambda g, b: (b, 0, 0))],
        out_specs=pl.BlockSpec((nb, 1, 2 * P_BLK, w), lambda g, b: (b, g, 0, 0)),
        compiler_params=_params(("parallel", "parallel")),
        name="hyena_fwd",
    )(cr, sr, t, gspec, zg)
    state_bytes = seq * w * (4 + 2 + 2 + 2 * 2)
    nb = 2 if bsz % 2 == 0 and 2 * state_bytes <= _INV_STATE_BUDGET else 1
    blk = pl.BlockSpec((nb, P_BLK, w), lambda b, g: (b, g, 0))
    return pl.pallas_call(
        functools.partial(_hy_inv_kernel, s1n=s1n, nb=nb),
        out_shape=jax.ShapeDtypeStruct((bsz, seq, w), BF16),
        grid=(bsz // nb, s1n),
        in_specs=[_SMEM, _SMEM, _resident(tt.shape),
                  pl.BlockSpec((nb, 1, 2 * P_BLK, w), lambda b, g: (b, g, 0, 0)),
                  blk, blk, _resident(fbias.shape), _resident(g_hyena.shape)],
        out_specs=pl.BlockSpec((nb, seq, w), lambda b, g: (b, 0, 0)),
        scratch_shapes=[pltpu.VMEM((nb, seq, w), F32), pltpu.VMEM((nb, seq, w), BF16),
                        pltpu.VMEM((nb, seq, w), BF16)],
        compiler_params=_params(("parallel", "arbitrary")),
        name="hyena_inv",
    )(cr, sr, tt, y, zg, g0, fbias, g_hyena)


def _merge_kernel(ma_ref, mh_ref, x_ref, wa_ref, wh_ref, g_ref, b_ref, o_ref):
    y = (jnp.dot(ma_ref[0], wa_ref[...], preferred_element_type=F32)
         + jnp.dot(mh_ref[0], wh_ref[...], preferred_element_type=F32))
    o_ref[0] = _layer_norm(ALPHA * x_ref[0] + y, g_ref[...], b_ref[...])


def _merge(ma, mh, x, w_out, g, b, tl):
    bsz, seq, d = x.shape
    wa, wh = w_out[:ATTN_WIDTH], w_out[ATTN_WIDTH:]
    half = pl.BlockSpec((1, tl, ATTN_WIDTH), lambda bb, i: (bb, i, 0))
    full = pl.BlockSpec((1, tl, d), lambda bb, i: (bb, i, 0))
    return pl.pallas_call(
        _merge_kernel,
        out_shape=jax.ShapeDtypeStruct(x.shape, F32),
        grid=(bsz, seq // tl),
        in_specs=[half, half, full, _resident(wa.shape), _resident(wh.shape),
                  _resident(g.shape), _resident(b.shape)],
        out_specs=full,
        compiler_params=_params(("parallel", "parallel")),
        name="merge_ln1",
    )(ma, mh, x, wa, wh, g, b)


FF_CHUNK = 256
N_FF_CHUNKS = D_FF // FF_CHUNK


def _ffn_kernel(xp_ref, x_ref, xn_ref, wi_ref, cw_ref, cb_ref, wo_ref, g_ref, b_ref,
                o_ref, xs_ref, hid_ref, os_ref, *, tl):
    _fill_halo_slabs(xs_ref, xp_ref, x_ref, xn_ref, tl)
    xh = _load_interleaved(xs_ref).astype(BF16)
    for j in range(N_FF_CHUNKS):
        conv = []
        for part in range(2):
            cols = slice(part * D_FF + j * FF_CHUNK, part * D_FF + (j + 1) * FF_CHUNK)
            u = jnp.dot(xh, wi_ref[:, cols], preferred_element_type=F32)
            conv.append(_dwconv3_interleaved(u, cw_ref[:, cols], cb_ref[:, cols]))
        gate = conv[1]
        gelu = 0.5 * gate * (1.0 + lax.erf(gate * (2.0 ** -0.5)))
        hid_ref[:, j * FF_CHUNK:(j + 1) * FF_CHUNK] = (conv[0] * gelu).astype(BF16)
    y = jnp.dot(hid_ref[...], wo_ref[...], preferred_element_type=F32)
    out = _layer_norm(ALPHA * _load_interleaved(xs_ref) + y, g_ref[...], b_ref[...])
    _store_natural(os_ref, 0, out)
    for k in range(os_ref.shape[0]):
        o_ref[0, :, k * LANES:(k + 1) * LANES] = os_ref[k, HALO:HALO + tl, :]


def _ffn(x1, w_in, conv_w, conv_b, w_out, g, b, tl):
    bsz, seq, d = x1.shape
    cb = conv_b.reshape(1, -1)
    return pl.pallas_call(
        functools.partial(_ffn_kernel, tl=tl),
        out_shape=jax.ShapeDtypeStruct(x1.shape, F32),
        grid=(bsz, seq // tl),
        in_specs=_halo_specs(tl, seq, d) + [_resident(w_in.shape), _resident(conv_w.shape), _resident(cb.shape),
                                            _resident(w_out.shape), _resident(g.shape), _resident(b.shape)],
        out_specs=pl.BlockSpec((1, tl, d), lambda bb, i: (bb, i, 0)),
        scratch_shapes=[pltpu.VMEM((d // LANES, tl + 2 * HALO, LANES), F32),
                        pltpu.VMEM((tl + 2 * HALO, D_FF), BF16),
                        pltpu.VMEM((d // LANES, tl + 2 * HALO, LANES), F32)],
        compiler_params=_params(("parallel", "parallel")),
        name="conv_ffn",
    )(x1, x1, x1, w_in, conv_w, cb, w_out, g, b)


def _encoder_layer(x, p, tl=1024, rb=8):
    seq = x.shape[1]
    q, k, v, zg, g0 = _in_proj(x, p["w_in"], p["short_w"], p["short_b"], tl)
    ma = _attention(q, k, v, p["bias"], p["g_attn"], rb)
    a, d = _filter_taps(seq, p["filt_w1"], p["filt_b1"], p["filt_freq"], p["filt_w_inner"],
                        p["filt_b_inner"], p["filt_w3"])
    cr, sr = _pair_coefficients(seq)
    t, tt = _dft_matrices(seq)
    gspec = _filter_spectrum(cr, sr, t, a, d)
    mh = _hyena_conv(zg, g0, cr, sr, t, tt, gspec, p["filt_bias"], p["g_hyena"])
    x1 = _merge(ma, mh, x, p["w_out"], p["ln1_g"], p["ln1_b"], tl)
    return _ffn(x1, p["ffn_w_in"], p["ffn_conv_w"], p["ffn_conv_b"], p["ffn_w_out"],
                p["ln2_g"], p["ln2_b"], tl)


def kernel(x_prompt, x_sample, w_in, short_w, short_b, rpb, filt_w1, filt_b1, filt_freq, filt_w_inner,
           filt_b_inner, filt_w3, filt_bias, g_attn, g_hyena, w_out, ln1_g, ln1_b, ffn_w_in, ffn_conv_w,
           ffn_conv_b, ffn_w_out, ln2_g, ln2_b):
    assert w_in.shape[0] == DEPTH == 1
    row = lambda a: a[0].reshape(1, -1)
    p = dict(
        w_in=w_in[0].astype(BF16), short_w=short_w[0], short_b=short_b[0],
        bias=_attention_bias(rpb[0]), g_attn=row(g_attn),
        filt_w1=filt_w1[0], filt_b1=filt_b1[0], filt_freq=filt_freq[0], filt_w_inner=filt_w_inner[0],
        filt_b_inner=filt_b_inner[0], filt_w3=filt_w3[0], filt_bias=filt_bias[0], g_hyena=row(g_hyena),
        w_out=w_out[0].astype(BF16), ln1_g=row(ln1_g), ln1_b=row(ln1_b),
        ffn_w_in=ffn_w_in[0].astype(BF16), ffn_conv_w=ffn_conv_w[0], ffn_conv_b=ffn_conv_b[0],
        ffn_w_out=ffn_w_out[0].astype(BF16), ln2_g=row(ln2_g), ln2_b=row(ln2_b),
    )
    return (_encoder_layer(x_prompt, p), _encoder_layer(x_sample, p))
```

```python
import functools
import math

import numpy as np
import jax
import jax.numpy as jnp
from jax import lax
from jax.experimental import pallas as pl
from jax.experimental.pallas import tpu as pltpu

F32 = jnp.float32
BF16 = jnp.bfloat16

D_MODEL = 1024
GRID_W = 64
ATTN_WIDTH = 512
HYENA_WIDTH = 512
HEAD_DIM = 64
N_HEADS = ATTN_WIDTH // HEAD_DIM
NA_ROWS = 8
NA_COLS = 16
FILTER_EMB = 33
FILTER_BANDS = (FILTER_EMB - 1) // 2
FILTER_HIDDEN = 64
FILTER_INNER = 2
MAX_DECAY = math.log(1e-2) / 0.3
MIN_DECAY = math.log(1e-2) / 1.5
D_FF = 2816
DEPTH = 1
ALPHA = (2 * DEPTH) ** 0.25
LN_EPS = 1e-5
RMS_EPS = 1e-6
LOG2E = math.log2(math.e)

SUBLANES = 8
LANES = 128
HALO = SUBLANES
VMEM_LIMIT = 56 * 1024 * 1024

HIGHEST = lax.Precision.HIGHEST


def _params(sem):
    return pltpu.CompilerParams(dimension_semantics=sem, vmem_limit_bytes=VMEM_LIMIT)


def _resident(shape):
    nd = len(shape)
    return pl.BlockSpec(shape, lambda *_: (0,) * nd, pipeline_mode=pl.Buffered(1))


def _layer_norm(y, g, b):
    mu = jnp.mean(y, axis=-1, keepdims=True)
    yc = y - mu
    var = jnp.mean(yc * yc, axis=-1, keepdims=True)
    return yc * lax.rsqrt(var + LN_EPS) * g + b


def _rms_norm(y, g):
    ms = jnp.mean(y * y, axis=-1, keepdims=True)
    return y * lax.rsqrt(ms + RMS_EPS) * g


def _fill_halo_slabs(xs_ref, xp_ref, x_ref, xn_ref, tl):
    i = pl.program_id(1)
    last = pl.num_programs(1) - 1
    prev = jnp.where(i > 0, xp_ref[0], 0.0)
    nxt = jnp.where(i < last, xn_ref[0], 0.0)
    for k in range(xs_ref.shape[0]):
        sl = slice(k * LANES, (k + 1) * LANES)
        xs_ref[k, 0:HALO, :] = prev[:, sl]
        xs_ref[k, HALO:HALO + tl, :] = x_ref[0, :, sl]
        xs_ref[k, HALO + tl:HALO + tl + HALO, :] = nxt[:, sl]


def _load_interleaved(xs_ref):
    nslab, rows, _ = xs_ref.shape
    nv = rows // SUBLANES
    assert rows % SUBLANES == 0 and nv % SUBLANES != 0
    return jnp.concatenate(
        [jnp.concatenate([xs_ref[k, pl.ds(j, SUBLANES, stride=nv), :] for j in range(nv)], axis=0)
         for k in range(nslab)], axis=1)


def _store_natural(os_ref, first, val):
    nv = os_ref.shape[1] // SUBLANES
    for k in range(val.shape[1] // LANES):
        for j in range(nv):
            os_ref[first + k, pl.ds(j, SUBLANES, stride=nv), :] = val[j * SUBLANES:(j + 1) * SUBLANES,
                                                                      k * LANES:(k + 1) * LANES]


def _dwconv3_interleaved(u, w, b):
    head = pltpu.roll(u[-SUBLANES:], 1, axis=0)
    tail = pltpu.roll(u[:SUBLANES], SUBLANES - 1, axis=0)
    up = jnp.concatenate([head, u[:-SUBLANES]], axis=0)
    un = jnp.concatenate([u[SUBLANES:], tail], axis=0)
    return up * w[0:1] + u * w[1:2] + un * w[2:3] + b


def _halo_specs(tl, seq, d):
    nb = tl // HALO
    last = seq // HALO - 1
    return [
        pl.BlockSpec((1, HALO, d), lambda b, i: (b, jnp.maximum(i * nb - 1, 0), 0)),
        pl.BlockSpec((1, tl, d), lambda b, i: (b, i, 0)),
        pl.BlockSpec((1, HALO, d), lambda b, i: (b, jnp.minimum((i + 1) * nb, last), 0)),
    ]


def _in_proj_kernel(xp_ref, x_ref, xn_ref, w_ref, sw_ref, sb_ref,
                    q_ref, k_ref, v_ref, zg_ref, g0_ref, xs_ref, os_ref, *, tl):
    xq = x_ref[0].astype(BF16)
    proj = lambda n: jnp.dot(xq, w_ref[:, n * ATTN_WIDTH:(n + 1) * ATTN_WIDTH], preferred_element_type=F32)
    q_ref[0] = (proj(0) * (HEAD_DIM ** -0.5 * LOG2E)).astype(BF16)
    k_ref[0] = proj(1).astype(BF16)
    v = proj(2).astype(BF16)
    ones = jnp.ones((tl, LANES), BF16)
    v_ref[0] = jnp.concatenate([piece for p in range(N_HEADS // 2)
                                for piece in (v[:, p * LANES:(p + 1) * LANES], ones)], axis=1)
    _fill_halo_slabs(xs_ref, xp_ref, x_ref, xn_ref, tl)
    xh = _load_interleaved(xs_ref).astype(BF16)
    conv = []
    for n in range(3):
        lo = 3 * ATTN_WIDTH + n * HYENA_WIDTH
        u = jnp.dot(xh, w_ref[:, lo:lo + HYENA_WIDTH], preferred_element_type=F32)
        conv.append(_dwconv3_interleaved(u, sw_ref[n], sb_ref[n]))
    nslab = HYENA_WIDTH // LANES
    for first, ref, val in ((0, g0_ref, conv[0]), (nslab, zg_ref, conv[2] * conv[1])):
        _store_natural(os_ref, first, val)
        for k in range(nslab):
            ref[0, :, k * LANES:(k + 1) * LANES] = os_ref[first + k, HALO:HALO + tl, :].astype(BF16)


def _in_proj(x, w_in, short_w, short_b, tl):
    bsz, seq, d = x.shape
    sw = short_w.reshape(3, 3, HYENA_WIDTH).transpose(1, 0, 2)
    sb = short_b.reshape(3, 1, HYENA_WIDTH)
    out = jax.ShapeDtypeStruct((bsz, seq, ATTN_WIDTH), BF16)
    ospec = pl.BlockSpec((1, tl, ATTN_WIDTH), lambda b, i: (b, i, 0))
    vout = jax.ShapeDtypeStruct((bsz, seq, 2 * ATTN_WIDTH), BF16)
    vspec = pl.BlockSpec((1, tl, 2 * ATTN_WIDTH), lambda b, i: (b, i, 0))
    return pl.pallas_call(
        functools.partial(_in_proj_kernel, tl=tl),
        out_shape=(out, out, vout, out, out),
        grid=(bsz, seq // tl),
        in_specs=_halo_specs(tl, seq, d) + [_resident(w_in.shape), _resident(sw.shape), _resident(sb.shape)],
        out_specs=(ospec, ospec, vspec, ospec, ospec),
        scratch_shapes=[pltpu.VMEM((d // LANES, tl + 2 * HALO, LANES), F32),
                        pltpu.VMEM((2 * HYENA_WIDTH // LANES, tl + 2 * HALO, LANES), F32)],
        compiler_params=_params(("parallel", "parallel")),
        name="in_proj",
    )(x, x, x, w_in, sw, sb)


N_PAIRS = N_HEADS // 2
KEY_WIN = NA_ROWS * GRID_W
N_DR = 2 * NA_ROWS - 1
N_DC = 2 * NA_COLS - 1
SOFTMAX_PARTS = 4


def _bias_kernel(rpb_ref, ea_ref, eb_ref, o_ref, t_ref):
    row = lax.broadcasted_iota(jnp.int32, (GRID_W, LANES), 0)
    lane = lax.broadcasted_iota(jnp.int32, (GRID_W, LANES), 1)
    first = lane < GRID_W
    qcol = lane & (GRID_W - 1)
    col_start = jnp.clip(qcol - NA_COLS // 2, 0, GRID_W - NA_COLS)
    valid = (row >= col_start) & (row < col_start + NA_COLS)
    wa = jnp.dot(rpb_ref[0, 0], ea_ref[...], precision=HIGHEST, preferred_element_type=F32)
    wb = jnp.dot(rpb_ref[0, 1], eb_ref[...], precision=HIGHEST, preferred_element_type=F32)

    def rotate_rows(x):
        for t in range(GRID_W.bit_length() - 1):
            x = jnp.where(((row >> t) & 1) == 1, pltpu.roll(x, 1 << t, axis=1), x)
        return x

    for dr in range(N_DR):
        ta = rotate_rows(jnp.broadcast_to(wa[dr:dr + 1], (GRID_W, LANES)))
        tb = rotate_rows(jnp.broadcast_to(wb[dr:dr + 1], (GRID_W, LANES)))
        t_ref[dr] = jnp.where(valid, jnp.where(first, ta, tb) * LOG2E, -jnp.inf)
    for cls in range(NA_ROWS):
        for i in range(NA_ROWS):
            o_ref[0, cls, i * GRID_W:(i + 1) * GRID_W, :] = t_ref[i - cls + NA_ROWS - 1]


def _toeplitz_selectors():
    m = np.arange(LANES)
    dc = np.where(m < GRID_W, (NA_COLS - 1) - np.minimum(m, NA_COLS - 1),
                  np.minimum(LANES - m, NA_COLS - 1) + (NA_COLS - 1))
    ea = np.zeros((LANES, LANES), np.float32)
    ea[dc, m] = 1.0
    ea[:, GRID_W] = 0.0
    eb = np.roll(ea, GRID_W, axis=1)
    return ea, eb


def _attention_bias(rpb):
    ea, eb = _toeplitz_selectors()
    rp = jnp.pad(rpb.astype(F32), ((0, 0), (0, 16 - N_DR), (0, LANES - N_DC))).reshape(N_PAIRS, 2, 16, LANES)
    return pl.pallas_call(
        _bias_kernel,
        out_shape=jax.ShapeDtypeStruct((N_PAIRS, NA_ROWS, KEY_WIN, LANES), F32),
        grid=(N_PAIRS,),
        in_specs=[pl.BlockSpec((1, 2, 16, LANES), lambda p: (p, 0, 0, 0)),
                  _resident(ea.shape), _resident(eb.shape)],
        out_specs=pl.BlockSpec((1, NA_ROWS, KEY_WIN, LANES), lambda p: (p, 0, 0, 0)),
        scratch_shapes=[pltpu.VMEM((N_DR, GRID_W, LANES), F32)],
        compiler_params=_params(("parallel",)),
        name="attn_bias",
    )(rp, jnp.asarray(ea), jnp.asarray(eb))


def _attn_kernel(q_ref, k_ref, vx_ref, bias_ref, g_ref, o_ref, *, rows, rb, unroll):
    jb = pl.program_id(1)
    lane = lax.broadcasted_iota(jnp.int32, (GRID_W, LANES), 1)
    first = lane < HEAD_DIM
    keep = (first.astype(F32).astype(BF16), (~first).astype(F32).astype(BF16))
    nt = (((1,), (1,)), ((), ()))
    tn = (((0,), (0,)), ((), ()))

    def score_stage(j):
        r = jb * rb + j
        start = jnp.clip(r - NA_ROWS // 2, 0, rows - NA_ROWS)
        qoff = pl.multiple_of(j * GRID_W, GRID_W)
        koff = pl.multiple_of(start * GRID_W, GRID_W)
        scores = []
        for p in range(N_PAIRS):
            sl = slice(p * LANES, (p + 1) * LANES)
            qp = q_ref[0, pl.ds(qoff, GRID_W), sl]
            qblk = jnp.concatenate([qp * keep[0], qp * keep[1]], axis=0)
            kp = k_ref[0, pl.ds(koff, KEY_WIN), sl]
            scores.append(lax.dot_general(kp, qblk, nt, preferred_element_type=F32))
        return r - start, qoff, koff, scores

    def output_stage(cls, qoff, koff, scores):
        probs = []
        step = KEY_WIN // SOFTMAX_PARTS
        for p in range(N_PAIRS):
            parts, maxes = [], []
            for lo in range(0, KEY_WIN, step):
                s = scores[p][lo:lo + step] + bias_ref[p, cls, lo:lo + step]
                m_h = jnp.max(s, axis=0, keepdims=True)
                parts.append(jnp.exp2(s - m_h))
                maxes.append(m_h)
            m = functools.reduce(jnp.maximum, maxes)
            probs.append(jnp.concatenate([e * jnp.exp2(m_h - m) for e, m_h in zip(parts, maxes)],
                                         axis=0).astype(BF16))
        outs = []
        for p in range(N_PAIRS):
            vx = vx_ref[0, pl.ds(koff, KEY_WIN), 2 * p * LANES:2 * (p + 1) * LANES]
            o = lax.dot_general(probs[p], vx, tn, preferred_element_type=F32)
            oa = o[:GRID_W, :LANES] / o[:GRID_W, LANES:]
            ob = o[GRID_W:, :LANES] / o[GRID_W:, LANES:]
            outs.append(jnp.where(first, oa, ob))
        o_ref[0, pl.ds(qoff, GRID_W), :] = _rms_norm(jnp.concatenate(outs, axis=1), g_ref[...]).astype(BF16)

    def body(jj, carry):
        state = score_stage(jj * unroll)
        for u in range(unroll):
            ahead = score_stage(jj * unroll + u + 1) if u + 1 < unroll else None
            output_stage(*state)
            state = ahead
        return carry

    lax.fori_loop(0, rb // unroll, body, 0)


def _attention(q, k, vx, bias, g_attn, rb, unroll=8):
    bsz, seq, w = q.shape
    rows = seq // GRID_W
    tq = rb * GRID_W
    return pl.pallas_call(
        functools.partial(_attn_kernel, rows=rows, rb=rb, unroll=unroll),
        out_shape=jax.ShapeDtypeStruct((bsz, seq, w), BF16),
        grid=(bsz, rows // rb),
        in_specs=[
            pl.BlockSpec((1, tq, w), lambda b, i: (b, i, 0)),
            pl.BlockSpec((1, seq, w), lambda b, i: (b, 0, 0)),
            pl.BlockSpec((1, seq, 2 * w), lambda b, i: (b, 0, 0)),
            _resident(bias.shape),
            _resident(g_attn.shape),
        ],
        out_specs=pl.BlockSpec((1, tq, w), lambda b, i: (b, i, 0)),
        compiler_params=_params(("parallel", "arbitrary")),
        name="attention",
    )(q, k, vx, bias, g_attn)


def _filter_kernel(w1t_ref, w1c_ref, w1s_ref, b1_ref, fq_ref, wi_ref, bi_ref, w3_ref,
                   a_ref, d_ref, cb_ref, sb_ref, h_ref, *, seq, tl):
    i = pl.program_id(0)
    rows = tl + SUBLANES
    band = lax.broadcasted_iota(jnp.int32, (rows, LANES), 1)
    freqs = jnp.where(band < FILTER_BANDS,
                      1e-4 + band.astype(F32) * ((FILTER_BANDS - 1 - 1e-4) / (FILTER_BANDS - 1)), 0.0)
    rad = freqs * (2.0 * math.pi / seq)
    local = lax.broadcasted_iota(jnp.int32, (rows, LANES), 0)

    @pl.when(i == 0)
    def _():
        cb_ref[...] = jnp.cos(local.astype(F32) * rad)
        sb_ref[...] = jnp.sin(local.astype(F32) * rad)

    base = (i * tl).astype(F32) * rad[0:SUBLANES]
    ca = jnp.cos(base)[0:1]
    sa = jnp.sin(base)[0:1]
    cos_ang = ca * cb_ref[...] - sa * sb_ref[...]
    sin_ang = sa * cb_ref[...] + ca * sb_ref[...]
    pos = local[:, 0:1] + i * tl
    t = pos.astype(F32) * (1.0 / (seq - 1))
    fq = fq_ref[...]
    pre = (t * w1t_ref[...]
           + jnp.dot(cos_ang, w1c_ref[...], precision=HIGHEST, preferred_element_type=F32)
           - jnp.dot(sin_ang, w1s_ref[...], precision=HIGHEST, preferred_element_type=F32)
           + b1_ref[...])
    h = jnp.sin(fq * pre)
    for n in range(FILTER_INNER):
        h = jnp.sin(fq * (jnp.dot(h, wi_ref[n], precision=HIGHEST, preferred_element_type=F32) + bi_ref[n]))
    chan = lax.broadcasted_iota(jnp.int32, (rows, HYENA_WIDTH), 1).astype(F32)
    deltas = jnp.abs(MIN_DECAY + chan * ((MAX_DECAY - MIN_DECAY) / (HYENA_WIDTH - 1)))
    decay = jnp.exp(-t * deltas)
    for n in range(2):
        cols = slice(n * HYENA_WIDTH, (n + 1) * HYENA_WIDTH)
        h_ref[:, cols] = jnp.dot(h, w3_ref[:, cols], precision=HIGHEST, preferred_element_type=F32) * decay
    h_fwd = h_ref[0:tl, 0:HYENA_WIDTH]
    h_bwd = jnp.where(pos[0:tl] + 1 < seq, h_ref[pl.ds(1, tl), HYENA_WIDTH:2 * HYENA_WIDTH], 0.0)
    a_ref[...] = (h_fwd + h_bwd).astype(BF16)
    d_ref[...] = (h_bwd - h_fwd).astype(BF16)


def _filter_taps(seq, w1, b1, freq, w_inner, b_inner, w3, tl=512):
    pad = LANES - FILTER_BANDS
    w1t = w1[0:1]
    w1c = jnp.pad(w1[1:1 + FILTER_BANDS], ((0, pad), (0, 0)))
    w1s = jnp.pad(w1[1 + FILTER_BANDS:], ((0, pad), (0, 0)))
    args = (w1t, w1c, w1s, b1.reshape(1, -1), freq.reshape(1, -1), w_inner,
            b_inner.reshape(FILTER_INNER, 1, -1), w3)
    out = jax.ShapeDtypeStruct((seq, HYENA_WIDTH), BF16)
    ospec = pl.BlockSpec((tl, HYENA_WIDTH), lambda i: (i, 0))
    return pl.pallas_call(
        functools.partial(_filter_kernel, seq=seq, tl=tl),
        out_shape=(out, out),
        grid=(seq // tl,),
        in_specs=[_resident(a.shape) for a in args],
        out_specs=(ospec, ospec),
        scratch_shapes=[pltpu.VMEM((tl + SUBLANES, LANES), F32), pltpu.VMEM((tl + SUBLANES, LANES), F32),
                        pltpu.VMEM((tl + SUBLANES, 2 * HYENA_WIDTH), F32)],
        compiler_params=_params(("arbitrary",)),
        name="hyena_filter",
    )(*args)


P_BLK = 512
F_HALF = P_BLK // 2


def _pair_coefficients(seq):
    s1n = seq // P_BLK
    g = np.arange(s1n)[:, None]
    s = np.arange(s1n)[None, :]
    ang = np.pi * (2 * g + 1) * s / (2 * s1n)
    return jnp.asarray(np.cos(ang), F32), jnp.asarray(np.sin(ang), F32)


def _dft_kernel(t_ref, tt_ref, cb_ref, sb_ref, *, seq):
    g = pl.program_id(0)
    s1n = seq // P_BLK

    @pl.when(g == 0)
    def _():
        f2 = lax.broadcasted_iota(jnp.int32, (F_HALF, P_BLK), 0)
        s_odd = 2 * lax.broadcasted_iota(jnp.int32, (F_HALF, P_BLK), 1) + 1
        beta = ((f2 * s_odd) & (2 * P_BLK - 1)).astype(F32) * (math.pi / P_BLK)
        cb_ref[...] = jnp.cos(beta)
        sb_ref[...] = jnp.sin(beta)

    cb = cb_ref[...]
    sb = sb_ref[...]
    s_odd = 2 * lax.broadcasted_iota(jnp.int32, (SUBLANES, P_BLK), 1) + 1

    def block(f1):
        alpha = ((s_odd * (2 * f1 + 1)) & (8 * seq - 1)).astype(F32) * (math.pi / (4 * seq))
        ca = jnp.cos(alpha)[0:1]
        sa = jnp.sin(alpha)[0:1]
        return ca * cb - sa * sb, sa * cb + ca * sb

    mra, mia = block(g)
    mrb, mib = block(2 * s1n - 1 - g)
    for r, (left, right) in enumerate(((mra, -mia), (mia, mra), (mrb, mib), (mib, -mrb))):
        rows = slice(r * F_HALF, (r + 1) * F_HALF)
        t_ref[0, rows, :P_BLK] = left.astype(BF16)
        t_ref[0, rows, P_BLK:] = right.astype(BF16)
        tt_ref[0, :P_BLK, rows] = left.T.astype(BF16)
        tt_ref[0, P_BLK:, rows] = right.T.astype(BF16)


def _dft_matrices(seq):
    s1n = seq // P_BLK
    mat = jax.ShapeDtypeStruct((s1n, 2 * P_BLK, 2 * P_BLK), BF16)
    spec = pl.BlockSpec((1, 2 * P_BLK, 2 * P_BLK), lambda g: (g, 0, 0))
    return pl.pallas_call(
        functools.partial(_dft_kernel, seq=seq),
        out_shape=(mat, mat),
        grid=(s1n,),
        out_specs=(spec, spec),
        scratch_shapes=[pltpu.VMEM((F_HALF, P_BLK), F32), pltpu.VMEM((F_HALF, P_BLK), F32)],
        compiler_params=_params(("arbitrary",)),
        name="dft_matrices",
    )()


_SMEM = pl.BlockSpec(memory_space=pltpu.SMEM)
_COL_HALVES = (slice(0, HYENA_WIDTH // 2), slice(HYENA_WIDTH // 2, HYENA_WIDTH))


def _stacked_blocks(cr_ref, sr_ref, g, block, s1n):
    ar = block(0)
    bi = None
    for s in range(1, s1n):
        b = block(s)
        ar = ar + cr_ref[g, s] * b
        bi = sr_ref[g, s] * b if bi is None else bi + sr_ref[g, s] * b
    return jnp.concatenate([ar, bi], axis=0).astype(BF16)


def _spectrum_kernel(cr_ref, sr_ref, t_ref, a_ref, d_ref, g_ref, *, seq):
    g = pl.program_id(0)
    s1n = seq // P_BLK
    time_block = lambda ref: (lambda s: ref[s * P_BLK:(s + 1) * P_BLK, :].astype(F32))
    t_re = jnp.concatenate([t_ref[0, 0:F_HALF], t_ref[0, P_BLK:P_BLK + F_HALF]], axis=0)
    t_im = jnp.concatenate([t_ref[0, F_HALF:P_BLK], t_ref[0, P_BLK + F_HALF:2 * P_BLK]], axis=0)
    ka = jnp.dot(t_re, _stacked_blocks(cr_ref, sr_ref, g, time_block(a_ref), s1n),
                 preferred_element_type=F32)
    kd = jnp.dot(t_im, _stacked_blocks(cr_ref, sr_ref, g, time_block(d_ref), s1n),
                 preferred_element_type=F32)
    f2 = lax.broadcasted_iota(jnp.int32, (F_HALF, LANES), 0)
    reps = HYENA_WIDTH // LANES
    for blk, f1 in ((0, g), (1, 2 * s1n - 1 - g)):
        lo = blk * P_BLK
        phi = (2 * (f1 + 2 * s1n * f2) + 1).astype(F32) * (math.pi / (4 * seq))
        c = pltpu.repeat(jnp.cos(phi), reps, axis=1) * (1.0 / seq)
        s = pltpu.repeat(jnp.sin(phi), reps, axis=1) * (1.0 / seq)
        kr = ka[blk * F_HALF:(blk + 1) * F_HALF]
        ki = kd[blk * F_HALF:(blk + 1) * F_HALF]
        g_ref[0, lo:lo + F_HALF] = c * kr - s * ki
        g_ref[0, lo + F_HALF:lo + P_BLK] = c * ki + s * kr


def _filter_spectrum(cr, sr, t, a, d):
    seq = a.shape[0]
    s1n = seq // P_BLK
    return pl.pallas_call(
        functools.partial(_spectrum_kernel, seq=seq),
        out_shape=jax.ShapeDtypeStruct((s1n, 2 * P_BLK, HYENA_WIDTH), F32),
        grid=(s1n,),
        in_specs=[_SMEM, _SMEM, pl.BlockSpec((1, 2 * P_BLK, 2 * P_BLK), lambda g: (g, 0, 0)),
                  _resident(a.shape), _resident(d.shape)],
        out_specs=pl.BlockSpec((1, 2 * P_BLK, HYENA_WIDTH), lambda g: (g, 0, 0)),
        compiler_params=_params(("parallel",)),
        name="hyena_spectrum",
    )(cr, sr, t, a, d)


def _hy_fwd_kernel(cr_ref, sr_ref, t_ref, g_ref, z_ref, y_ref, *, s1n, nb):
    g = pl.program_id(0)
    for bi in range(nb):
        for cols in _COL_HALVES:
            x = _stacked_blocks(cr_ref, sr_ref, g,
                                lambda s: z_ref[bi, s * P_BLK:(s + 1) * P_BLK, cols].astype(F32), s1n)
            res = jnp.dot(t_ref[0], x, preferred_element_type=F32)
            for blk in range(2):
                lo = blk * P_BLK
                zr = res[lo:lo + F_HALF]
                wi = res[lo + F_HALF:lo + P_BLK]
                gr = g_ref[0, lo:lo + F_HALF, cols]
                gi = g_ref[0, lo + F_HALF:lo + P_BLK, cols]
                y_ref[bi, 0, lo:lo + F_HALF, cols] = (gr * zr + gi * wi).astype(BF16)
                y_ref[bi, 0, lo + F_HALF:lo + P_BLK, cols] = (gr * wi - gi * zr).astype(BF16)


def _hy_inv_kernel(cr_ref, sr_ref, tt_ref, y_ref, z_ref, g0_ref, fb_ref, gn_ref, o_ref,
                   acc_ref, zs_ref, g0s_ref, *, s1n, nb):
    g = pl.program_id(1)

    @pl.when(g == 0)
    def _():
        acc_ref[...] = jnp.zeros_like(acc_ref)

    tblk = pl.ds(pl.multiple_of(g * P_BLK, P_BLK), P_BLK)
    zs_ref[:, tblk, :] = z_ref[...]
    g0s_ref[:, tblk, :] = g0_ref[...]
    for bi in range(nb):
        for cols in _COL_HALVES:
            uv = jnp.dot(tt_ref[g], y_ref[bi, 0, :, cols], preferred_element_type=F32)
            u = uv[:P_BLK]
            vn = uv[P_BLK:]
            acc_ref[bi, 0:P_BLK, cols] += u
            for t1 in range(1, s1n):
                rows = slice(t1 * P_BLK, (t1 + 1) * P_BLK)
                acc_ref[bi, rows, cols] += cr_ref[g, t1] * u + sr_ref[g, t1] * vn

    @pl.when(g == s1n - 1)
    def _():
        for bi in range(nb):
            for t1 in range(s1n):
                rows = slice(t1 * P_BLK, (t1 + 1) * P_BLK)
                y = ((acc_ref[bi, rows, :] + zs_ref[bi, rows, :].astype(F32) * fb_ref[...])
                     * g0s_ref[bi, rows, :].astype(F32))
                o_ref[bi, rows, :] = _rms_norm(y, gn_ref[...]).astype(BF16)


_INV_STATE_BUDGET = 28 * 1024 * 1024


def _hyena_conv(zg, g0, cr, sr, t, tt, gspec, fbias, g_hyena):
    bsz, seq, w = zg.shape
    s1n = seq // P_BLK
    nb = 2 if bsz % 2 == 0 else 1
    mat = pl.BlockSpec((1, 2 * P_BLK, 2 * P_BLK), lambda g, b: (g, 0, 0))
    spec = pl.BlockSpec((1, 2 * P_BLK, w), lambda g, b: (g, 0, 0))
    y = pl.pallas_call(
        functools.partial(_hy_fwd_kernel, s1n=s1n, nb=nb),
        out_shape=jax.ShapeDtypeStruct((bsz, s1n, 2 * P_BLK, w), BF16),
        grid=(s1n, bsz // nb),
        in_specs=[_SMEM, _SMEM, mat, spec, pl.BlockSpec((nb, seq, w), lambda g, b: (b, 0, 0))],
        out_specs=pl.BlockSpec((nb, 1, 2 * P_BLK, w), lambda g, b: (b, g, 0, 0)),
        compiler_params=_params(("parallel", "parallel")),
        name="hyena_fwd",
    )(cr, sr, t, gspec, zg)
    state_bytes = seq * w * (4 + 2 + 2 + 2 * 2)
    nb = 2 if bsz % 2 == 0 and 2 * state_bytes <= _INV_STATE_BUDGET else 1
    blk = pl.BlockSpec((nb, P_BLK, w), lambda b, g: (b, g, 0))
    return pl.pallas_call(
        functools.partial(_hy_inv_kernel, s1n=s1n, nb=nb),
        out_shape=jax.ShapeDtypeStruct((bsz, seq, w), BF16),
        grid=(bsz // nb, s1n),
        in_specs=[_SMEM, _SMEM, _resident(tt.shape),
                  pl.BlockSpec((nb, 1, 2 * P_BLK, w), lambda b, g: (b, g, 0, 0)),
                  blk, blk, _resident(fbias.shape), _resident(g_hyena.shape)],
        out_specs=pl.BlockSpec((nb, seq, w), lambda b, g: (b, 0, 0)),
        scratch_shapes=[pltpu.VMEM((nb, seq, w), F32), pltpu.VMEM((nb, seq, w), BF16),
                        pltpu.VMEM((nb, seq, w), BF16)],
        compiler_params=_params(("parallel", "arbitrary")),
        name="hyena_inv",
    )(cr, sr, tt, y, zg, g0, fbias, g_hyena)


def _merge_kernel(ma_ref, mh_ref, x_ref, wa_ref, wh_ref, g_ref, b_ref, o_ref):
    y = (jnp.dot(ma_ref[0], wa_ref[...], preferred_element_type=F32)
         + jnp.dot(mh_ref[0], wh_ref[...], preferred_element_type=F32))
    o_ref[0] = _layer_norm(ALPHA * x_ref[0] + y, g_ref[...], b_ref[...])


def _merge(ma, mh, x, w_out, g, b, tl):
    bsz, seq, d = x.shape
    wa, wh = w_out[:ATTN_WIDTH], w_out[ATTN_WIDTH:]
    half = pl.BlockSpec((1, tl, ATTN_WIDTH), lambda bb, i: (bb, i, 0))
    full = pl.BlockSpec((1, tl, d), lambda bb, i: (bb, i, 0))
    return pl.pallas_call(
        _merge_kernel,
        out_shape=jax.ShapeDtypeStruct(x.shape, F32),
        grid=(bsz, seq // tl),
        in_specs=[half, half, full, _resident(wa.shape), _resident(wh.shape),
                  _resident(g.shape), _resident(b.shape)],
        out_specs=full,
        compiler_params=_params(("parallel", "parallel")),
        name="merge_ln1",
    )(ma, mh, x, wa, wh, g, b)


FF_CHUNK = 256
N_FF_CHUNKS = D_FF // FF_CHUNK


def _ffn_kernel(xp_ref, x_ref, xn_ref, wi_ref, cw_ref, cb_ref, wo_ref, g_ref, b_ref,
                o_ref, xs_ref, hid_ref, os_ref, *, tl):
    _fill_halo_slabs(xs_ref, xp_ref, x_ref, xn_ref, tl)
    xh = _load_interleaved(xs_ref).astype(BF16)
    for j in range(N_FF_CHUNKS):
        conv = []
        for part in range(2):
            cols = slice(part * D_FF + j * FF_CHUNK, part * D_FF + (j + 1) * FF_CHUNK)
            u = jnp.dot(xh, wi_ref[:, cols], preferred_element_type=F32)
            conv.append(_dwconv3_interleaved(u, cw_ref[:, cols], cb_ref[:, cols]))
        gate = conv[1]
        gelu = 0.5 * gate * (1.0 + lax.erf(gate * (2.0 ** -0.5)))
        hid_ref[:, j * FF_CHUNK:(j + 1) * FF_CHUNK] = (conv[0] * gelu).astype(BF16)
    y = jnp.dot(hid_ref[...], wo_ref[...], preferred_element_type=F32)
    out = _layer_norm(ALPHA * _load_interleaved(xs_ref) + y, g_ref[...], b_ref[...])
    _store_natural(os_ref, 0, out)
    for k in range(os_ref.shape[0]):
        o_ref[0, :, k * LANES:(k + 1) * LANES] = os_ref[k, HALO:HALO + tl, :]


def _ffn(x1, w_in, conv_w, conv_b, w_out, g, b, tl):
    bsz, seq, d = x1.shape
    cb = conv_b.reshape(1, -1)
    return pl.pallas_call(
        functools.partial(_ffn_kernel, tl=tl),
        out_shape=jax.ShapeDtypeStruct(x1.shape, F32),
        grid=(bsz, seq // tl),
        in_specs=_halo_specs(tl, seq, d) + [_resident(w_in.shape), _resident(conv_w.shape), _resident(cb.shape),
                                            _resident(w_out.shape), _resident(g.shape), _resident(b.shape)],
        out_specs=pl.BlockSpec((1, tl, d), lambda bb, i: (bb, i, 0)),
        scratch_shapes=[pltpu.VMEM((d // LANES, tl + 2 * HALO, LANES), F32),
                        pltpu.VMEM((tl + 2 * HALO, D_FF), BF16),
                        pltpu.VMEM((d // LANES, tl + 2 * HALO, LANES), F32)],
        compiler_params=_params(("parallel", "parallel")),
        name="conv_ffn",
    )(x1, x1, x1, w_in, conv_w, cb, w_out, g, b)


def _encoder_layer(x, p, tl=1024, rb=8):
    seq = x.shape[1]
    q, k, v, zg, g0 = _in_proj(x, p["w_in"], p["short_w"], p["short_b"], tl)
    ma = _attention(q, k, v, p["bias"], p["g_attn"], rb)
    a, d = _filter_taps(seq, p["filt_w1"], p["filt_b1"], p["filt_freq"], p["filt_w_inner"],
                        p["filt_b_inner"], p["filt_w3"])
    cr, sr = _pair_coefficients(seq)
    t, tt = _dft_matrices(seq)
    gspec = _filter_spectrum(cr, sr, t, a, d)
    mh = _hyena_conv(zg, g0, cr, sr, t, tt, gspec, p["filt_bias"], p["g_hyena"])
    x1 = _merge(ma, mh, x, p["w_out"], p["ln1_g"], p["ln1_b"], tl)
    return _ffn(x1, p["ffn_w_in"], p["ffn_conv_w"], p["ffn_conv_b"], p["ffn_w_out"],
                p["ln2_g"], p["ln2_b"], tl)


def kernel(x_prompt, x_sample, w_in, short_w, short_b, rpb, filt_w1, filt_b1, filt_freq, filt_w_inner,
           filt_b_inner, filt_w3, filt_bias, g_attn, g_hyena, w_out, ln1_g, ln1_b, ffn_w_in, ffn_conv_w,
           ffn_conv_b, ffn_w_out, ln2_g, ln2_b):
    assert w_in.shape[0] == DEPTH == 1
    row = lambda a: a[0].reshape(1, -1)
    p = dict(
        w_in=w_in[0].astype(BF16), short_w=short_w[0], short_b=short_b[0],
        bias=_attention_bias(rpb[0]), g_attn=row(g_attn),
        filt_w1=filt_w1[0], filt_b1=filt_b1[0], filt_freq=filt_freq[0], filt_w_inner=filt_w_inner[0],
        filt_b_inner=filt_b_inner[0], filt_w3=filt_w3[0], filt_bias=filt_bias[0], g_hyena=row(g_hyena),
        w_out=w_out[0].astype(BF16), ln1_g=row(ln1_g), ln1_b=row(ln1_b),
        ffn_w_in=ffn_w_in[0].astype(BF16), ffn_conv_w=ffn_conv_w[0], ffn_conv_b=ffn_conv_b[0],
        ffn_w_out=ffn_w_out[0].astype(BF16), ln2_g=row(ln2_g), ln2_b=row(ln2_b),
    )
    return (_encoder_layer(x_prompt, p), _encoder_layer(x_sample, p))
```

```python
import functools
import math

import numpy as np
import jax
import jax.numpy as jnp
from jax import lax
from jax.experimental import pallas as pl
from jax.experimental.pallas import tpu as pltpu

F32 = jnp.float32
BF16 = jnp.bfloat16

D_MODEL = 1024
GRID_W = 64
ATTN_WIDTH = 512
HYENA_WIDTH = 512
HEAD_DIM = 64
N_HEADS = ATTN_WIDTH // HEAD_DIM
NA_ROWS = 8
NA_COLS = 16
FILTER_EMB = 33
FILTER_BANDS = (FILTER_EMB - 1) // 2
FILTER_HIDDEN = 64
FILTER_INNER = 2
MAX_DECAY = math.log(1e-2) / 0.3
MIN_DECAY = math.log(1e-2) / 1.5
D_FF = 2816
DEPTH = 1
ALPHA = (2 * DEPTH) ** 0.25
LN_EPS = 1e-5
RMS_EPS = 1e-6
LOG2E = math.log2(math.e)

SUBLANES = 8
LANES = 128
HALO = SUBLANES
VMEM_LIMIT = 56 * 1024 * 1024

HIGHEST = lax.Precision.HIGHEST


def _params(sem):
    return pltpu.CompilerParams(dimension_semantics=sem, vmem_limit_bytes=VMEM_LIMIT)


def _resident(shape):
    nd = len(shape)
    return pl.BlockSpec(shape, lambda *_: (0,) * nd, pipeline_mode=pl.Buffered(1))


def _layer_norm(y, g, b):
    mu = jnp.mean(y, axis=-1, keepdims=True)
    yc = y - mu
    var = jnp.mean(yc * yc, axis=-1, keepdims=True)
    return yc * lax.rsqrt(var + LN_EPS) * g + b


def _rms_norm(y, g):
    ms = jnp.mean(y * y, axis=-1, keepdims=True)
    return y * lax.rsqrt(ms + RMS_EPS) * g


def _fill_halo_slabs(xs_ref, xp_ref, x_ref, xn_ref, tl):
    i = pl.program_id(1)
    last = pl.num_programs(1) - 1
    prev = jnp.where(i > 0, xp_ref[0], 0.0)
    nxt = jnp.where(i < last, xn_ref[0], 0.0)
    for k in range(xs_ref.shape[0]):
        sl = slice(k * LANES, (k + 1) * LANES)
        xs_ref[k, 0:HALO, :] = prev[:, sl]
        xs_ref[k, HALO:HALO + tl, :] = x_ref[0, :, sl]
        xs_ref[k, HALO + tl:HALO + tl + HALO, :] = nxt[:, sl]


def _load_interleaved(xs_ref):
    nslab, rows, _ = xs_ref.shape
    nv = rows // SUBLANES
    assert rows % SUBLANES == 0 and nv % SUBLANES != 0
    return jnp.concatenate(
        [jnp.concatenate([xs_ref[k, pl.ds(j, SUBLANES, stride=nv), :] for j in range(nv)], axis=0)
         for k in range(nslab)], axis=1)


def _store_natural(os_ref, first, val):
    nv = os_ref.shape[1] // SUBLANES
    for k in range(val.shape[1] // LANES):
        for j in range(nv):
            os_ref[first + k, pl.ds(j, SUBLANES, stride=nv), :] = val[j * SUBLANES:(j + 1) * SUBLANES,
                                                                      k * LANES:(k + 1) * LANES]


def _dwconv3_interleaved(u, w, b):
    head = pltpu.roll(u[-SUBLANES:], 1, axis=0)
    tail = pltpu.roll(u[:SUBLANES], SUBLANES - 1, axis=0)
    up = jnp.concatenate([head, u[:-SUBLANES]], axis=0)
    un = jnp.concatenate([u[SUBLANES:], tail], axis=0)
    return up * w[0:1] + u * w[1:2] + un * w[2:3] + b


def _halo_specs(tl, seq, d):
    nb = tl // HALO
    last = seq // HALO - 1
    return [
        pl.BlockSpec((1, HALO, d), lambda b, i: (b, jnp.maximum(i * nb - 1, 0), 0)),
        pl.BlockSpec((1, tl, d), lambda b, i: (b, i, 0)),
        pl.BlockSpec((1, HALO, d), lambda b, i: (b, jnp.minimum((i + 1) * nb, last), 0)),
    ]


def _in_proj_kernel(xp_ref, x_ref, xn_ref, w_ref, sw_ref, sb_ref,
                    q_ref, k_ref, v_ref, zg_ref, g0_ref, xs_ref, os_ref, *, tl):
    xq = x_ref[0].astype(BF16)
    proj = lambda n: jnp.dot(xq, w_ref[:, n * ATTN_WIDTH:(n + 1) * ATTN_WIDTH], preferred_element_type=F32)
    q_ref[0] = (proj(0) * (HEAD_DIM ** -0.5 * LOG2E)).astype(BF16)
    k_ref[0] = proj(1).astype(BF16)
    v = proj(2).astype(BF16)
    ones = jnp.ones((tl, LANES), BF16)
    v_ref[0] = jnp.concatenate([piece for p in range(N_HEADS // 2)
                                for piece in (v[:, p * LANES:(p + 1) * LANES], ones)], axis=1)
    _fill_halo_slabs(xs_ref, xp_ref, x_ref, xn_ref, tl)
    xh = _load_interleaved(xs_ref).astype(BF16)
    conv = []
    for n in range(3):
        lo = 3 * ATTN_WIDTH + n * HYENA_WIDTH
        u = jnp.dot(xh, w_ref[:, lo:lo + HYENA_WIDTH], preferred_element_type=F32)
        conv.append(_dwconv3_interleaved(u, sw_ref[n], sb_ref[n]))
    nslab = HYENA_WIDTH // LANES
    for first, ref, val in ((0, g0_ref, conv[0]), (nslab, zg_ref, conv[2] * conv[1])):
        _store_natural(os_ref, first, val)
        for k in range(nslab):
            ref[0, :, k * LANES:(k + 1) * LANES] = os_ref[first + k, HALO:HALO + tl, :].astype(BF16)


def _in_proj(x, w_in, short_w, short_b, tl):
    bsz, seq, d = x.shape
    sw = short_w.reshape(3, 3, HYENA_WIDTH).transpose(1, 0, 2)
    sb = short_b.reshape(3, 1, HYENA_WIDTH)
    out = jax.ShapeDtypeStruct((bsz, seq, ATTN_WIDTH), BF16)
    ospec = pl.BlockSpec((1, tl, ATTN_WIDTH), lambda b, i: (b, i, 0))
    vout = jax.ShapeDtypeStruct((bsz, seq, 2 * ATTN_WIDTH), BF16)
    vspec = pl.BlockSpec((1, tl, 2 * ATTN_WIDTH), lambda b, i: (b, i, 0))
    return pl.pallas_call(
        functools.partial(_in_proj_kernel, tl=tl),
        out_shape=(out, out, vout, out, out),
        grid=(bsz, seq // tl),
        in_specs=_halo_specs(tl, seq, d) + [_resident(w_in.shape), _resident(sw.shape), _resident(sb.shape)],
        out_specs=(ospec, ospec, vspec, ospec, ospec),
        scratch_shapes=[pltpu.VMEM((d // LANES, tl + 2 * HALO, LANES), F32),
                        pltpu.VMEM((2 * HYENA_WIDTH // LANES, tl + 2 * HALO, LANES), F32)],
        compiler_params=_params(("parallel", "parallel")),
        name="in_proj",
    )(x, x, x, w_in, sw, sb)


N_PAIRS = N_HEADS // 2
KEY_WIN = NA_ROWS * GRID_W
N_DR = 2 * NA_ROWS - 1
N_DC = 2 * NA_COLS - 1
SOFTMAX_PARTS = 4


def _bias_kernel(rpb_ref, ea_ref, eb_ref, o_ref, t_ref):
    row = lax.broadcasted_iota(jnp.int32, (GRID_W, LANES), 0)
    lane = lax.broadcasted_iota(jnp.int32, (GRID_W, LANES), 1)
    first = lane < GRID_W
    qcol = lane & (GRID_W - 1)
    col_start = jnp.clip(qcol - NA_COLS // 2, 0, GRID_W - NA_COLS)
    valid = (row >= col_start) & (row < col_start + NA_COLS)
    wa = jnp.dot(rpb_ref[0, 0], ea_ref[...], precision=HIGHEST, preferred_element_type=F32)
    wb = jnp.dot(rpb_ref[0, 1], eb_ref[...], precision=HIGHEST, preferred_element_type=F32)

    def rotate_rows(x):
        for t in range(GRID_W.bit_length() - 1):
            x = jnp.where(((row >> t) & 1) == 1, pltpu.roll(x, 1 << t, axis=1), x)
        return x

    for dr in range(N_DR):
        ta = rotate_rows(jnp.broadcast_to(wa[dr:dr + 1], (GRID_W, LANES)))
        tb = rotate_rows(jnp.broadcast_to(wb[dr:dr + 1], (GRID_W, LANES)))
        t_ref[dr] = jnp.where(valid, jnp.where(first, ta, tb) * LOG2E, -jnp.inf)
    for cls in range(NA_ROWS):
        for i in range(NA_ROWS):
            o_ref[0, cls, i * GRID_W:(i + 1) * GRID_W, :] = t_ref[i - cls + NA_ROWS - 1]


def _toeplitz_selectors():
    m = np.arange(LANES)
    dc = np.where(m < GRID_W, (NA_COLS - 1) - np.minimum(m, NA_COLS - 1),
                  np.minimum(LANES - m, NA_COLS - 1) + (NA_COLS - 1))
    ea = np.zeros((LANES, LANES), np.float32)
    ea[dc, m] = 1.0
    ea[:, GRID_W] = 0.0
    eb = np.roll(ea, GRID_W, axis=1)
    return ea, eb


def _attention_bias(rpb):
    ea, eb = _toeplitz_selectors()
    rp = jnp.pad(rpb.astype(F32), ((0, 0), (0, 16 - N_DR), (0, LANES - N_DC))).reshape(N_PAIRS, 2, 16, LANES)
    return pl.pallas_call(
        _bias_kernel,
        out_shape=jax.ShapeDtypeStruct((N_PAIRS, NA_ROWS, KEY_WIN, LANES), F32),
        grid=(N_PAIRS,),
        in_specs=[pl.BlockSpec((1, 2, 16, LANES), lambda p: (p, 0, 0, 0)),
                  _resident(ea.shape), _resident(eb.shape)],
        out_specs=pl.BlockSpec((1, NA_ROWS, KEY_WIN, LANES), lambda p: (p, 0, 0, 0)),
        scratch_shapes=[pltpu.VMEM((N_DR, GRID_W, LANES), F32)],
        compiler_params=_params(("parallel",)),
        name="attn_bias",
    )(rp, jnp.asarray(ea), jnp.asarray(eb))


def _attn_kernel(q_ref, k_ref, vx_ref, bias_ref, g_ref, o_ref, *, rows, rb, unroll):
    jb = pl.program_id(1)
    lane = lax.broadcasted_iota(jnp.int32, (GRID_W, LANES), 1)
    first = lane < HEAD_DIM
    keep = (first.astype(F32).astype(BF16), (~first).astype(F32).astype(BF16))
    nt = (((1,), (1,)), ((), ()))
    tn = (((0,), (0,)), ((), ()))

    def score_stage(j):
        r = jb * rb + j
        start = jnp.clip(r - NA_ROWS // 2, 0, rows - NA_ROWS)
        qoff = pl.multiple_of(j * GRID_W, GRID_W)
        koff = pl.multiple_of(start * GRID_W, GRID_W)
        scores = []
        for p in range(N_PAIRS):
            sl = slice(p * LANES, (p + 1) * LANES)
            qp = q_ref[0, pl.ds(qoff, GRID_W), sl]
            qblk = jnp.concatenate([qp * keep[0], qp * keep[1]], axis=0)
            kp = k_ref[0, pl.ds(koff, KEY_WIN), sl]
            scores.append(lax.dot_general(kp, qblk, nt, preferred_element_type=F32))
        return r - start, qoff, koff, scores

    def output_stage(cls, qoff, koff, scores):
        probs = []
        step = KEY_WIN // SOFTMAX_PARTS
        for p in range(N_PAIRS):
            parts, maxes = [], []
            for lo in range(0, KEY_WIN, step):
                s = scores[p][lo:lo + step] + bias_ref[p, cls, lo:lo + step]
                m_g = jnp.max(s, axis=0, keepdims=True)
                parts.append(jnp.exp2(s - m_g))
                maxes.append(m_g)
            m = functools.reduce(jnp.maximum, maxes)
            probs.append(jnp.concatenate([e * jnp.exp2(m_g - m) for e, m_g in zip(parts, maxes)],
                                         axis=0).astype(BF16))
        outs = []
        for p in range(N_PAIRS):
            vx = vx_ref[0, pl.ds(koff, KEY_WIN), 2 * p * LANES:2 * (p + 1) * LANES]
            o = lax.dot_general(probs[p], vx, tn, preferred_element_type=F32)
            oa = o[:GRID_W, :LANES] / o[:GRID_W, LANES:]
            ob = o[GRID_W:, :LANES] / o[GRID_W:, LANES:]
            outs.append(jnp.where(first, oa, ob))
        o_ref[0, pl.ds(qoff, GRID_W), :] = _rms_norm(jnp.concatenate(outs, axis=1), g_ref[...]).astype(BF16)

    def body(jj, carry):
        state = score_stage(jj * unroll)
        for u in range(unroll):
            ahead = score_stage(jj * unroll + u + 1) if u + 1 < unroll else None
            output_stage(*state)
            state = ahead
        return carry

    lax.fori_loop(0, rb // unroll, body, 0)


def _attention(q, k, vx, bias, g_attn, rb, unroll=8):
    bsz, seq, w = q.shape
    rows = seq // GRID_W
    tq = rb * GRID_W
    return pl.pallas_call(
        functools.partial(_attn_kernel, rows=rows, rb=rb, unroll=unroll),
        out_shape=jax.ShapeDtypeStruct((bsz, seq, w), BF16),
        grid=(bsz, rows // rb),
        in_specs=[
            pl.BlockSpec((1, tq, w), lambda b, i: (b, i, 0)),
            pl.BlockSpec((1, seq, w), lambda b, i: (b, 0, 0)),
            pl.BlockSpec((1, seq, 2 * w), lambda b, i: (b, 0, 0)),
            _resident(bias.shape),
            _resident(g_attn.shape),
        ],
        out_specs=pl.BlockSpec((1, tq, w), lambda b, i: (b, i, 0)),
        compiler_params=_params(("parallel", "arbitrary")),
        name="attention",
    )(q, k, vx, bias, g_attn)


def _filter_kernel(w1t_ref, w1c_ref, w1s_ref, b1_ref, fq_ref, wi_ref, bi_ref, w3_ref,
                   a_ref, d_ref, cb_ref, sb_ref, h_ref, *, seq, tl):
    i = pl.program_id(0)
    rows = tl + 2 * SUBLANES
    half = rows // 2
    band = lax.broadcasted_iota(jnp.int32, (rows, LANES), 1)
    freqs = jnp.where(band < FILTER_BANDS,
                      1e-4 + band.astype(F32) * ((FILTER_BANDS - 1 - 1e-4) / (FILTER_BANDS - 1)), 0.0)
    rad = freqs * (2.0 * math.pi / seq)
    local = lax.broadcasted_iota(jnp.int32, (rows, LANES), 0)

    @pl.when(i == 0)
    def _():
        cb_ref[...] = jnp.cos(local.astype(F32) * rad)
        sb_ref[...] = jnp.sin(local.astype(F32) * rad)

    base = (i * tl).astype(F32) * rad[0:SUBLANES]
    ca = jnp.cos(base)[0:1]
    sa = jnp.sin(base)[0:1]
    cos_ang = ca * cb_ref[...] - sa * sb_ref[...]
    sin_ang = sa * cb_ref[...] + ca * sb_ref[...]
    t = (local + i * tl).astype(F32) * (1.0 / (seq - 1))
    fq = fq_ref[...]
    dot = functools.partial(jnp.dot, precision=HIGHEST, preferred_element_type=F32)
    side = lambda x: jnp.concatenate([x[:half], x[half:]], axis=1)
    left = lax.broadcasted_iota(jnp.int32, (half, LANES), 1) < FILTER_HIDDEN
    t_packed = jnp.where(left, t[:half], t[half:])
    pre = (t_packed * w1t_ref[...] + dot(side(cos_ang), w1c_ref[...]) - dot(side(sin_ang), w1s_ref[...])
           + b1_ref[...])
    h = jnp.sin(fq * pre)
    for n in range(FILTER_INNER):
        h = jnp.sin(fq * (dot(h, wi_ref[n]) + bi_ref[n]))
    chan = lax.broadcasted_iota(jnp.int32, (half, HYENA_WIDTH), 1).astype(F32)
    deltas = jnp.abs(MIN_DECAY + chan * ((MAX_DECAY - MIN_DECAY) / (HYENA_WIDTH - 1)))
    for part in range(2):
        rws = slice(part * half, (part + 1) * half)
        pos = lax.broadcasted_iota(jnp.int32, (half, HYENA_WIDTH), 0) + (i * tl + part * half)
        decay = jnp.exp(-(pos.astype(F32) * (1.0 / (seq - 1))) * deltas)
        for n in range(2):
            cols = slice(n * HYENA_WIDTH, (n + 1) * HYENA_WIDTH)
            h_ref[rws, cols] = dot(h, w3_ref[part, :, cols]) * decay
    h_fwd = h_ref[0:tl, 0:HYENA_WIDTH]
    lag = lax.broadcasted_iota(jnp.int32, (tl, HYENA_WIDTH), 0) + (i * tl + 1)
    h_bwd = jnp.where(lag < seq, h_ref[pl.ds(1, tl), HYENA_WIDTH:2 * HYENA_WIDTH], 0.0)
    a_ref[...] = (h_fwd + h_bwd).astype(BF16)
    d_ref[...] = (h_bwd - h_fwd).astype(BF16)


def _filter_taps(seq, w1, b1, freq, w_inner, b_inner, w3, tl=512):
    pad = LANES - FILTER_BANDS
    twice = lambda v: jnp.tile(v.reshape(1, -1), (1, 2))
    blockdiag = lambda w: jnp.kron(jnp.eye(2, dtype=w.dtype), w)
    w1c = blockdiag(jnp.pad(w1[1:1 + FILTER_BANDS], ((0, pad), (0, 0))))
    w1s = blockdiag(jnp.pad(w1[1 + FILTER_BANDS:], ((0, pad), (0, 0))))
    zero = jnp.zeros_like(w3)
    args = (twice(w1[0]), w1c, w1s, twice(b1), twice(freq),
            jnp.stack([blockdiag(w_inner[n]) for n in range(FILTER_INNER)]),
            jnp.stack([twice(b_inner[n]) for n in range(FILTER_INNER)]),
            jnp.stack([jnp.concatenate([w3, zero]), jnp.concatenate([zero, w3])]))
    out = jax.ShapeDtypeStruct((seq, HYENA_WIDTH), BF16)
    ospec = pl.BlockSpec((tl, HYENA_WIDTH), lambda i: (i, 0))
    return pl.pallas_call(
        functools.partial(_filter_kernel, seq=seq, tl=tl),
        out_shape=(out, out),
        grid=(seq // tl,),
        in_specs=[_resident(a.shape) for a in args],
        out_specs=(ospec, ospec),
        scratch_shapes=[pltpu.VMEM((tl + 2 * SUBLANES, LANES), F32),
                        pltpu.VMEM((tl + 2 * SUBLANES, LANES), F32),
                        pltpu.VMEM((tl + 2 * SUBLANES, 2 * HYENA_WIDTH), F32)],
        compiler_params=_params(("arbitrary",)),
        name="hyena_filter",
    )(*args)


P_BLK = 512
F_HALF = P_BLK // 2


def _pair_coefficients(seq):
    s1n = seq // P_BLK
    g = np.arange(s1n)[:, None]
    s = np.arange(s1n)[None, :]
    ang = np.pi * (2 * g + 1) * s / (2 * s1n)
    return jnp.asarray(np.cos(ang), F32), jnp.asarray(np.sin(ang), F32)


def _dft_kernel(t_ref, tt_ref, cb_ref, sb_ref, *, seq):
    g = pl.program_id(0)
    s1n = seq // P_BLK

    @pl.when(g == 0)
    def _():
        f2 = lax.broadcasted_iota(jnp.int32, (F_HALF, P_BLK), 0)
        s_odd = 2 * lax.broadcasted_iota(jnp.int32, (F_HALF, P_BLK), 1) + 1
        beta = ((f2 * s_odd) & (2 * P_BLK - 1)).astype(F32) * (math.pi / P_BLK)
        cb_ref[...] = jnp.cos(beta)
        sb_ref[...] = jnp.sin(beta)

    cb = cb_ref[...]
    sb = sb_ref[...]
    s_odd = 2 * lax.broadcasted_iota(jnp.int32, (SUBLANES, P_BLK), 1) + 1

    def block(f1):
        alpha = ((s_odd * (2 * f1 + 1)) & (8 * seq - 1)).astype(F32) * (math.pi / (4 * seq))
        ca = jnp.cos(alpha)[0:1]
        sa = jnp.sin(alpha)[0:1]
        return ca * cb - sa * sb, sa * cb + ca * sb

    mra, mia = block(g)
    mrb, mib = block(2 * s1n - 1 - g)
    for r, (left, right) in enumerate(((mra, -mia), (mia, mra), (mrb, mib), (mib, -mrb))):
        rows = slice(r * F_HALF, (r + 1) * F_HALF)
        t_ref[0, rows, :P_BLK] = left.astype(BF16)
        t_ref[0, rows, P_BLK:] = right.astype(BF16)
        tt_ref[0, :P_BLK, rows] = left.T.astype(BF16)
        tt_ref[0, P_BLK:, rows] = right.T.astype(BF16)


def _dft_matrices(seq):
    s1n = seq // P_BLK
    mat = jax.ShapeDtypeStruct((s1n, 2 * P_BLK, 2 * P_BLK), BF16)
    spec = pl.BlockSpec((1, 2 * P_BLK, 2 * P_BLK), lambda g: (g, 0, 0))
    return pl.pallas_call(
        functools.partial(_dft_kernel, seq=seq),
        out_shape=(mat, mat),
        grid=(s1n,),
        out_specs=(spec, spec),
        scratch_shapes=[pltpu.VMEM((F_HALF, P_BLK), F32), pltpu.VMEM((F_HALF, P_BLK), F32)],
        compiler_params=_params(("arbitrary",)),
        name="dft_matrices",
    )()


_SMEM = pl.BlockSpec(memory_space=pltpu.SMEM)
_COL_HALVES = (slice(0, HYENA_WIDTH // 2), slice(HYENA_WIDTH // 2, HYENA_WIDTH))


ROW_CHUNK = 64


def _stacked_blocks(cr_ref, sr_ref, g, block, s1n):
    re, im = [], []
    for r in range(0, P_BLK, ROW_CHUNK):
        ar = block(0, r)
        bi = None
        for s in range(1, s1n):
            b = block(s, r)
            ar = ar + cr_ref[g, s] * b
            bi = sr_ref[g, s] * b if bi is None else bi + sr_ref[g, s] * b
        re.append(ar)
        im.append(bi)
    return jnp.concatenate(re + im, axis=0).astype(BF16)


def _spectrum_kernel(cr_ref, sr_ref, t_ref, a_ref, d_ref, g_ref, *, seq):
    g = pl.program_id(0)
    s1n = seq // P_BLK
    time_block = lambda ref: (lambda s, r: ref[s * P_BLK + r:s * P_BLK + r + ROW_CHUNK, :].astype(F32))
    t_re = jnp.concatenate([t_ref[0, 0:F_HALF], t_ref[0, P_BLK:P_BLK + F_HALF]], axis=0)
    t_im = jnp.concatenate([t_ref[0, F_HALF:P_BLK], t_ref[0, P_BLK + F_HALF:2 * P_BLK]], axis=0)
    ka = jnp.dot(t_re, _stacked_blocks(cr_ref, sr_ref, g, time_block(a_ref), s1n),
                 preferred_element_type=F32)
    kd = jnp.dot(t_im, _stacked_blocks(cr_ref, sr_ref, g, time_block(d_ref), s1n),
                 preferred_element_type=F32)
    f2 = lax.broadcasted_iota(jnp.int32, (F_HALF, LANES), 0)
    reps = HYENA_WIDTH // LANES
    for blk, f1 in ((0, g), (1, 2 * s1n - 1 - g)):
        lo = blk * P_BLK
        phi = (2 * (f1 + 2 * s1n * f2) + 1).astype(F32) * (math.pi / (4 * seq))
        c = pltpu.repeat(jnp.cos(phi), reps, axis=1) * (1.0 / seq)
        s = pltpu.repeat(jnp.sin(phi), reps, axis=1) * (1.0 / seq)
        for r in range(0, F_HALF, ROW_CHUNK):
            rows = slice(blk * F_HALF + r, blk * F_HALF + r + ROW_CHUNK)
            cc, ss = c[r:r + ROW_CHUNK], s[r:r + ROW_CHUNK]
            g_ref[0, lo + r:lo + r + ROW_CHUNK] = cc * ka[rows] - ss * kd[rows]
            g_ref[0, lo + F_HALF + r:lo + F_HALF + r + ROW_CHUNK] = cc * kd[rows] + ss * ka[rows]


def _filter_spectrum(cr, sr, t, a, d):
    seq = a.shape[0]
    s1n = seq // P_BLK
    return pl.pallas_call(
        functools.partial(_spectrum_kernel, seq=seq),
        out_shape=jax.ShapeDtypeStruct((s1n, 2 * P_BLK, HYENA_WIDTH), F32),
        grid=(s1n,),
        in_specs=[_SMEM, _SMEM, pl.BlockSpec((1, 2 * P_BLK, 2 * P_BLK), lambda g: (g, 0, 0)),
                  _resident(a.shape), _resident(d.shape)],
        out_specs=pl.BlockSpec((1, 2 * P_BLK, HYENA_WIDTH), lambda g: (g, 0, 0)),
        compiler_params=_params(("parallel",)),
        name="hyena_spectrum",
    )(cr, sr, t, a, d)


def _hy_fwd_kernel(cr_ref, sr_ref, t_ref, g_ref, z_ref, y_ref, *, s1n, nb):
    g = pl.program_id(0)
    for bi in range(nb):
        for cols in _COL_HALVES:
            x = _stacked_blocks(
                cr_ref, sr_ref, g,
                lambda s, r: z_ref[bi, s * P_BLK + r:s * P_BLK + r + ROW_CHUNK, cols].astype(F32), s1n)
            res = jnp.dot(t_ref[0], x, preferred_element_type=F32)
            for lo in range(0, 2 * P_BLK, P_BLK):
                for r in range(lo, lo + F_HALF, ROW_CHUNK):
                    re_rows = slice(r, r + ROW_CHUNK)
                    im_rows = slice(r + F_HALF, r + F_HALF + ROW_CHUNK)
                    zr, wi = res[re_rows], res[im_rows]
                    gr, gi = g_ref[0, re_rows, cols], g_ref[0, im_rows, cols]
                    y_ref[bi, 0, re_rows, cols] = (gr * zr + gi * wi).astype(BF16)
                    y_ref[bi, 0, im_rows, cols] = (gr * wi - gi * zr).astype(BF16)


def _hy_inv_kernel(cr_ref, sr_ref, tt_ref, y_ref, z_ref, g0_ref, fb_ref, gn_ref, o_ref,
                   acc_ref, zs_ref, g0s_ref, *, s1n, nb):
    g = pl.program_id(1)

    @pl.when(g == 0)
    def _():
        acc_ref[...] = jnp.zeros_like(acc_ref)

    tblk = pl.ds(pl.multiple_of(g * P_BLK, P_BLK), P_BLK)
    zs_ref[:, tblk, :] = z_ref[...]
    g0s_ref[:, tblk, :] = g0_ref[...]
    for bi in range(nb):
        for cols in _COL_HALVES:
            uv = jnp.dot(tt_ref[g], y_ref[bi, 0, :, cols], preferred_element_type=F32)
            for r in range(0, P_BLK, ROW_CHUNK):
                u = uv[r:r + ROW_CHUNK]
                vn = uv[P_BLK + r:P_BLK + r + ROW_CHUNK]
                acc_ref[bi, r:r + ROW_CHUNK, cols] += u
                for t1 in range(1, s1n):
                    rows = slice(t1 * P_BLK + r, t1 * P_BLK + r + ROW_CHUNK)
                    acc_ref[bi, rows, cols] += cr_ref[g, t1] * u + sr_ref[g, t1] * vn

    @pl.when(g == s1n - 1)
    def _():
        for bi in range(nb):
            for r in range(0, s1n * P_BLK, ROW_CHUNK):
                rows = slice(r, r + ROW_CHUNK)
                y = ((acc_ref[bi, rows, :] + zs_ref[bi, rows, :].astype(F32) * fb_ref[...])
                     * g0s_ref[bi, rows, :].astype(F32))
                o_ref[bi, rows, :] = _rms_norm(y, gn_ref[...]).astype(BF16)


_INV_STATE_BUDGET = 28 * 1024 * 1024


def _hyena_conv(zg, g0, cr, sr, t, tt, gspec, fbias, g_hyena):
    bsz, seq, w = zg.shape
    s1n = seq // P_BLK
    nb = 2 if bsz % 2 == 0 else 1
    mat = pl.BlockSpec((1, 2 * P_BLK, 2 * P_BLK), lambda g, b: (g, 0, 0))
    spec = pl.BlockSpec((1, 2 * P_BLK, w), lambda g, b: (g, 0, 0))
    y = pl.pallas_call(
        functools.partial(_hy_fwd_kernel, s1n=s1n, nb=nb),
        out_shape=jax.ShapeDtypeStruct((bsz, s1n, 2 * P_BLK, w), BF16),
        grid=(s1n, bsz // nb),
        in_specs=[_SMEM, _SMEM, mat, spec, pl.BlockSpec((nb, seq, w), lambda g, b: (b, 0, 0))],
        out_specs=pl.BlockSpec((nb, 1, 2 * P_BLK, w), lambda g, b: (b, g, 0, 0)),
        compiler_params=_params(("parallel", "parallel")),
        name="hyena_fwd",
    )(cr, sr, t, gspec, zg)
    state_bytes = seq * w * (4 + 2 + 2 + 2 * 2)
    nb = 2 if bsz % 2 == 0 and 2 * state_bytes <= _INV_STATE_BUDGET else 1
    blk = pl.BlockSpec((nb, P_BLK, w), lambda b, g: (b, g, 0))
    return pl.pallas_call(
        functools.partial(_hy_inv_kernel, s1n=s1n, nb=nb),
        out_shape=jax.ShapeDtypeStruct((bsz, seq, w), BF16),
        grid=(bsz // nb, s1n),
        in_specs=[_SMEM, _SMEM, _resident(tt.shape),
                  pl.BlockSpec((nb, 1, 2 * P_BLK, w), lambda b, g: (b, g, 0, 0)),
                  blk, blk, _resident(fbias.shape), _resident(g_hyena.shape)],
        out_specs=pl.BlockSpec((nb, seq, w), lambda b, g: (b, 0, 0)),
        scratch_shapes=[pltpu.VMEM((nb, seq, w), F32), pltpu.VMEM((nb, seq, w), BF16),
                        pltpu.VMEM((nb, seq, w), BF16)],
        compiler_params=_params(("parallel", "arbitrary")),
        name="hyena_inv",
    )(cr, sr, tt, y, zg, g0, fbias, g_hyena)


MERGE_ROW_CHUNKS = 4


def _merge_kernel(ma_ref, mh_ref, x_ref, wa_ref, wh_ref, g_ref, b_ref, o_ref):
    tl = x_ref.shape[1]
    chunk = tl // MERGE_ROW_CHUNKS
    for lo in range(0, tl, chunk):
        rows = slice(lo, lo + chunk)
        y = (jnp.dot(ma_ref[0, rows], wa_ref[...], preferred_element_type=F32)
             + jnp.dot(mh_ref[0, rows], wh_ref[...], preferred_element_type=F32))
        o_ref[0, rows] = _layer_norm(ALPHA * x_ref[0, rows] + y, g_ref[...], b_ref[...])


def _merge(ma, mh, x, w_out, g, b, tl):
    bsz, seq, d = x.shape
    wa, wh = w_out[:ATTN_WIDTH], w_out[ATTN_WIDTH:]
    half = pl.BlockSpec((1, tl, ATTN_WIDTH), lambda bb, i: (bb, i, 0))
    full = pl.BlockSpec((1, tl, d), lambda bb, i: (bb, i, 0))
    return pl.pallas_call(
        _merge_kernel,
        out_shape=jax.ShapeDtypeStruct(x.shape, F32),
        grid=(bsz, seq // tl),
        in_specs=[half, half, full, _resident(wa.shape), _resident(wh.shape),
                  _resident(g.shape), _resident(b.shape)],
        out_specs=full,
        compiler_params=_params(("parallel", "parallel")),
        name="merge_ln1",
    )(ma, mh, x, wa, wh, g, b)


FF_CHUNK = 256
N_FF_CHUNKS = D_FF // FF_CHUNK


def _ffn_kernel(xp_ref, x_ref, xn_ref, wi_ref, cw_ref, cb_ref, wo_ref, g_ref, b_ref,
                o_ref, xs_ref, hid_ref, os_ref, *, tl):
    _fill_halo_slabs(xs_ref, xp_ref, x_ref, xn_ref, tl)
    xh = _load_interleaved(xs_ref).astype(BF16)
    for j in range(N_FF_CHUNKS):
        conv = []
        for part in range(2):
            cols = slice(part * D_FF + j * FF_CHUNK, part * D_FF + (j + 1) * FF_CHUNK)
            u = jnp.dot(xh, wi_ref[:, cols], preferred_element_type=F32)
            conv.append(_dwconv3_interleaved(u, cw_ref[:, cols], cb_ref[:, cols]))
        gate = conv[1]
        gelu = 0.5 * gate * (1.0 + lax.erf(gate * (2.0 ** -0.5)))
        hid_ref[:, j * FF_CHUNK:(j + 1) * FF_CHUNK] = (conv[0] * gelu).astype(BF16)
    y = jnp.dot(hid_ref[...], wo_ref[...], preferred_element_type=F32)
    out = _layer_norm(ALPHA * _load_interleaved(xs_ref) + y, g_ref[...], b_ref[...])
    _store_natural(os_ref, 0, out)
    for k in range(os_ref.shape[0]):
        o_ref[0, :, k * LANES:(k + 1) * LANES] = os_ref[k, HALO:HALO + tl, :]


def _ffn(x1, w_in, conv_w, conv_b, w_out, g, b, tl):
    bsz, seq, d = x1.shape
    cb = conv_b.reshape(1, -1)
    return pl.pallas_call(
        functools.partial(_ffn_kernel, tl=tl),
        out_shape=jax.ShapeDtypeStruct(x1.shape, F32),
        grid=(bsz, seq // tl),
        in_specs=_halo_specs(tl, seq, d) + [_resident(w_in.shape), _resident(conv_w.shape), _resident(cb.shape),
                                            _resident(w_out.shape), _resident(g.shape), _resident(b.shape)],
        out_specs=pl.BlockSpec((1, tl, d), lambda bb, i: (bb, i, 0)),
        scratch_shapes=[pltpu.VMEM((d // LANES, tl + 2 * HALO, LANES), F32),
                        pltpu.VMEM((tl + 2 * HALO, D_FF), BF16),
                        pltpu.VMEM((d // LANES, tl + 2 * HALO, LANES), F32)],
        compiler_params=_params(("parallel", "parallel")),
        name="conv_ffn",
    )(x1, x1, x1, w_in, conv_w, cb, w_out, g, b)


def _encoder_layer(x, p, tl=1024, rb=8):
    seq = x.shape[1]
    q, k, v, zg, g0 = _in_proj(x, p["w_in"], p["short_w"], p["short_b"], tl)
    ma = _attention(q, k, v, p["bias"], p["g_attn"], rb)
    a, d = _filter_taps(seq, p["filt_w1"], p["filt_b1"], p["filt_freq"], p["filt_w_inner"],
                        p["filt_b_inner"], p["filt_w3"])
    cr, sr = _pair_coefficients(seq)
    t, tt = _dft_matrices(seq)
    gspec = _filter_spectrum(cr, sr, t, a, d)
    mh = _hyena_conv(zg, g0, cr, sr, t, tt, gspec, p["filt_bias"], p["g_hyena"])
    x1 = _merge(ma, mh, x, p["w_out"], p["ln1_g"], p["ln1_b"], tl)
    return _ffn(x1, p["ffn_w_in"], p["ffn_conv_w"], p["ffn_conv_b"], p["ffn_w_out"],
                p["ln2_g"], p["ln2_b"], tl)


def kernel(x_prompt, x_sample, w_in, short_w, short_b, rpb, filt_w1, filt_b1, filt_freq, filt_w_inner,
           filt_b_inner, filt_w3, filt_bias, g_attn, g_hyena, w_out, ln1_g, ln1_b, ffn_w_in, ffn_conv_w,
           ffn_conv_b, ffn_w_out, ln2_g, ln2_b):
    assert w_in.shape[0] == DEPTH == 1
    row = lambda a: a[0].reshape(1, -1)
    p = dict(
        w_in=w_in[0].astype(BF16), short_w=short_w[0], short_b=short_b[0],
        bias=_attention_bias(rpb[0]), g_attn=row(g_attn),
        filt_w1=filt_w1[0], filt_b1=filt_b1[0], filt_freq=filt_freq[0], filt_w_inner=filt_w_inner[0],
        filt_b_inner=filt_b_inner[0], filt_w3=filt_w3[0], filt_bias=filt_bias[0], g_hyena=row(g_hyena),
        w_out=w_out[0].astype(BF16), ln1_g=row(ln1_g), ln1_b=row(ln1_b),
        ffn_w_in=ffn_w_in[0].astype(BF16), ffn_conv_w=ffn_conv_w[0], ffn_conv_b=ffn_conv_b[0],
        ffn_w_out=ffn_w_out[0].astype(BF16), ln2_g=row(ln2_g), ln2_b=row(ln2_b),
    )
    return (_encoder_layer(x_prompt, p), _encoder_layer(x_sample, p))
```

```python
import functools
import math

import numpy as np
import jax
import jax.numpy as jnp
from jax import lax
from jax.experimental import pallas as pl
from jax.experimental.pallas import tpu as pltpu

F32 = jnp.float32
BF16 = jnp.bfloat16

D_MODEL = 1024
GRID_W = 64
ATTN_WIDTH = 512
HYENA_WIDTH = 512
HEAD_DIM = 64
N_HEADS = ATTN_WIDTH // HEAD_DIM
NA_ROWS = 8
NA_COLS = 16
FILTER_EMB = 33
FILTER_BANDS = (FILTER_EMB - 1) // 2
FILTER_HIDDEN = 64
FILTER_INNER = 2
MAX_DECAY = math.log(1e-2) / 0.3
MIN_DECAY = math.log(1e-2) / 1.5
D_FF = 2816
DEPTH = 1
ALPHA = (2 * DEPTH) ** 0.25
LN_EPS = 1e-5
RMS_EPS = 1e-6
LOG2E = math.log2(math.e)

SUBLANES = 8
LANES = 128
HALO = SUBLANES
VMEM_LIMIT = 56 * 1024 * 1024

HIGHEST = lax.Precision.HIGHEST


def _params(sem):
    return pltpu.CompilerParams(dimension_semantics=sem, vmem_limit_bytes=VMEM_LIMIT)


def _resident(shape):
    nd = len(shape)
    return pl.BlockSpec(shape, lambda *_: (0,) * nd, pipeline_mode=pl.Buffered(1))


def _layer_norm(y, g, b):
    mu = jnp.mean(y, axis=-1, keepdims=True)
    yc = y - mu
    var = jnp.mean(yc * yc, axis=-1, keepdims=True)
    return yc * lax.rsqrt(var + LN_EPS) * g + b


def _rms_norm(y, g):
    ms = jnp.mean(y * y, axis=-1, keepdims=True)
    return y * lax.rsqrt(ms + RMS_EPS) * g


def _fill_halo_slabs(xs_ref, xp_ref, x_ref, xn_ref, tl):
    i = pl.program_id(1)
    last = pl.num_programs(1) - 1
    prev = jnp.where(i > 0, xp_ref[0], 0.0)
    nxt = jnp.where(i < last, xn_ref[0], 0.0)
    for k in range(xs_ref.shape[0]):
        sl = slice(k * LANES, (k + 1) * LANES)
        xs_ref[k, 0:HALO, :] = prev[:, sl]
        xs_ref[k, HALO:HALO + tl, :] = x_ref[0, :, sl]
        xs_ref[k, HALO + tl:HALO + tl + HALO, :] = nxt[:, sl]


def _load_interleaved(xs_ref):
    nslab, rows, _ = xs_ref.shape
    nv = rows // SUBLANES
    assert rows % SUBLANES == 0 and nv % SUBLANES != 0
    return jnp.concatenate(
        [jnp.concatenate([xs_ref[k, pl.ds(j, SUBLANES, stride=nv), :] for j in range(nv)], axis=0)
         for k in range(nslab)], axis=1)


def _store_natural(os_ref, first, val):
    nv = os_ref.shape[1] // SUBLANES
    for k in range(val.shape[1] // LANES):
        for j in range(nv):
            os_ref[first + k, pl.ds(j, SUBLANES, stride=nv), :] = val[j * SUBLANES:(j + 1) * SUBLANES,
                                                                      k * LANES:(k + 1) * LANES]


def _dwconv3_interleaved(u, w, b):
    head = pltpu.roll(u[-SUBLANES:], 1, axis=0)
    tail = pltpu.roll(u[:SUBLANES], SUBLANES - 1, axis=0)
    up = jnp.concatenate([head, u[:-SUBLANES]], axis=0)
    un = jnp.concatenate([u[SUBLANES:], tail], axis=0)
    return up * w[0:1] + u * w[1:2] + un * w[2:3] + b


def _halo_specs(tl, seq, d):
    nb = tl // HALO
    last = seq // HALO - 1
    return [
        pl.BlockSpec((1, HALO, d), lambda b, i: (b, jnp.maximum(i * nb - 1, 0), 0)),
        pl.BlockSpec((1, tl, d), lambda b, i: (b, i, 0)),
        pl.BlockSpec((1, HALO, d), lambda b, i: (b, jnp.minimum((i + 1) * nb, last), 0)),
    ]


def _in_proj_kernel(xp_ref, x_ref, xn_ref, w_ref, sw_ref, sb_ref,
                    q_ref, k_ref, v_ref, zg_ref, g0_ref, xs_ref, os_ref, *, tl):
    xq = x_ref[0].astype(BF16)
    proj = lambda n: jnp.dot(xq, w_ref[:, n * ATTN_WIDTH:(n + 1) * ATTN_WIDTH], preferred_element_type=F32)
    q_ref[0] = (proj(0) * (HEAD_DIM ** -0.5 * LOG2E)).astype(BF16)
    k_ref[0] = proj(1).astype(BF16)
    v = proj(2).astype(BF16)
    ones = jnp.ones((tl, LANES), BF16)
    v_ref[0] = jnp.concatenate([piece for p in range(N_HEADS // 2)
                                for piece in (v[:, p * LANES:(p + 1) * LANES], ones)], axis=1)
    _fill_halo_slabs(xs_ref, xp_ref, x_ref, xn_ref, tl)
    xh = _load_interleaved(xs_ref).astype(BF16)
    conv = []
    for n in range(3):
        lo = 3 * ATTN_WIDTH + n * HYENA_WIDTH
        u = jnp.dot(xh, w_ref[:, lo:lo + HYENA_WIDTH], preferred_element_type=F32)
        conv.append(_dwconv3_interleaved(u, sw_ref[n], sb_ref[n]))
    nslab = HYENA_WIDTH // LANES
    for first, ref, val in ((0, g0_ref, conv[0]), (nslab, zg_ref, conv[2] * conv[1])):
        _store_natural(os_ref, first, val)
        for k in range(nslab):
            ref[0, :, k * LANES:(k + 1) * LANES] = os_ref[first + k, HALO:HALO + tl, :].astype(BF16)


def _in_proj(x, w_in, short_w, short_b, tl):
    bsz, seq, d = x.shape
    sw = short_w.reshape(3, 3, HYENA_WIDTH).transpose(1, 0, 2)
    sb = short_b.reshape(3, 1, HYENA_WIDTH)
    out = jax.ShapeDtypeStruct((bsz, seq, ATTN_WIDTH), BF16)
    ospec = pl.BlockSpec((1, tl, ATTN_WIDTH), lambda b, i: (b, i, 0))
    vout = jax.ShapeDtypeStruct((bsz, seq, 2 * ATTN_WIDTH), BF16)
    vspec = pl.BlockSpec((1, tl, 2 * ATTN_WIDTH), lambda b, i: (b, i, 0))
    return pl.pallas_call(
        functools.partial(_in_proj_kernel, tl=tl),
        out_shape=(out, out, vout, out, out),
        grid=(bsz, seq // tl),
        in_specs=_halo_specs(tl, seq, d) + [_resident(w_in.shape), _resident(sw.shape), _resident(sb.shape)],
        out_specs=(ospec, ospec, vspec, ospec, ospec),
        scratch_shapes=[pltpu.VMEM((d // LANES, tl + 2 * HALO, LANES), F32),
                        pltpu.VMEM((2 * HYENA_WIDTH // LANES, tl + 2 * HALO, LANES), F32)],
        compiler_params=_params(("parallel", "parallel")),
        name="in_proj",
    )(x, x, x, w_in, sw, sb)


N_PAIRS = N_HEADS // 2
KEY_WIN = NA_ROWS * GRID_W
N_DR = 2 * NA_ROWS - 1
N_DC = 2 * NA_COLS - 1
SOFTMAX_PARTS = 4


def _bias_kernel(rpb_ref, ea_ref, eb_ref, o_ref, t_ref):
    row = lax.broadcasted_iota(jnp.int32, (GRID_W, LANES), 0)
    lane = lax.broadcasted_iota(jnp.int32, (GRID_W, LANES), 1)
    first = lane < GRID_W
    qcol = lane & (GRID_W - 1)
    col_start = jnp.clip(qcol - NA_COLS // 2, 0, GRID_W - NA_COLS)
    valid = (row >= col_start) & (row < col_start + NA_COLS)
    wa = jnp.dot(rpb_ref[0, 0], ea_ref[...], precision=HIGHEST, preferred_element_type=F32)
    wb = jnp.dot(rpb_ref[0, 1], eb_ref[...], precision=HIGHEST, preferred_element_type=F32)

    def rotate_rows(x):
        return pltpu.roll(x, 0, axis=1, stride=1, stride_axis=0)

    for dr in range(N_DR):
        ta = rotate_rows(jnp.broadcast_to(wa[dr:dr + 1], (GRID_W, LANES)))
        tb = rotate_rows(jnp.broadcast_to(wb[dr:dr + 1], (GRID_W, LANES)))
        t_ref[dr] = jnp.where(valid, jnp.where(first, ta, tb) * LOG2E, -jnp.inf)
    for cls in range(NA_ROWS):
        for i in range(NA_ROWS):
            o_ref[0, cls, i * GRID_W:(i + 1) * GRID_W, :] = t_ref[i - cls + NA_ROWS - 1]


def _toeplitz_selectors():
    m = np.arange(LANES)
    dc = np.where(m < GRID_W, (NA_COLS - 1) - np.minimum(m, NA_COLS - 1),
                  np.minimum(LANES - m, NA_COLS - 1) + (NA_COLS - 1))
    ea = np.zeros((LANES, LANES), np.float32)
    ea[dc, m] = 1.0
    ea[:, GRID_W] = 0.0
    eb = np.roll(ea, GRID_W, axis=1)
    return ea, eb


def _attention_bias(rpb):
    ea, eb = _toeplitz_selectors()
    rp = jnp.pad(rpb.astype(F32), ((0, 0), (0, 16 - N_DR), (0, LANES - N_DC))).reshape(N_PAIRS, 2, 16, LANES)
    return pl.pallas_call(
        _bias_kernel,
        out_shape=jax.ShapeDtypeStruct((N_PAIRS, NA_ROWS, KEY_WIN, LANES), F32),
        grid=(N_PAIRS,),
        in_specs=[pl.BlockSpec((1, 2, 16, LANES), lambda p: (p, 0, 0, 0)),
                  _resident(ea.shape), _resident(eb.shape)],
        out_specs=pl.BlockSpec((1, NA_ROWS, KEY_WIN, LANES), lambda p: (p, 0, 0, 0)),
        scratch_shapes=[pltpu.VMEM((N_DR, GRID_W, LANES), F32)],
        compiler_params=_params(("parallel",)),
        name="attn_bias",
    )(rp, jnp.asarray(ea), jnp.asarray(eb))


def _attn_kernel(q_ref, k_ref, vx_ref, bias_ref, g_ref, o_ref, *, rows, rb, unroll):
    jb = pl.program_id(1)
    lane = lax.broadcasted_iota(jnp.int32, (GRID_W, LANES), 1)
    first = lane < HEAD_DIM
    keep = (first.astype(F32).astype(BF16), (~first).astype(F32).astype(BF16))
    nt = (((1,), (1,)), ((), ()))
    tn = (((0,), (0,)), ((), ()))

    def score_stage(j):
        r = jb * rb + j
        start = jnp.clip(r - NA_ROWS // 2, 0, rows - NA_ROWS)
        qoff = pl.multiple_of(j * GRID_W, GRID_W)
        koff = pl.multiple_of(start * GRID_W, GRID_W)
        scores = []
        for p in range(N_PAIRS):
            sl = slice(p * LANES, (p + 1) * LANES)
            qp = q_ref[0, pl.ds(qoff, GRID_W), sl]
            qblk = jnp.concatenate([qp * keep[0], qp * keep[1]], axis=0)
            kp = k_ref[0, pl.ds(koff, KEY_WIN), sl]
            scores.append(lax.dot_general(kp, qblk, nt, preferred_element_type=F32))
        return r - start, qoff, koff, scores

    def output_stage(cls, qoff, koff, scores):
        probs = []
        step = KEY_WIN // SOFTMAX_PARTS
        for p in range(N_PAIRS):
            parts, maxes = [], []
            for lo in range(0, KEY_WIN, step):
                s = scores[p][lo:lo + step] + bias_ref[p, cls, lo:lo + step]
                m_g = jnp.max(s, axis=0, keepdims=True)
                parts.append(jnp.exp2(s - m_g))
                maxes.append(m_g)
            m = functools.reduce(jnp.maximum, maxes)
            probs.append(jnp.concatenate([e * jnp.exp2(m_g - m) for e, m_g in zip(parts, maxes)],
                                         axis=0).astype(BF16))
        outs = []
        for p in range(N_PAIRS):
            vx = vx_ref[0, pl.ds(koff, KEY_WIN), 2 * p * LANES:2 * (p + 1) * LANES]
            o = lax.dot_general(probs[p], vx, tn, preferred_element_type=F32)
            oa = o[:GRID_W, :LANES] / o[:GRID_W, LANES:]
            ob = o[GRID_W:, :LANES] / o[GRID_W:, LANES:]
            outs.append(jnp.where(first, oa, ob))
        o_ref[0, pl.ds(qoff, GRID_W), :] = _rms_norm(jnp.concatenate(outs, axis=1), g_ref[...]).astype(BF16)

    def body(jj, carry):
        state = score_stage(jj * unroll)
        for u in range(unroll):
            ahead = score_stage(jj * unroll + u + 1) if u + 1 < unroll else None
            output_stage(*state)
            state = ahead
        return carry

    lax.fori_loop(0, rb // unroll, body, 0)


def _attention(q, k, vx, bias, g_attn, rb, unroll=8):
    bsz, seq, w = q.shape
    rows = seq // GRID_W
    tq = rb * GRID_W
    return pl.pallas_call(
        functools.partial(_attn_kernel, rows=rows, rb=rb, unroll=unroll),
        out_shape=jax.ShapeDtypeStruct((bsz, seq, w), BF16),
        grid=(bsz, rows // rb),
        in_specs=[
            pl.BlockSpec((1, tq, w), lambda b, i: (b, i, 0)),
            pl.BlockSpec((1, seq, w), lambda b, i: (b, 0, 0)),
            pl.BlockSpec((1, seq, 2 * w), lambda b, i: (b, 0, 0)),
            _resident(bias.shape),
            _resident(g_attn.shape),
        ],
        out_specs=pl.BlockSpec((1, tq, w), lambda b, i: (b, i, 0)),
        compiler_params=_params(("parallel", "arbitrary")),
        name="attention",
    )(q, k, vx, bias, g_attn)


def _filter_kernel(w1t_ref, w1c_ref, w1s_ref, b1_ref, fq_ref, wi_ref, bi_ref, w3_ref,
                   a_ref, d_ref, cb_ref, sb_ref, h_ref, *, seq, tl):
    i = pl.program_id(0)
    rows = tl + 2 * SUBLANES
    half = rows // 2
    band = lax.broadcasted_iota(jnp.int32, (rows, LANES), 1)
    freqs = jnp.where(band < FILTER_BANDS,
                      1e-4 + band.astype(F32) * ((FILTER_BANDS - 1 - 1e-4) / (FILTER_BANDS - 1)), 0.0)
    rad = freqs * (2.0 * math.pi / seq)
    local = lax.broadcasted_iota(jnp.int32, (rows, LANES), 0)

    @pl.when(i == 0)
    def _():
        cb_ref[...] = jnp.cos(local.astype(F32) * rad)
        sb_ref[...] = jnp.sin(local.astype(F32) * rad)

    base = (i * tl).astype(F32) * rad[0:SUBLANES]
    ca = jnp.cos(base)[0:1]
    sa = jnp.sin(base)[0:1]
    cos_ang = ca * cb_ref[...] - sa * sb_ref[...]
    sin_ang = sa * cb_ref[...] + ca * sb_ref[...]
    t = (local + i * tl).astype(F32) * (1.0 / (seq - 1))
    fq = fq_ref[...]
    dot = functools.partial(jnp.dot, precision=HIGHEST, preferred_element_type=F32)
    side = lambda x: jnp.concatenate([x[:half], x[half:]], axis=1)
    left = lax.broadcasted_iota(jnp.int32, (half, LANES), 1) < FILTER_HIDDEN
    t_packed = jnp.where(left, t[:half], t[half:])
    pre = (t_packed * w1t_ref[...] + dot(side(cos_ang), w1c_ref[...]) - dot(side(sin_ang), w1s_ref[...])
           + b1_ref[...])
    h = jnp.sin(fq * pre)
    for n in range(FILTER_INNER):
        h = jnp.sin(fq * (dot(h, wi_ref[n]) + bi_ref[n]))
    chan = lax.broadcasted_iota(jnp.int32, (half, HYENA_WIDTH), 1).astype(F32)
    deltas = jnp.abs(MIN_DECAY + chan * ((MAX_DECAY - MIN_DECAY) / (HYENA_WIDTH - 1)))
    for part in range(2):
        rws = slice(part * half, (part + 1) * half)
        pos = lax.broadcasted_iota(jnp.int32, (half, HYENA_WIDTH), 0) + (i * tl + part * half)
        decay = jnp.exp(-(pos.astype(F32) * (1.0 / (seq - 1))) * deltas)
        for n in range(2):
            cols = slice(n * HYENA_WIDTH, (n + 1) * HYENA_WIDTH)
            h_ref[rws, cols] = dot(h, w3_ref[part, :, cols]) * decay
    h_fwd = h_ref[0:tl, 0:HYENA_WIDTH]
    lag = lax.broadcasted_iota(jnp.int32, (tl, HYENA_WIDTH), 0) + (i * tl + 1)
    h_bwd = jnp.where(lag < seq, h_ref[pl.ds(1, tl), HYENA_WIDTH:2 * HYENA_WIDTH], 0.0)
    a_ref[...] = h_fwd + h_bwd
    d_ref[...] = h_bwd - h_fwd


def _filter_taps(seq, w1, b1, freq, w_inner, b_inner, w3, tl=512):
    pad = LANES - FILTER_BANDS
    twice = lambda v: jnp.tile(v.reshape(1, -1), (1, 2))
    blockdiag = lambda w: jnp.kron(jnp.eye(2, dtype=w.dtype), w)
    w1c = blockdiag(jnp.pad(w1[1:1 + FILTER_BANDS], ((0, pad), (0, 0))))
    w1s = blockdiag(jnp.pad(w1[1 + FILTER_BANDS:], ((0, pad), (0, 0))))
    zero = jnp.zeros_like(w3)
    args = (twice(w1[0]), w1c, w1s, twice(b1), twice(freq),
            jnp.stack([blockdiag(w_inner[n]) for n in range(FILTER_INNER)]),
            jnp.stack([twice(b_inner[n]) for n in range(FILTER_INNER)]),
            jnp.stack([jnp.concatenate([w3, zero]), jnp.concatenate([zero, w3])]))
    out = jax.ShapeDtypeStruct((seq, HYENA_WIDTH), F32)
    ospec = pl.BlockSpec((tl, HYENA_WIDTH), lambda i: (i, 0))
    return pl.pallas_call(
        functools.partial(_filter_kernel, seq=seq, tl=tl),
        out_shape=(out, out),
        grid=(seq // tl,),
        in_specs=[_resident(a.shape) for a in args],
        out_specs=(ospec, ospec),
        scratch_shapes=[pltpu.VMEM((tl + 2 * SUBLANES, LANES), F32),
                        pltpu.VMEM((tl + 2 * SUBLANES, LANES), F32),
                        pltpu.VMEM((tl + 2 * SUBLANES, 2 * HYENA_WIDTH), F32)],
        compiler_params=_params(("arbitrary",)),
        name="hyena_filter",
    )(*args)


P_BLK = 512
F_HALF = P_BLK // 2


def _pair_coefficients(seq):
    s1n = seq // P_BLK
    g = np.arange(s1n)[:, None]
    s = np.arange(s1n)[None, :]
    ang = np.pi * (2 * g + 1) * s / (2 * s1n)
    return jnp.asarray(np.cos(ang), F32), jnp.asarray(np.sin(ang), F32)


def _dft_kernel(t_ref, tt_ref, cb_ref, sb_ref, *, seq):
    g = pl.program_id(0)
    s1n = seq // P_BLK

    @pl.when(g == 0)
    def _():
        f2 = lax.broadcasted_iota(jnp.int32, (F_HALF, P_BLK), 0)
        s_odd = 2 * lax.broadcasted_iota(jnp.int32, (F_HALF, P_BLK), 1) + 1
        beta = ((f2 * s_odd) & (2 * P_BLK - 1)).astype(F32) * (math.pi / P_BLK)
        cb_ref[...] = jnp.cos(beta)
        sb_ref[...] = jnp.sin(beta)

    cb = cb_ref[...]
    sb = sb_ref[...]
    s_odd = 2 * lax.broadcasted_iota(jnp.int32, (SUBLANES, P_BLK), 1) + 1

    def block(f1):
        alpha = ((s_odd * (2 * f1 + 1)) & (8 * seq - 1)).astype(F32) * (math.pi / (4 * seq))
        ca = jnp.cos(alpha)[0:1]
        sa = jnp.sin(alpha)[0:1]
        return ca * cb - sa * sb, sa * cb + ca * sb

    mra, mia = block(g)
    mrb, mib = block(2 * s1n - 1 - g)
    for r, (left, right) in enumerate(((mra, -mia), (mia, mra), (mrb, mib), (mib, -mrb))):
        rows = slice(r * F_HALF, (r + 1) * F_HALF)
        t_ref[0, rows, :P_BLK] = left.astype(BF16)
        t_ref[0, rows, P_BLK:] = right.astype(BF16)
        tt_ref[0, :P_BLK, rows] = left.T.astype(BF16)
        tt_ref[0, P_BLK:, rows] = right.T.astype(BF16)


def _dft_matrices(seq):
    s1n = seq // P_BLK
    mat = jax.ShapeDtypeStruct((s1n, 2 * P_BLK, 2 * P_BLK), BF16)
    spec = pl.BlockSpec((1, 2 * P_BLK, 2 * P_BLK), lambda g: (g, 0, 0))
    return pl.pallas_call(
        functools.partial(_dft_kernel, seq=seq),
        out_shape=(mat, mat),
        grid=(s1n,),
        out_specs=(spec, spec),
        scratch_shapes=[pltpu.VMEM((F_HALF, P_BLK), F32), pltpu.VMEM((F_HALF, P_BLK), F32)],
        compiler_params=_params(("arbitrary",)),
        name="dft_matrices",
    )()


_SMEM = pl.BlockSpec(memory_space=pltpu.SMEM)
_COL_HALVES = (slice(0, HYENA_WIDTH // 2), slice(HYENA_WIDTH // 2, HYENA_WIDTH))


ROW_CHUNK = 64


def _stacked_blocks(cr_ref, sr_ref, g, block, s1n):
    re, im = [], []
    for r in range(0, P_BLK, ROW_CHUNK):
        ar = block(0, r)
        bi = None
        for s in range(1, s1n):
            b = block(s, r)
            ar = ar + cr_ref[g, s] * b
            bi = sr_ref[g, s] * b if bi is None else bi + sr_ref[g, s] * b
        re.append(ar)
        im.append(bi)
    return jnp.concatenate(re + im, axis=0).astype(BF16)


def _spectrum_kernel(cr_ref, sr_ref, t_ref, a_ref, d_ref, g_ref, *, seq):
    g = pl.program_id(0)
    s1n = seq // P_BLK
    time_block = lambda ref: (lambda s, r: ref[s * P_BLK + r:s * P_BLK + r + ROW_CHUNK, :])
    t_re = jnp.concatenate([t_ref[0, 0:F_HALF], t_ref[0, P_BLK:P_BLK + F_HALF]], axis=0)
    t_im = jnp.concatenate([t_ref[0, F_HALF:P_BLK], t_ref[0, P_BLK + F_HALF:2 * P_BLK]], axis=0)
    ka = jnp.dot(t_re, _stacked_blocks(cr_ref, sr_ref, g, time_block(a_ref), s1n),
                 preferred_element_type=F32)
    kd = jnp.dot(t_im, _stacked_blocks(cr_ref, sr_ref, g, time_block(d_ref), s1n),
                 preferred_element_type=F32)
    f2 = lax.broadcasted_iota(jnp.int32, (F_HALF, LANES), 0)
    reps = HYENA_WIDTH // LANES
    for blk, f1 in ((0, g), (1, 2 * s1n - 1 - g)):
        lo = blk * P_BLK
        phi = (2 * (f1 + 2 * s1n * f2) + 1).astype(F32) * (math.pi / (4 * seq))
        c = pltpu.repeat(jnp.cos(phi), reps, axis=1) * (1.0 / seq)
        s = pltpu.repeat(jnp.sin(phi), reps, axis=1) * (1.0 / seq)
        for r in range(0, F_HALF, ROW_CHUNK):
            rows = slice(blk * F_HALF + r, blk * F_HALF + r + ROW_CHUNK)
            cc, ss = c[r:r + ROW_CHUNK], s[r:r + ROW_CHUNK]
            g_ref[0, lo + r:lo + r + ROW_CHUNK] = cc * ka[rows] - ss * kd[rows]
            g_ref[0, lo + F_HALF + r:lo + F_HALF + r + ROW_CHUNK] = cc * kd[rows] + ss * ka[rows]


def _filter_spectrum(cr, sr, t, a, d):
    seq = a.shape[0]
    s1n = seq // P_BLK
    return pl.pallas_call(
        functools.partial(_spectrum_kernel, seq=seq),
        out_shape=jax.ShapeDtypeStruct((s1n, 2 * P_BLK, HYENA_WIDTH), F32),
        grid=(s1n,),
        in_specs=[_SMEM, _SMEM, pl.BlockSpec((1, 2 * P_BLK, 2 * P_BLK), lambda g: (g, 0, 0)),
                  _resident(a.shape), _resident(d.shape)],
        out_specs=pl.BlockSpec((1, 2 * P_BLK, HYENA_WIDTH), lambda g: (g, 0, 0)),
        compiler_params=_params(("parallel",)),
        name="hyena_spectrum",
    )(cr, sr, t, a, d)


def _hy_fwd_kernel(cr_ref, sr_ref, t_ref, g_ref, z_ref, y_ref, *, s1n, nb):
    g = pl.program_id(0)
    for bi in range(nb):
        for cols in _COL_HALVES:
            x = _stacked_blocks(
                cr_ref, sr_ref, g,
                lambda s, r: z_ref[bi, s * P_BLK + r:s * P_BLK + r + ROW_CHUNK, cols].astype(F32), s1n)
            res = jnp.dot(t_ref[0], x, preferred_element_type=F32)
            for lo in range(0, 2 * P_BLK, P_BLK):
                for r in range(lo, lo + F_HALF, ROW_CHUNK):
                    re_rows = slice(r, r + ROW_CHUNK)
                    im_rows = slice(r + F_HALF, r + F_HALF + ROW_CHUNK)
                    zr, wi = res[re_rows], res[im_rows]
                    gr, gi = g_ref[0, re_rows, cols], g_ref[0, im_rows, cols]
                    y_ref[bi, 0, re_rows, cols] = (gr * zr + gi * wi).astype(BF16)
                    y_ref[bi, 0, im_rows, cols] = (gr * wi - gi * zr).astype(BF16)


def _hy_inv_kernel(cr_ref, sr_ref, tt_ref, y_ref, z_ref, g0_ref, fb_ref, gn_ref, o_ref,
                   acc_ref, zs_ref, g0s_ref, *, s1n, nb):
    g = pl.program_id(1)

    @pl.when(g == 0)
    def _():
        acc_ref[...] = jnp.zeros_like(acc_ref)

    tblk = pl.ds(pl.multiple_of(g * P_BLK, P_BLK), P_BLK)
    zs_ref[:, tblk, :] = z_ref[...]
    g0s_ref[:, tblk, :] = g0_ref[...]
    for bi in range(nb):
        for cols in _COL_HALVES:
            uv = jnp.dot(tt_ref[g], y_ref[bi, 0, :, cols], preferred_element_type=F32)
            for r in range(0, P_BLK, ROW_CHUNK):
                u = uv[r:r + ROW_CHUNK]
                vn = uv[P_BLK + r:P_BLK + r + ROW_CHUNK]
                acc_ref[bi, r:r + ROW_CHUNK, cols] += u
                for t1 in range(1, s1n):
                    rows = slice(t1 * P_BLK + r, t1 * P_BLK + r + ROW_CHUNK)
                    acc_ref[bi, rows, cols] += cr_ref[g, t1] * u + sr_ref[g, t1] * vn

    @pl.when(g == s1n - 1)
    def _():
        for bi in range(nb):
            for r in range(0, s1n * P_BLK, ROW_CHUNK):
                rows = slice(r, r + ROW_CHUNK)
                y = ((acc_ref[bi, rows, :] + zs_ref[bi, rows, :].astype(F32) * fb_ref[...])
                     * g0s_ref[bi, rows, :].astype(F32))
                o_ref[bi, rows, :] = _rms_norm(y, gn_ref[...]).astype(BF16)


_INV_STATE_BUDGET = 28 * 1024 * 1024


def _hyena_conv(zg, g0, cr, sr, t, tt, gspec, fbias, g_hyena):
    bsz, seq, w = zg.shape
    s1n = seq // P_BLK
    nb = 2 if bsz % 2 == 0 else 1
    mat = pl.BlockSpec((1, 2 * P_BLK, 2 * P_BLK), lambda g, b: (g, 0, 0))
    spec = pl.BlockSpec((1, 2 * P_BLK, w), lambda g, b: (g, 0, 0))
    y = pl.pallas_call(
        functools.partial(_hy_fwd_kernel, s1n=s1n, nb=nb),
        out_shape=jax.ShapeDtypeStruct((bsz, s1n, 2 * P_BLK, w), BF16),
        grid=(s1n, bsz // nb),
        in_specs=[_SMEM, _SMEM, mat, spec, pl.BlockSpec((nb, seq, w), lambda g, b: (b, 0, 0))],
        out_specs=pl.BlockSpec((nb, 1, 2 * P_BLK, w), lambda g, b: (b, g, 0, 0)),
        compiler_params=_params(("parallel", "parallel")),
        name="hyena_fwd",
    )(cr, sr, t, gspec, zg)
    state_bytes = seq * w * (4 + 2 + 2 + 2 * 2)
    nb = 2 if bsz % 2 == 0 and 2 * state_bytes <= _INV_STATE_BUDGET else 1
    blk = pl.BlockSpec((nb, P_BLK, w), lambda b, g: (b, g, 0))
    return pl.pallas_call(
        functools.partial(_hy_inv_kernel, s1n=s1n, nb=nb),
        out_shape=jax.ShapeDtypeStruct((bsz, seq, w), BF16),
        grid=(bsz // nb, s1n),
        in_specs=[_SMEM, _SMEM, _resident(tt.shape),
                  pl.BlockSpec((nb, 1, 2 * P_BLK, w), lambda b, g: (b, g, 0, 0)),
                  blk, blk, _resident(fbias.shape), _resident(g_hyena.shape)],
        out_specs=pl.BlockSpec((nb, seq, w), lambda b, g: (b, 0, 0)),
        scratch_shapes=[pltpu.VMEM((nb, seq, w), F32), pltpu.VMEM((nb, seq, w), BF16),
                        pltpu.VMEM((nb, seq, w), BF16)],
        compiler_params=_params(("parallel", "arbitrary")),
        name="hyena_inv",
    )(cr, sr, tt, y, zg, g0, fbias, g_hyena)


MERGE_ROW_CHUNKS = 4


def _merge_kernel(ma_ref, mh_ref, x_ref, wa_ref, wh_ref, g_ref, b_ref, o_ref):
    tl = x_ref.shape[1]
    chunk = tl // MERGE_ROW_CHUNKS
    for lo in range(0, tl, chunk):
        rows = slice(lo, lo + chunk)
        y = (jnp.dot(ma_ref[0, rows], wa_ref[...], preferred_element_type=F32)
             + jnp.dot(mh_ref[0, rows], wh_ref[...], preferred_element_type=F32))
        o_ref[0, rows] = _layer_norm(ALPHA * x_ref[0, rows] + y, g_ref[...], b_ref[...])


def _merge(ma, mh, x, w_out, g, b, tl):
    bsz, seq, d = x.shape
    wa, wh = w_out[:ATTN_WIDTH], w_out[ATTN_WIDTH:]
    half = pl.BlockSpec((1, tl, ATTN_WIDTH), lambda bb, i: (bb, i, 0))
    full = pl.BlockSpec((1, tl, d), lambda bb, i: (bb, i, 0))
    return pl.pallas_call(
        _merge_kernel,
        out_shape=jax.ShapeDtypeStruct(x.shape, F32),
        grid=(bsz, seq // tl),
        in_specs=[half, half, full, _resident(wa.shape), _resident(wh.shape),
                  _resident(g.shape), _resident(b.shape)],
        out_specs=full,
        compiler_params=_params(("parallel", "parallel")),
        name="merge_ln1",
    )(ma, mh, x, wa, wh, g, b)


FF_CHUNK = 256
N_FF_CHUNKS = D_FF // FF_CHUNK


def _ffn_kernel(xp_ref, x_ref, xn_ref, wi_ref, cw_ref, cb_ref, wo_ref, g_ref, b_ref,
                o_ref, xs_ref, hid_ref, os_ref, *, tl):
    _fill_halo_slabs(xs_ref, xp_ref, x_ref, xn_ref, tl)
    xh = _load_interleaved(xs_ref).astype(BF16)
    for j in range(N_FF_CHUNKS):
        conv = []
        for part in range(2):
            cols = slice(part * D_FF + j * FF_CHUNK, part * D_FF + (j + 1) * FF_CHUNK)
            u = jnp.dot(xh, wi_ref[:, cols], preferred_element_type=F32)
            conv.append(_dwconv3_interleaved(u, cw_ref[:, cols], cb_ref[:, cols]))
        gate = conv[1]
        gelu = 0.5 * gate * (1.0 + lax.erf(gate * (2.0 ** -0.5)))
        hid_ref[:, j * FF_CHUNK:(j + 1) * FF_CHUNK] = (conv[0] * gelu).astype(BF16)
    y = jnp.dot(hid_ref[...], wo_ref[...], preferred_element_type=F32)
    out = _layer_norm(ALPHA * _load_interleaved(xs_ref) + y, g_ref[...], b_ref[...])
    _store_natural(os_ref, 0, out)
    for k in range(os_ref.shape[0]):
        o_ref[0, :, k * LANES:(k + 1) * LANES] = os_ref[k, HALO:HALO + tl, :]


def _ffn(x1, w_in, conv_w, conv_b, w_out, g, b, tl):
    bsz, seq, d = x1.shape
    cb = conv_b.reshape(1, -1)
    return pl.pallas_call(
        functools.partial(_ffn_kernel, tl=tl),
        out_shape=jax.ShapeDtypeStruct(x1.shape, F32),
        grid=(bsz, seq // tl),
        in_specs=_halo_specs(tl, seq, d) + [_resident(w_in.shape), _resident(conv_w.shape), _resident(cb.shape),
                                            _resident(w_out.shape), _resident(g.shape), _resident(b.shape)],
        out_specs=pl.BlockSpec((1, tl, d), lambda bb, i: (bb, i, 0)),
        scratch_shapes=[pltpu.VMEM((d // LANES, tl + 2 * HALO, LANES), F32),
                        pltpu.VMEM((tl + 2 * HALO, D_FF), BF16),
                        pltpu.VMEM((d // LANES, tl + 2 * HALO, LANES), F32)],
        compiler_params=_params(("parallel", "parallel")),
        name="conv_ffn",
    )(x1, x1, x1, w_in, conv_w, cb, w_out, g, b)


def _encoder_layer(x, p, tl=1024, rb=16):
    seq = x.shape[1]
    q, k, v, zg, g0 = _in_proj(x, p["w_in"], p["short_w"], p["short_b"], tl)
    ma = _attention(q, k, v, p["bias"], p["g_attn"], rb)
    a, d = _filter_taps(seq, p["filt_w1"], p["filt_b1"], p["filt_freq"], p["filt_w_inner"],
                        p["filt_b_inner"], p["filt_w3"])
    cr, sr = _pair_coefficients(seq)
    t, tt = _dft_matrices(seq)
    gspec = _filter_spectrum(cr, sr, t, a, d)
    mh = _hyena_conv(zg, g0, cr, sr, t, tt, gspec, p["filt_bias"], p["g_hyena"])
    x1 = _merge(ma, mh, x, p["w_out"], p["ln1_g"], p["ln1_b"], tl)
    return _ffn(x1, p["ffn_w_in"], p["ffn_conv_w"], p["ffn_conv_b"], p["ffn_w_out"],
                p["ln2_g"], p["ln2_b"], tl)


def kernel(x_prompt, x_sample, w_in, short_w, short_b, rpb, filt_w1, filt_b1, filt_freq, filt_w_inner,
           filt_b_inner, filt_w3, filt_bias, g_attn, g_hyena, w_out, ln1_g, ln1_b, ffn_w_in, ffn_conv_w,
           ffn_conv_b, ffn_w_out, ln2_g, ln2_b):
    assert w_in.shape[0] == DEPTH == 1
    row = lambda a: a[0].reshape(1, -1)
    p = dict(
        w_in=w_in[0].astype(BF16), short_w=short_w[0], short_b=short_b[0],
        bias=_attention_bias(rpb[0]), g_attn=row(g_attn),
        filt_w1=filt_w1[0], filt_b1=filt_b1[0], filt_freq=filt_freq[0], filt_w_inner=filt_w_inner[0],
        filt_b_inner=filt_b_inner[0], filt_w3=filt_w3[0], filt_bias=filt_bias[0], g_hyena=row(g_hyena),
        w_out=w_out[0].astype(BF16), ln1_g=row(ln1_g), ln1_b=row(ln1_b),
        ffn_w_in=ffn_w_in[0].astype(BF16), ffn_conv_w=ffn_conv_w[0], ffn_conv_b=ffn_conv_b[0],
        ffn_w_out=ffn_w_out[0].astype(BF16), ln2_g=row(ln2_g), ln2_b=row(ln2_b),
    )
    return (_encoder_layer(x_prompt, p), _encoder_layer(x_sample, p))
```

```python
import functools
import math

import numpy as np
import jax
import jax.numpy as jnp
from jax import lax
from jax.experimental import pallas as pl
from jax.experimental.pallas import tpu as pltpu

F32 = jnp.float32
BF16 = jnp.bfloat16

D_MODEL = 1024
GRID_W = 64
ATTN_WIDTH = 512
HYENA_WIDTH = 512
HEAD_DIM = 64
N_HEADS = ATTN_WIDTH // HEAD_DIM
NA_ROWS = 8
NA_COLS = 16
FILTER_EMB = 33
FILTER_BANDS = (FILTER_EMB - 1) // 2
FILTER_HIDDEN = 64
FILTER_INNER = 2
MAX_DECAY = math.log(1e-2) / 0.3
MIN_DECAY = math.log(1e-2) / 1.5
D_FF = 2816
DEPTH = 1
ALPHA = (2 * DEPTH) ** 0.25
LN_EPS = 1e-5
RMS_EPS = 1e-6
LOG2E = math.log2(math.e)

SUBLANES = 8
LANES = 128
HALO = SUBLANES
VMEM_LIMIT = 56 * 1024 * 1024

HIGHEST = lax.Precision.HIGHEST


def _params(sem):
    return pltpu.CompilerParams(dimension_semantics=sem, vmem_limit_bytes=VMEM_LIMIT)


def _resident(shape):
    nd = len(shape)
    return pl.BlockSpec(shape, lambda *_: (0,) * nd, pipeline_mode=pl.Buffered(1))


def _layer_norm(y, g, b):
    mu = jnp.mean(y, axis=-1, keepdims=True)
    yc = y - mu
    var = jnp.mean(yc * yc, axis=-1, keepdims=True)
    return yc * lax.rsqrt(var + LN_EPS) * g + b


def _rms_norm(y, g):
    ms = jnp.mean(y * y, axis=-1, keepdims=True)
    return y * lax.rsqrt(ms + RMS_EPS) * g


def _fill_halo_slabs(xs_ref, xp_ref, x_ref, xn_ref, tl):
    i = pl.program_id(1)
    last = pl.num_programs(1) - 1
    prev = jnp.where(i > 0, xp_ref[0], 0.0)
    nxt = jnp.where(i < last, xn_ref[0], 0.0)
    for k in range(xs_ref.shape[0]):
        sl = slice(k * LANES, (k + 1) * LANES)
        xs_ref[k, 0:HALO, :] = prev[:, sl]
        xs_ref[k, HALO:HALO + tl, :] = x_ref[0, :, sl]
        xs_ref[k, HALO + tl:HALO + tl + HALO, :] = nxt[:, sl]


def _load_interleaved(xs_ref, groups=None):
    nslab, rows, _ = xs_ref.shape
    nv = rows // SUBLANES
    assert rows % SUBLANES == 0 and nv % SUBLANES != 0
    groups = range(nv) if groups is None else groups
    return jnp.concatenate(
        [jnp.concatenate([xs_ref[k, pl.ds(j, SUBLANES, stride=nv), :] for j in groups], axis=0)
         for k in range(nslab)], axis=1)


def _store_natural(os_ref, first, val, groups=None):
    nv = os_ref.shape[1] // SUBLANES
    groups = range(nv) if groups is None else groups
    for k in range(val.shape[1] // LANES):
        for n, j in enumerate(groups):
            os_ref[first + k, pl.ds(j, SUBLANES, stride=nv), :] = val[n * SUBLANES:(n + 1) * SUBLANES,
                                                                      k * LANES:(k + 1) * LANES]


def _dwconv3_interleaved(u, w, b):
    head = pltpu.roll(u[-SUBLANES:], 1, axis=0)
    tail = pltpu.roll(u[:SUBLANES], SUBLANES - 1, axis=0)
    up = jnp.concatenate([head, u[:-SUBLANES]], axis=0)
    un = jnp.concatenate([u[SUBLANES:], tail], axis=0)
    return up * w[0:1] + u * w[1:2] + un * w[2:3] + b


def _halo_specs(tl, seq, d):
    nb = tl // HALO
    last = seq // HALO - 1
    return [
        pl.BlockSpec((1, HALO, d), lambda b, i: (b, jnp.maximum(i * nb - 1, 0), 0)),
        pl.BlockSpec((1, tl, d), lambda b, i: (b, i, 0)),
        pl.BlockSpec((1, HALO, d), lambda b, i: (b, jnp.minimum((i + 1) * nb, last), 0)),
    ]


def _in_proj_kernel(xp_ref, x_ref, xn_ref, w_ref, sw_ref, sb_ref,
                    q_ref, k_ref, v_ref, zg_ref, g0_ref, xs_ref, os_ref, *, tl):
    xq = x_ref[0].astype(BF16)
    proj = lambda n: jnp.dot(xq, w_ref[:, n * ATTN_WIDTH:(n + 1) * ATTN_WIDTH], preferred_element_type=F32)
    q_ref[0] = (proj(0) * (HEAD_DIM ** -0.5 * LOG2E)).astype(BF16)
    k_ref[0] = proj(1).astype(BF16)
    v = proj(2).astype(BF16)
    ones = jnp.ones((tl, LANES), BF16)
    v_ref[0] = jnp.concatenate([piece for p in range(N_HEADS // 2)
                                for piece in (v[:, p * LANES:(p + 1) * LANES], ones)], axis=1)
    _fill_halo_slabs(xs_ref, xp_ref, x_ref, xn_ref, tl)
    xh = _load_interleaved(xs_ref).astype(BF16)
    conv = []
    for n in range(3):
        lo = 3 * ATTN_WIDTH + n * HYENA_WIDTH
        u = jnp.dot(xh, w_ref[:, lo:lo + HYENA_WIDTH], preferred_element_type=F32)
        conv.append(_dwconv3_interleaved(u, sw_ref[n], sb_ref[n]))
    nslab = HYENA_WIDTH // LANES
    for first, ref, val in ((0, g0_ref, conv[0]), (nslab, zg_ref, conv[2] * conv[1])):
        _store_natural(os_ref, first, val)
        for k in range(nslab):
            ref[0, :, k * LANES:(k + 1) * LANES] = os_ref[first + k, HALO:HALO + tl, :].astype(BF16)


def _in_proj(x, w_in, short_w, short_b, tl):
    bsz, seq, d = x.shape
    sw = short_w.reshape(3, 3, HYENA_WIDTH).transpose(1, 0, 2)
    sb = short_b.reshape(3, 1, HYENA_WIDTH)
    out = jax.ShapeDtypeStruct((bsz, seq, ATTN_WIDTH), BF16)
    ospec = pl.BlockSpec((1, tl, ATTN_WIDTH), lambda b, i: (b, i, 0))
    vout = jax.ShapeDtypeStruct((bsz, seq, 2 * ATTN_WIDTH), BF16)
    vspec = pl.BlockSpec((1, tl, 2 * ATTN_WIDTH), lambda b, i: (b, i, 0))
    return pl.pallas_call(
        functools.partial(_in_proj_kernel, tl=tl),
        out_shape=(out, out, vout, out, out),
        grid=(bsz, seq // tl),
        in_specs=_halo_specs(tl, seq, d) + [_resident(w_in.shape), _resident(sw.shape), _resident(sb.shape)],
        out_specs=(ospec, ospec, vspec, ospec, ospec),
        scratch_shapes=[pltpu.VMEM((d // LANES, tl + 2 * HALO, LANES), F32),
                        pltpu.VMEM((2 * HYENA_WIDTH // LANES, tl + 2 * HALO, LANES), F32)],
        compiler_params=_params(("parallel", "parallel")),
        name="in_proj",
    )(x, x, x, w_in, sw, sb)


N_PAIRS = N_HEADS // 2
KEY_WIN = NA_ROWS * GRID_W
N_DR = 2 * NA_ROWS - 1
N_DC = 2 * NA_COLS - 1
SOFTMAX_PARTS = 4


def _bias_kernel(rpb_ref, ea_ref, eb_ref, o_ref, t_ref):
    row = lax.broadcasted_iota(jnp.int32, (GRID_W, LANES), 0)
    lane = lax.broadcasted_iota(jnp.int32, (GRID_W, LANES), 1)
    first = lane < GRID_W
    qcol = lane & (GRID_W - 1)
    col_start = jnp.clip(qcol - NA_COLS // 2, 0, GRID_W - NA_COLS)
    valid = (row >= col_start) & (row < col_start + NA_COLS)
    wa = jnp.dot(rpb_ref[0, 0], ea_ref[...], precision=HIGHEST, preferred_element_type=F32)
    wb = jnp.dot(rpb_ref[0, 1], eb_ref[...], precision=HIGHEST, preferred_element_type=F32)

    def rotate_rows(x):
        return pltpu.roll(x, 0, axis=1, stride=1, stride_axis=0)

    for dr in range(N_DR):
        ta = rotate_rows(jnp.broadcast_to(wa[dr:dr + 1], (GRID_W, LANES)))
        tb = rotate_rows(jnp.broadcast_to(wb[dr:dr + 1], (GRID_W, LANES)))
        t_ref[dr] = jnp.where(valid, jnp.where(first, ta, tb) * LOG2E, -jnp.inf)
    for cls in range(NA_ROWS):
        for i in range(NA_ROWS):
            o_ref[0, cls, i * GRID_W:(i + 1) * GRID_W, :] = t_ref[i - cls + NA_ROWS - 1]


def _toeplitz_selectors():
    m = np.arange(LANES)
    dc = np.where(m < GRID_W, (NA_COLS - 1) - np.minimum(m, NA_COLS - 1),
                  np.minimum(LANES - m, NA_COLS - 1) + (NA_COLS - 1))
    ea = np.zeros((LANES, LANES), np.float32)
    ea[dc, m] = 1.0
    ea[:, GRID_W] = 0.0
    eb = np.roll(ea, GRID_W, axis=1)
    return ea, eb


def _attention_bias(rpb):
    ea, eb = _toeplitz_selectors()
    rp = jnp.pad(rpb.astype(F32), ((0, 0), (0, 16 - N_DR), (0, LANES - N_DC))).reshape(N_PAIRS, 2, 16, LANES)
    return pl.pallas_call(
        _bias_kernel,
        out_shape=jax.ShapeDtypeStruct((N_PAIRS, NA_ROWS, KEY_WIN, LANES), F32),
        grid=(N_PAIRS,),
        in_specs=[pl.BlockSpec((1, 2, 16, LANES), lambda p: (p, 0, 0, 0)),
                  _resident(ea.shape), _resident(eb.shape)],
        out_specs=pl.BlockSpec((1, NA_ROWS, KEY_WIN, LANES), lambda p: (p, 0, 0, 0)),
        scratch_shapes=[pltpu.VMEM((N_DR, GRID_W, LANES), F32)],
        compiler_params=_params(("parallel",)),
        name="attn_bias",
    )(rp, jnp.asarray(ea), jnp.asarray(eb))


def _attn_kernel(q_ref, k_ref, vx_ref, bias_ref, g_ref, o_ref, *, rows, rb, unroll):
    jb = pl.program_id(1)
    lane = lax.broadcasted_iota(jnp.int32, (GRID_W, LANES), 1)
    first = lane < HEAD_DIM
    keep = (first.astype(F32).astype(BF16), (~first).astype(F32).astype(BF16))
    nt = (((1,), (1,)), ((), ()))
    tn = (((0,), (0,)), ((), ()))

    def score_stage(j):
        r = jb * rb + j
        start = jnp.clip(r - NA_ROWS // 2, 0, rows - NA_ROWS)
        qoff = pl.multiple_of(j * GRID_W, GRID_W)
        koff = pl.multiple_of(start * GRID_W, GRID_W)
        scores = []
        for p in range(N_PAIRS):
            sl = slice(p * LANES, (p + 1) * LANES)
            qp = q_ref[0, pl.ds(qoff, GRID_W), sl]
            qblk = jnp.concatenate([qp * keep[0], qp * keep[1]], axis=0)
            kp = k_ref[0, pl.ds(koff, KEY_WIN), sl]
            scores.append(lax.dot_general(kp, qblk, nt, preferred_element_type=F32))
        return r - start, qoff, koff, scores

    def output_stage(cls, qoff, koff, scores):
        probs = []
        step = KEY_WIN // SOFTMAX_PARTS
        for p in range(N_PAIRS):
            parts, maxes = [], []
            for lo in range(0, KEY_WIN, step):
                s = scores[p][lo:lo + step] + bias_ref[p, cls, lo:lo + step]
                m_g = jnp.max(s, axis=0, keepdims=True)
                parts.append(jnp.exp2(s - m_g))
                maxes.append(m_g)
            m = functools.reduce(jnp.maximum, maxes)
            probs.append(jnp.concatenate([e * jnp.exp2(m_g - m) for e, m_g in zip(parts, maxes)],
                                         axis=0).astype(BF16))
        outs = []
        for p in range(N_PAIRS):
            vx = vx_ref[0, pl.ds(koff, KEY_WIN), 2 * p * LANES:2 * (p + 1) * LANES]
            o = lax.dot_general(probs[p], vx, tn, preferred_element_type=F32)
            oa = o[:GRID_W, :LANES] / o[:GRID_W, LANES:]
            ob = o[GRID_W:, :LANES] / o[GRID_W:, LANES:]
            outs.append(jnp.where(first, oa, ob))
        o_ref[0, pl.ds(qoff, GRID_W), :] = _rms_norm(jnp.concatenate(outs, axis=1), g_ref[...]).astype(BF16)

    def body(jj, carry):
        state = score_stage(jj * unroll)
        for u in range(unroll):
            ahead = score_stage(jj * unroll + u + 1) if u + 1 < unroll else None
            output_stage(*state)
            state = ahead
        return carry

    lax.fori_loop(0, rb // unroll, body, 0)


def _attention(q, k, vx, bias, g_attn, rb, unroll=8):
    bsz, seq, w = q.shape
    rows = seq // GRID_W
    tq = rb * GRID_W
    return pl.pallas_call(
        functools.partial(_attn_kernel, rows=rows, rb=rb, unroll=unroll),
        out_shape=jax.ShapeDtypeStruct((bsz, seq, w), BF16),
        grid=(bsz, rows // rb),
        in_specs=[
            pl.BlockSpec((1, tq, w), lambda b, i: (b, i, 0)),
            pl.BlockSpec((1, seq, w), lambda b, i: (b, 0, 0)),
            pl.BlockSpec((1, seq, 2 * w), lambda b, i: (b, 0, 0)),
            _resident(bias.shape),
            _resident(g_attn.shape),
        ],
        out_specs=pl.BlockSpec((1, tq, w), lambda b, i: (b, i, 0)),
        compiler_params=_params(("parallel", "arbitrary")),
        name="attention",
    )(q, k, vx, bias, g_attn)


def _filter_kernel(w1t_ref, w1c_ref, w1s_ref, b1_ref, fq_ref, wi_ref, bi_ref, w3_ref,
                   a_ref, d_ref, cb_ref, sb_ref, h_ref, *, seq, tl):
    i = pl.program_id(0)
    rows = tl + 2 * SUBLANES
    half = rows // 2
    band = lax.broadcasted_iota(jnp.int32, (rows, LANES), 1)
    freqs = jnp.where(band < FILTER_BANDS,
                      1e-4 + band.astype(F32) * ((FILTER_BANDS - 1 - 1e-4) / (FILTER_BANDS - 1)), 0.0)
    rad = freqs * (2.0 * math.pi / seq)
    local = lax.broadcasted_iota(jnp.int32, (rows, LANES), 0)

    @pl.when(i == 0)
    def _():
        cb_ref[...] = jnp.cos(local.astype(F32) * rad)
        sb_ref[...] = jnp.sin(local.astype(F32) * rad)

    base = (i * tl).astype(F32) * rad[0:SUBLANES]
    ca = jnp.cos(base)[0:1]
    sa = jnp.sin(base)[0:1]
    cos_ang = ca * cb_ref[...] - sa * sb_ref[...]
    sin_ang = sa * cb_ref[...] + ca * sb_ref[...]
    t = (local + i * tl).astype(F32) * (1.0 / (seq - 1))
    fq = fq_ref[...]
    dot = functools.partial(jnp.dot, precision=HIGHEST, preferred_element_type=F32)
    side = lambda x: jnp.concatenate([x[:half], x[half:]], axis=1)
    left = lax.broadcasted_iota(jnp.int32, (half, LANES), 1) < FILTER_HIDDEN
    t_packed = jnp.where(left, t[:half], t[half:])
    pre = (t_packed * w1t_ref[...] + dot(side(cos_ang), w1c_ref[...]) - dot(side(sin_ang), w1s_ref[...])
           + b1_ref[...])
    h = jnp.sin(fq * pre)
    for n in range(FILTER_INNER):
        h = jnp.sin(fq * (dot(h, wi_ref[n]) + bi_ref[n]))
    chan = lax.broadcasted_iota(jnp.int32, (half, HYENA_WIDTH), 1).astype(F32)
    deltas = jnp.abs(MIN_DECAY + chan * ((MAX_DECAY - MIN_DECAY) / (HYENA_WIDTH - 1)))
    for part in range(2):
        rws = slice(part * half, (part + 1) * half)
        pos = lax.broadcasted_iota(jnp.int32, (half, HYENA_WIDTH), 0) + (i * tl + part * half)
        decay = jnp.exp(-(pos.astype(F32) * (1.0 / (seq - 1))) * deltas)
        for n in range(2):
            cols = slice(n * HYENA_WIDTH, (n + 1) * HYENA_WIDTH)
            h_ref[rws, cols] = dot(h, w3_ref[part, :, cols]) * decay
    h_fwd = h_ref[0:tl, 0:HYENA_WIDTH]
    lag = lax.broadcasted_iota(jnp.int32, (tl, HYENA_WIDTH), 0) + (i * tl + 1)
    h_bwd = jnp.where(lag < seq, h_ref[pl.ds(1, tl), HYENA_WIDTH:2 * HYENA_WIDTH], 0.0)
    a_ref[...] = h_fwd + h_bwd
    d_ref[...] = h_bwd - h_fwd


def _filter_taps(seq, w1, b1, freq, w_inner, b_inner, w3, tl=512):
    pad = LANES - FILTER_BANDS
    twice = lambda v: jnp.tile(v.reshape(1, -1), (1, 2))
    blockdiag = lambda w: jnp.kron(jnp.eye(2, dtype=w.dtype), w)
    w1c = blockdiag(jnp.pad(w1[1:1 + FILTER_BANDS], ((0, pad), (0, 0))))
    w1s = blockdiag(jnp.pad(w1[1 + FILTER_BANDS:], ((0, pad), (0, 0))))
    zero = jnp.zeros_like(w3)
    args = (twice(w1[0]), w1c, w1s, twice(b1), twice(freq),
            jnp.stack([blockdiag(w_inner[n]) for n in range(FILTER_INNER)]),
            jnp.stack([twice(b_inner[n]) for n in range(FILTER_INNER)]),
            jnp.stack([jnp.concatenate([w3, zero]), jnp.concatenate([zero, w3])]))
    out = jax.ShapeDtypeStruct((seq, HYENA_WIDTH), F32)
    ospec = pl.BlockSpec((tl, HYENA_WIDTH), lambda i: (i, 0))
    return pl.pallas_call(
        functools.partial(_filter_kernel, seq=seq, tl=tl),
        out_shape=(out, out),
        grid=(seq // tl,),
        in_specs=[_resident(a.shape) for a in args],
        out_specs=(ospec, ospec),
        scratch_shapes=[pltpu.VMEM((tl + 2 * SUBLANES, LANES), F32),
                        pltpu.VMEM((tl + 2 * SUBLANES, LANES), F32),
                        pltpu.VMEM((tl + 2 * SUBLANES, 2 * HYENA_WIDTH), F32)],
        compiler_params=_params(("arbitrary",)),
        name="hyena_filter",
    )(*args)


P_BLK = 512
F_HALF = P_BLK // 2


def _pair_coefficients(seq):
    s1n = seq // P_BLK
    g = np.arange(s1n)[:, None]
    s = np.arange(s1n)[None, :]
    ang = np.pi * (2 * g + 1) * s / (2 * s1n)
    return jnp.asarray(np.cos(ang), F32), jnp.asarray(np.sin(ang), F32)


def _dft_kernel(t_ref, tt_ref, cb_ref, sb_ref, *, seq):
    g = pl.program_id(0)
    s1n = seq // P_BLK

    @pl.when(g == 0)
    def _():
        f2 = lax.broadcasted_iota(jnp.int32, (F_HALF, P_BLK), 0)
        s_odd = 2 * lax.broadcasted_iota(jnp.int32, (F_HALF, P_BLK), 1) + 1
        beta = ((f2 * s_odd) & (2 * P_BLK - 1)).astype(F32) * (math.pi / P_BLK)
        cb_ref[...] = jnp.cos(beta)
        sb_ref[...] = jnp.sin(beta)

    cb = cb_ref[...]
    sb = sb_ref[...]
    s_odd = 2 * lax.broadcasted_iota(jnp.int32, (SUBLANES, P_BLK), 1) + 1

    def block(f1):
        alpha = ((s_odd * (2 * f1 + 1)) & (8 * seq - 1)).astype(F32) * (math.pi / (4 * seq))
        ca = jnp.cos(alpha)[0:1]
        sa = jnp.sin(alpha)[0:1]
        return ca * cb - sa * sb, sa * cb + ca * sb

    mra, mia = block(g)
    mrb, mib = block(2 * s1n - 1 - g)
    for r, (left, right) in enumerate(((mra, -mia), (mia, mra), (mrb, mib), (mib, -mrb))):
        rows = slice(r * F_HALF, (r + 1) * F_HALF)
        t_ref[0, rows, :P_BLK] = left.astype(BF16)
        t_ref[0, rows, P_BLK:] = right.astype(BF16)
        tt_ref[0, :P_BLK, rows] = left.T.astype(BF16)
        tt_ref[0, P_BLK:, rows] = right.T.astype(BF16)


def _dft_matrices(seq):
    s1n = seq // P_BLK
    mat = jax.ShapeDtypeStruct((s1n, 2 * P_BLK, 2 * P_BLK), BF16)
    spec = pl.BlockSpec((1, 2 * P_BLK, 2 * P_BLK), lambda g: (g, 0, 0))
    return pl.pallas_call(
        functools.partial(_dft_kernel, seq=seq),
        out_shape=(mat, mat),
        grid=(s1n,),
        out_specs=(spec, spec),
        scratch_shapes=[pltpu.VMEM((F_HALF, P_BLK), F32), pltpu.VMEM((F_HALF, P_BLK), F32)],
        compiler_params=_params(("arbitrary",)),
        name="dft_matrices",
    )()


_SMEM = pl.BlockSpec(memory_space=pltpu.SMEM)
_COL_HALVES = (slice(0, HYENA_WIDTH // 2), slice(HYENA_WIDTH // 2, HYENA_WIDTH))


ROW_CHUNK = 64


def _stacked_blocks(cr_ref, sr_ref, g, block, s1n):
    re, im = [], []
    for r in range(0, P_BLK, ROW_CHUNK):
        ar = block(0, r)
        bi = None
        for s in range(1, s1n):
            b = block(s, r)
            ar = ar + cr_ref[g, s] * b
            bi = sr_ref[g, s] * b if bi is None else bi + sr_ref[g, s] * b
        re.append(ar)
        im.append(bi)
    return jnp.concatenate(re + im, axis=0).astype(BF16)


def _spectrum_kernel(cr_ref, sr_ref, t_ref, a_ref, d_ref, g_ref, *, seq):
    g = pl.program_id(0)
    s1n = seq // P_BLK
    time_block = lambda ref: (lambda s, r: ref[s * P_BLK + r:s * P_BLK + r + ROW_CHUNK, :])
    t_re = jnp.concatenate([t_ref[0, 0:F_HALF], t_ref[0, P_BLK:P_BLK + F_HALF]], axis=0)
    t_im = jnp.concatenate([t_ref[0, F_HALF:P_BLK], t_ref[0, P_BLK + F_HALF:2 * P_BLK]], axis=0)
    ka = jnp.dot(t_re, _stacked_blocks(cr_ref, sr_ref, g, time_block(a_ref), s1n),
                 preferred_element_type=F32)
    kd = jnp.dot(t_im, _stacked_blocks(cr_ref, sr_ref, g, time_block(d_ref), s1n),
                 preferred_element_type=F32)
    f2 = lax.broadcasted_iota(jnp.int32, (F_HALF, LANES), 0)
    reps = HYENA_WIDTH // LANES
    for blk, f1 in ((0, g), (1, 2 * s1n - 1 - g)):
        lo = blk * P_BLK
        phi = (2 * (f1 + 2 * s1n * f2) + 1).astype(F32) * (math.pi / (4 * seq))
        c = pltpu.repeat(jnp.cos(phi), reps, axis=1) * (1.0 / seq)
        s = pltpu.repeat(jnp.sin(phi), reps, axis=1) * (1.0 / seq)
        for r in range(0, F_HALF, ROW_CHUNK):
            rows = slice(blk * F_HALF + r, blk * F_HALF + r + ROW_CHUNK)
            cc, ss = c[r:r + ROW_CHUNK], s[r:r + ROW_CHUNK]
            g_ref[0, lo + r:lo + r + ROW_CHUNK] = cc * ka[rows] - ss * kd[rows]
            g_ref[0, lo + F_HALF + r:lo + F_HALF + r + ROW_CHUNK] = cc * kd[rows] + ss * ka[rows]


def _filter_spectrum(cr, sr, t, a, d):
    seq = a.shape[0]
    s1n = seq // P_BLK
    return pl.pallas_call(
        functools.partial(_spectrum_kernel, seq=seq),
        out_shape=jax.ShapeDtypeStruct((s1n, 2 * P_BLK, HYENA_WIDTH), F32),
        grid=(s1n,),
        in_specs=[_SMEM, _SMEM, pl.BlockSpec((1, 2 * P_BLK, 2 * P_BLK), lambda g: (g, 0, 0)),
                  _resident(a.shape), _resident(d.shape)],
        out_specs=pl.BlockSpec((1, 2 * P_BLK, HYENA_WIDTH), lambda g: (g, 0, 0)),
        compiler_params=_params(("parallel",)),
        name="hyena_spectrum",
    )(cr, sr, t, a, d)


def _hy_fwd_kernel(cr_ref, sr_ref, t_ref, g_ref, z_ref, y_ref, *, s1n, nb):
    g = pl.program_id(0)
    for bi in range(nb):
        for cols in _COL_HALVES:
            x = _stacked_blocks(
                cr_ref, sr_ref, g,
                lambda s, r: z_ref[bi, s * P_BLK + r:s * P_BLK + r + ROW_CHUNK, cols].astype(F32), s1n)
            res = jnp.dot(t_ref[0], x, preferred_element_type=F32)
            for lo in range(0, 2 * P_BLK, P_BLK):
                for r in range(lo, lo + F_HALF, ROW_CHUNK):
                    re_rows = slice(r, r + ROW_CHUNK)
                    im_rows = slice(r + F_HALF, r + F_HALF + ROW_CHUNK)
                    zr, wi = res[re_rows], res[im_rows]
                    gr, gi = g_ref[0, re_rows, cols], g_ref[0, im_rows, cols]
                    y_ref[bi, 0, re_rows, cols] = (gr * zr + gi * wi).astype(BF16)
                    y_ref[bi, 0, im_rows, cols] = (gr * wi - gi * zr).astype(BF16)


def _hy_inv_kernel(cr_ref, sr_ref, tt_ref, y_ref, z_ref, g0_ref, fb_ref, gn_ref, o_ref,
                   acc_ref, zs_ref, g0s_ref, *, s1n, nb):
    g = pl.program_id(1)

    @pl.when(g == 0)
    def _():
        acc_ref[...] = jnp.zeros_like(acc_ref)

    tblk = pl.ds(pl.multiple_of(g * P_BLK, P_BLK), P_BLK)
    zs_ref[:, tblk, :] = z_ref[...]
    g0s_ref[:, tblk, :] = g0_ref[...]
    for bi in range(nb):
        for cols in _COL_HALVES:
            uv = jnp.dot(tt_ref[g], y_ref[bi, 0, :, cols], preferred_element_type=F32)
            for r in range(0, P_BLK, ROW_CHUNK):
                u = uv[r:r + ROW_CHUNK]
                vn = uv[P_BLK + r:P_BLK + r + ROW_CHUNK]
                acc_ref[bi, r:r + ROW_CHUNK, cols] += u
                for t1 in range(1, s1n):
                    rows = slice(t1 * P_BLK + r, t1 * P_BLK + r + ROW_CHUNK)
                    acc_ref[bi, rows, cols] += cr_ref[g, t1] * u + sr_ref[g, t1] * vn

    @pl.when(g == s1n - 1)
    def _():
        for bi in range(nb):
            for r in range(0, s1n * P_BLK, ROW_CHUNK):
                rows = slice(r, r + ROW_CHUNK)
                y = ((acc_ref[bi, rows, :] + zs_ref[bi, rows, :].astype(F32) * fb_ref[...])
                     * g0s_ref[bi, rows, :].astype(F32))
                o_ref[bi, rows, :] = _rms_norm(y, gn_ref[...]).astype(BF16)


_INV_STATE_BUDGET = 28 * 1024 * 1024


def _hyena_conv(zg, g0, cr, sr, t, tt, gspec, fbias, g_hyena):
    bsz, seq, w = zg.shape
    s1n = seq // P_BLK
    nb = max(n for n in (4, 2, 1) if bsz % n == 0)
    mat = pl.BlockSpec((1, 2 * P_BLK, 2 * P_BLK), lambda g, b: (g, 0, 0))
    spec = pl.BlockSpec((1, 2 * P_BLK, w), lambda g, b: (g, 0, 0))
    y = pl.pallas_call(
        functools.partial(_hy_fwd_kernel, s1n=s1n, nb=nb),
        out_shape=jax.ShapeDtypeStruct((bsz, s1n, 2 * P_BLK, w), BF16),
        grid=(s1n, bsz // nb),
        in_specs=[_SMEM, _SMEM, mat, spec, pl.BlockSpec((nb, seq, w), lambda g, b: (b, 0, 0))],
        out_specs=pl.BlockSpec((nb, 1, 2 * P_BLK, w), lambda g, b: (b, g, 0, 0)),
        compiler_params=_params(("parallel", "parallel")),
        name="hyena_fwd",
    )(cr, sr, t, gspec, zg)
    state_bytes = seq * w * (4 + 2 + 2 + 2 * 2)
    nb = 2 if bsz % 2 == 0 and 2 * state_bytes <= _INV_STATE_BUDGET else 1
    blk = pl.BlockSpec((nb, P_BLK, w), lambda b, g: (b, g, 0))
    return pl.pallas_call(
        functools.partial(_hy_inv_kernel, s1n=s1n, nb=nb),
        out_shape=jax.ShapeDtypeStruct((bsz, seq, w), BF16),
        grid=(bsz // nb, s1n),
        in_specs=[_SMEM, _SMEM, _resident(tt.shape),
                  pl.BlockSpec((nb, 1, 2 * P_BLK, w), lambda b, g: (b, g, 0, 0)),
                  blk, blk, _resident(fbias.shape), _resident(g_hyena.shape)],
        out_specs=pl.BlockSpec((nb, seq, w), lambda b, g: (b, 0, 0)),
        scratch_shapes=[pltpu.VMEM((nb, seq, w), F32), pltpu.VMEM((nb, seq, w), BF16),
                        pltpu.VMEM((nb, seq, w), BF16)],
        compiler_params=_params(("parallel", "arbitrary")),
        name="hyena_inv",
    )(cr, sr, tt, y, zg, g0, fbias, g_hyena)


MERGE_ROW_CHUNKS = 4


def _merge_kernel(ma_ref, mh_ref, x_ref, wa_ref, wh_ref, g_ref, b_ref, o_ref):
    tl = x_ref.shape[1]
    chunk = tl // MERGE_ROW_CHUNKS
    for lo in range(0, tl, chunk):
        rows = slice(lo, lo + chunk)
        y = (jnp.dot(ma_ref[0, rows], wa_ref[...], preferred_element_type=F32)
             + jnp.dot(mh_ref[0, rows], wh_ref[...], preferred_element_type=F32))
        o_ref[0, rows] = _layer_norm(ALPHA * x_ref[0, rows] + y, g_ref[...], b_ref[...])


def _merge(ma, mh, x, w_out, g, b, tl):
    bsz, seq, d = x.shape
    wa, wh = w_out[:ATTN_WIDTH], w_out[ATTN_WIDTH:]
    half = pl.BlockSpec((1, tl, ATTN_WIDTH), lambda bb, i: (bb, i, 0))
    full = pl.BlockSpec((1, tl, d), lambda bb, i: (bb, i, 0))
    return pl.pallas_call(
        _merge_kernel,
        out_shape=jax.ShapeDtypeStruct(x.shape, F32),
        grid=(bsz, seq // tl),
        in_specs=[half, half, full, _resident(wa.shape), _resident(wh.shape),
                  _resident(g.shape), _resident(b.shape)],
        out_specs=full,
        compiler_params=_params(("parallel", "parallel")),
        name="merge_ln1",
    )(ma, mh, x, wa, wh, g, b)


FF_CHUNK = 256
N_FF_CHUNKS = D_FF // FF_CHUNK
FFN_OUT_PARTS = 1


def _ffn_kernel(xp_ref, x_ref, xn_ref, wi_ref, cw_ref, cb_ref, wo_ref, g_ref, b_ref,
                o_ref, xs_ref, hid_ref, os_ref, *, tl):
    _fill_halo_slabs(xs_ref, xp_ref, x_ref, xn_ref, tl)
    xh = _load_interleaved(xs_ref).astype(BF16)
    for j in range(N_FF_CHUNKS):
        conv = []
        for part in range(2):
            cols = slice(part * D_FF + j * FF_CHUNK, part * D_FF + (j + 1) * FF_CHUNK)
            u = jnp.dot(xh, wi_ref[:, cols], preferred_element_type=F32)
            conv.append(_dwconv3_interleaved(u, cw_ref[:, cols], cb_ref[:, cols]))
        gate = conv[1]
        gelu = 0.5 * gate * (1.0 + lax.erf(gate * (2.0 ** -0.5)))
        hid_ref[:, j * FF_CHUNK:(j + 1) * FF_CHUNK] = (conv[0] * gelu).astype(BF16)
    nv = xs_ref.shape[1] // SUBLANES
    for lo in range(0, nv, nv // FFN_OUT_PARTS):
        groups = range(lo, lo + nv // FFN_OUT_PARTS)
        rws = slice(lo * SUBLANES, (lo + nv // FFN_OUT_PARTS) * SUBLANES)
        y = jnp.dot(hid_ref[rws, :], wo_ref[...], preferred_element_type=F32)
        out = _layer_norm(ALPHA * _load_interleaved(xs_ref, groups) + y, g_ref[...], b_ref[...])
        _store_natural(os_ref, 0, out, groups)
    for k in range(os_ref.shape[0]):
        o_ref[0, :, k * LANES:(k + 1) * LANES] = os_ref[k, HALO:HALO + tl, :]


def _ffn(x1, w_in, conv_w, conv_b, w_out, g, b, tl):
    bsz, seq, d = x1.shape
    cb = conv_b.reshape(1, -1)
    return pl.pallas_call(
        functools.partial(_ffn_kernel, tl=tl),
        out_shape=jax.ShapeDtypeStruct(x1.shape, F32),
        grid=(bsz, seq // tl),
        in_specs=_halo_specs(tl, seq, d) + [_resident(w_in.shape), _resident(conv_w.shape), _resident(cb.shape),
                                            _resident(w_out.shape), _resident(g.shape), _resident(b.shape)],
        out_specs=pl.BlockSpec((1, tl, d), lambda bb, i: (bb, i, 0)),
        scratch_shapes=[pltpu.VMEM((d // LANES, tl + 2 * HALO, LANES), F32),
                        pltpu.VMEM((tl + 2 * HALO, D_FF), BF16),
                        pltpu.VMEM((d // LANES, tl + 2 * HALO, LANES), F32)],
        compiler_params=_params(("parallel", "parallel")),
        name="conv_ffn",
    )(x1, x1, x1, w_in, conv_w, cb, w_out, g, b)


def _encoder_layer(x, p, tl=1024, rb=16):
    seq = x.shape[1]
    q, k, v, zg, g0 = _in_proj(x, p["w_in"], p["short_w"], p["short_b"], tl)
    ma = _attention(q, k, v, p["bias"], p["g_attn"], rb)
    a, d = _filter_taps(seq, p["filt_w1"], p["filt_b1"], p["filt_freq"], p["filt_w_inner"],
                        p["filt_b_inner"], p["filt_w3"])
    cr, sr = _pair_coefficients(seq)
    t, tt = _dft_matrices(seq)
    gspec = _filter_spectrum(cr, sr, t, a, d)
    mh = _hyena_conv(zg, g0, cr, sr, t, tt, gspec, p["filt_bias"], p["g_hyena"])
    x1 = _merge(ma, mh, x, p["w_out"], p["ln1_g"], p["ln1_b"], tl)
    return _ffn(x1, p["ffn_w_in"], p["ffn_conv_w"], p["ffn_conv_b"], p["ffn_w_out"],
                p["ln2_g"], p["ln2_b"], tl)


def kernel(x_prompt, x_sample, w_in, short_w, short_b, rpb, filt_w1, filt_b1, filt_freq, filt_w_inner,
           filt_b_inner, filt_w3, filt_bias, g_attn, g_hyena, w_out, ln1_g, ln1_b, ffn_w_in, ffn_conv_w,
           ffn_conv_b, ffn_w_out, ln2_g, ln2_b):
    assert w_in.shape[0] == DEPTH == 1
    row = lambda a: a[0].reshape(1, -1)
    p = dict(
        w_in=w_in[0].astype(BF16), short_w=short_w[0], short_b=short_b[0],
        bias=_attention_bias(rpb[0]), g_attn=row(g_attn),
        filt_w1=filt_w1[0], filt_b1=filt_b1[0], filt_freq=filt_freq[0], filt_w_inner=filt_w_inner[0],
        filt_b_inner=filt_b_inner[0], filt_w3=filt_w3[0], filt_bias=filt_bias[0], g_hyena=row(g_hyena),
        w_out=w_out[0].astype(BF16), ln1_g=row(ln1_g), ln1_b=row(ln1_b),
        ffn_w_in=ffn_w_in[0].astype(BF16), ffn_conv_w=ffn_conv_w[0], ffn_conv_b=ffn_conv_b[0],
        ffn_w_out=ffn_w_out[0].astype(BF16), ln2_g=row(ln2_g), ln2_b=row(ln2_b),
    )
    return (_encoder_layer(x_prompt, p), _encoder_layer(x_sample, p))
```

```python
import functools
import math

import numpy as np
import jax
import jax.numpy as jnp
from jax import lax
from jax.experimental import pallas as pl
from jax.experimental.pallas import tpu as pltpu

F32 = jnp.float32
BF16 = jnp.bfloat16

D_MODEL = 1024
GRID_W = 64
ATTN_WIDTH = 512
HYENA_WIDTH = 512
HEAD_DIM = 64
N_HEADS = ATTN_WIDTH // HEAD_DIM
NA_ROWS = 8
NA_COLS = 16
FILTER_EMB = 33
FILTER_BANDS = (FILTER_EMB - 1) // 2
FILTER_HIDDEN = 64
FILTER_INNER = 2
MAX_DECAY = math.log(1e-2) / 0.3
MIN_DECAY = math.log(1e-2) / 1.5
D_FF = 2816
DEPTH = 1
ALPHA = (2 * DEPTH) ** 0.25
LN_EPS = 1e-5
RMS_EPS = 1e-6
LOG2E = math.log2(math.e)

SUBLANES = 8
LANES = 128
HALO = SUBLANES
VMEM_LIMIT = 56 * 1024 * 1024

HIGHEST = lax.Precision.HIGHEST


def _params(sem):
    return pltpu.CompilerParams(dimension_semantics=sem, vmem_limit_bytes=VMEM_LIMIT)


def _resident(shape):
    nd = len(shape)
    return pl.BlockSpec(shape, lambda *_: (0,) * nd, pipeline_mode=pl.Buffered(1))


def _layer_norm(y, g, b):
    mu = jnp.mean(y, axis=-1, keepdims=True)
    yc = y - mu
    var = jnp.mean(yc * yc, axis=-1, keepdims=True)
    return yc * lax.rsqrt(var + LN_EPS) * g + b


def _rms_norm(y, g):
    ms = jnp.mean(y * y, axis=-1, keepdims=True)
    return y * lax.rsqrt(ms + RMS_EPS) * g


def _fill_halo_slabs(xs_ref, xp_ref, x_ref, xn_ref, tl):
    i = pl.program_id(1)
    last = pl.num_programs(1) - 1
    prev = jnp.where(i > 0, xp_ref[0], 0.0)
    nxt = jnp.where(i < last, xn_ref[0], 0.0)
    for k in range(xs_ref.shape[0]):
        sl = slice(k * LANES, (k + 1) * LANES)
        xs_ref[k, 0:HALO, :] = prev[:, sl]
        xs_ref[k, HALO:HALO + tl, :] = x_ref[0, :, sl]
        xs_ref[k, HALO + tl:HALO + tl + HALO, :] = nxt[:, sl]


def _load_interleaved(xs_ref, groups=None):
    nslab, rows, _ = xs_ref.shape
    nv = rows // SUBLANES
    assert rows % SUBLANES == 0 and nv % SUBLANES != 0
    groups = range(nv) if groups is None else groups
    return jnp.concatenate(
        [jnp.concatenate([xs_ref[k, pl.ds(j, SUBLANES, stride=nv), :] for j in groups], axis=0)
         for k in range(nslab)], axis=1)


def _store_natural(os_ref, first, val, groups=None):
    nv = os_ref.shape[1] // SUBLANES
    groups = range(nv) if groups is None else groups
    for k in range(val.shape[1] // LANES):
        for n, j in enumerate(groups):
            os_ref[first + k, pl.ds(j, SUBLANES, stride=nv), :] = val[n * SUBLANES:(n + 1) * SUBLANES,
                                                                      k * LANES:(k + 1) * LANES]


def _dwconv3_interleaved(u, w, b):
    head = pltpu.roll(u[-SUBLANES:], 1, axis=0)
    tail = pltpu.roll(u[:SUBLANES], SUBLANES - 1, axis=0)
    up = jnp.concatenate([head, u[:-SUBLANES]], axis=0)
    un = jnp.concatenate([u[SUBLANES:], tail], axis=0)
    return up * w[0:1] + u * w[1:2] + un * w[2:3] + b


def _halo_specs(tl, seq, d):
    nb = tl // HALO
    last = seq // HALO - 1
    return [
        pl.BlockSpec((1, HALO, d), lambda b, i: (b, jnp.maximum(i * nb - 1, 0), 0)),
        pl.BlockSpec((1, tl, d), lambda b, i: (b, i, 0)),
        pl.BlockSpec((1, HALO, d), lambda b, i: (b, jnp.minimum((i + 1) * nb, last), 0)),
    ]


def _in_proj_kernel(xp_ref, x_ref, xn_ref, w_ref, sw_ref, sb_ref,
                    q_ref, k_ref, v_ref, zg_ref, g0_ref, xs_ref, os_ref, *, tl):
    xq = x_ref[0].astype(BF16)
    proj = lambda n: jnp.dot(xq, w_ref[:, n * ATTN_WIDTH:(n + 1) * ATTN_WIDTH], preferred_element_type=F32)
    q_ref[0] = (proj(0) * (HEAD_DIM ** -0.5 * LOG2E)).astype(BF16)
    k_ref[0] = proj(1).astype(BF16)
    v = proj(2).astype(BF16)
    ones = jnp.ones((tl, LANES), BF16)
    v_ref[0] = jnp.concatenate([piece for p in range(N_HEADS // 2)
                                for piece in (v[:, p * LANES:(p + 1) * LANES], ones)], axis=1)
    _fill_halo_slabs(xs_ref, xp_ref, x_ref, xn_ref, tl)
    xh = _load_interleaved(xs_ref).astype(BF16)
    conv = []
    for n in range(3):
        lo = 3 * ATTN_WIDTH + n * HYENA_WIDTH
        u = jnp.dot(xh, w_ref[:, lo:lo + HYENA_WIDTH], preferred_element_type=F32)
        conv.append(_dwconv3_interleaved(u, sw_ref[n], sb_ref[n]))
    nslab = HYENA_WIDTH // LANES
    for first, ref, val in ((0, g0_ref, conv[0]), (nslab, zg_ref, conv[2] * conv[1])):
        _store_natural(os_ref, first, val)
        for k in range(nslab):
            ref[0, :, k * LANES:(k + 1) * LANES] = os_ref[first + k, HALO:HALO + tl, :].astype(BF16)


def _in_proj(x, w_in, short_w, short_b, tl):
    bsz, seq, d = x.shape
    sw = short_w.reshape(3, 3, HYENA_WIDTH).transpose(1, 0, 2)
    sb = short_b.reshape(3, 1, HYENA_WIDTH)
    out = jax.ShapeDtypeStruct((bsz, seq, ATTN_WIDTH), BF16)
    ospec = pl.BlockSpec((1, tl, ATTN_WIDTH), lambda b, i: (b, i, 0))
    vout = jax.ShapeDtypeStruct((bsz, seq, 2 * ATTN_WIDTH), BF16)
    vspec = pl.BlockSpec((1, tl, 2 * ATTN_WIDTH), lambda b, i: (b, i, 0))
    return pl.pallas_call(
        functools.partial(_in_proj_kernel, tl=tl),
        out_shape=(out, out, vout, out, out),
        grid=(bsz, seq // tl),
        in_specs=_halo_specs(tl, seq, d) + [_resident(w_in.shape), _resident(sw.shape), _resident(sb.shape)],
        out_specs=(ospec, ospec, vspec, ospec, ospec),
        scratch_shapes=[pltpu.VMEM((d // LANES, tl + 2 * HALO, LANES), F32),
                        pltpu.VMEM((2 * HYENA_WIDTH // LANES, tl + 2 * HALO, LANES), F32)],
        compiler_params=_params(("parallel", "parallel")),
        name="in_proj",
    )(x, x, x, w_in, sw, sb)


N_PAIRS = N_HEADS // 2
KEY_WIN = NA_ROWS * GRID_W
N_DR = 2 * NA_ROWS - 1
N_DC = 2 * NA_COLS - 1
SOFTMAX_PARTS = 4


def _bias_kernel(rpb_ref, ea_ref, eb_ref, o_ref, t_ref):
    row = lax.broadcasted_iota(jnp.int32, (GRID_W, LANES), 0)
    lane = lax.broadcasted_iota(jnp.int32, (GRID_W, LANES), 1)
    first = lane < GRID_W
    qcol = lane & (GRID_W - 1)
    col_start = jnp.clip(qcol - NA_COLS // 2, 0, GRID_W - NA_COLS)
    valid = (row >= col_start) & (row < col_start + NA_COLS)
    wa = jnp.dot(rpb_ref[0, 0], ea_ref[...], precision=HIGHEST, preferred_element_type=F32)
    wb = jnp.dot(rpb_ref[0, 1], eb_ref[...], precision=HIGHEST, preferred_element_type=F32)

    def rotate_rows(x):
        return pltpu.roll(x, 0, axis=1, stride=1, stride_axis=0)

    for dr in range(N_DR):
        ta = rotate_rows(jnp.broadcast_to(wa[dr:dr + 1], (GRID_W, LANES)))
        tb = rotate_rows(jnp.broadcast_to(wb[dr:dr + 1], (GRID_W, LANES)))
        t_ref[dr] = jnp.where(valid, jnp.where(first, ta, tb) * LOG2E, -jnp.inf)
    for cls in range(NA_ROWS):
        for i in range(NA_ROWS):
            o_ref[0, cls, i * GRID_W:(i + 1) * GRID_W, :] = t_ref[i - cls + NA_ROWS - 1]


def _toeplitz_selectors():
    m = np.arange(LANES)
    dc = np.where(m < GRID_W, (NA_COLS - 1) - np.minimum(m, NA_COLS - 1),
                  np.minimum(LANES - m, NA_COLS - 1) + (NA_COLS - 1))
    ea = np.zeros((LANES, LANES), np.float32)
    ea[dc, m] = 1.0
    ea[:, GRID_W] = 0.0
    eb = np.roll(ea, GRID_W, axis=1)
    return ea, eb


def _attention_bias(rpb):
    ea, eb = _toeplitz_selectors()
    rp = jnp.pad(rpb.astype(F32), ((0, 0), (0, 16 - N_DR), (0, LANES - N_DC))).reshape(N_PAIRS, 2, 16, LANES)
    return pl.pallas_call(
        _bias_kernel,
        out_shape=jax.ShapeDtypeStruct((N_PAIRS, NA_ROWS, KEY_WIN, LANES), F32),
        grid=(N_PAIRS,),
        in_specs=[pl.BlockSpec((1, 2, 16, LANES), lambda p: (p, 0, 0, 0)),
                  _resident(ea.shape), _resident(eb.shape)],
        out_specs=pl.BlockSpec((1, NA_ROWS, KEY_WIN, LANES), lambda p: (p, 0, 0, 0)),
        scratch_shapes=[pltpu.VMEM((N_DR, GRID_W, LANES), F32)],
        compiler_params=_params(("parallel",)),
        name="attn_bias",
    )(rp, jnp.asarray(ea), jnp.asarray(eb))


def _attn_kernel(q_ref, k_ref, vx_ref, bias_ref, g_ref, o_ref, *, rows, rb, unroll):
    jb = pl.program_id(1)
    lane = lax.broadcasted_iota(jnp.int32, (GRID_W, LANES), 1)
    first = lane < HEAD_DIM
    keep = (first.astype(F32).astype(BF16), (~first).astype(F32).astype(BF16))
    nt = (((1,), (1,)), ((), ()))
    tn = (((0,), (0,)), ((), ()))

    def score_stage(j):
        r = jb * rb + j
        start = jnp.clip(r - NA_ROWS // 2, 0, rows - NA_ROWS)
        qoff = pl.multiple_of(j * GRID_W, GRID_W)
        koff = pl.multiple_of(start * GRID_W, GRID_W)
        scores = []
        for p in range(N_PAIRS):
            sl = slice(p * LANES, (p + 1) * LANES)
            qp = q_ref[0, pl.ds(qoff, GRID_W), sl]
            qblk = jnp.concatenate([qp * keep[0], qp * keep[1]], axis=0)
            kp = k_ref[0, pl.ds(koff, KEY_WIN), sl]
            scores.append(lax.dot_general(kp, qblk, nt, preferred_element_type=F32))
        return r - start, qoff, koff, scores

    def output_stage(cls, qoff, koff, scores):
        probs = []
        step = KEY_WIN // SOFTMAX_PARTS
        for p in range(N_PAIRS):
            parts, maxes = [], []
            for lo in range(0, KEY_WIN, step):
                s = scores[p][lo:lo + step] + bias_ref[p, cls, lo:lo + step]
                m_g = jnp.max(s, axis=0, keepdims=True)
                parts.append(jnp.exp2(s - m_g))
                maxes.append(m_g)
            m = functools.reduce(jnp.maximum, maxes)
            probs.append(jnp.concatenate([e.astype(BF16) * jnp.exp2(m_g - m).astype(BF16)
                                          for e, m_g in zip(parts, maxes)], axis=0))
        outs = []
        for p in range(N_PAIRS):
            vx = vx_ref[0, pl.ds(koff, KEY_WIN), 2 * p * LANES:2 * (p + 1) * LANES]
            o = lax.dot_general(probs[p], vx, tn, preferred_element_type=F32)
            oa = o[:GRID_W, :LANES] / o[:GRID_W, LANES:]
            ob = o[GRID_W:, :LANES] / o[GRID_W:, LANES:]
            outs.append(jnp.where(first, oa, ob))
        o_ref[0, pl.ds(qoff, GRID_W), :] = _rms_norm(jnp.concatenate(outs, axis=1), g_ref[...]).astype(BF16)

    def body(jj, carry):
        state = score_stage(jj * unroll)
        for u in range(unroll):
            ahead = score_stage(jj * unroll + u + 1) if u + 1 < unroll else None
            output_stage(*state)
            state = ahead
        return carry

    lax.fori_loop(0, rb // unroll, body, 0)


def _attention(q, k, vx, bias, g_attn, rb, unroll=8):
    bsz, seq, w = q.shape
    rows = seq // GRID_W
    tq = rb * GRID_W
    return pl.pallas_call(
        functools.partial(_attn_kernel, rows=rows, rb=rb, unroll=unroll),
        out_shape=jax.ShapeDtypeStruct((bsz, seq, w), BF16),
        grid=(bsz, rows // rb),
        in_specs=[
            pl.BlockSpec((1, tq, w), lambda b, i: (b, i, 0)),
            pl.BlockSpec((1, seq, w), lambda b, i: (b, 0, 0)),
            pl.BlockSpec((1, seq, 2 * w), lambda b, i: (b, 0, 0)),
            _resident(bias.shape),
            _resident(g_attn.shape),
        ],
        out_specs=pl.BlockSpec((1, tq, w), lambda b, i: (b, i, 0)),
        compiler_params=_params(("parallel", "arbitrary")),
        name="attention",
    )(q, k, vx, bias, g_attn)


def _filter_kernel(w1t_ref, w1c_ref, w1s_ref, b1_ref, fq_ref, wi_ref, bi_ref, w3_ref,
                   a_ref, d_ref, cb_ref, sb_ref, h_ref, *, seq, tl):
    i = pl.program_id(0)
    rows = tl + 2 * SUBLANES
    half = rows // 2
    band = lax.broadcasted_iota(jnp.int32, (rows, LANES), 1)
    freqs = jnp.where(band < FILTER_BANDS,
                      1e-4 + band.astype(F32) * ((FILTER_BANDS - 1 - 1e-4) / (FILTER_BANDS - 1)), 0.0)
    rad = freqs * (2.0 * math.pi / seq)
    local = lax.broadcasted_iota(jnp.int32, (rows, LANES), 0)

    @pl.when(i == 0)
    def _():
        cb_ref[...] = jnp.cos(local.astype(F32) * rad)
        sb_ref[...] = jnp.sin(local.astype(F32) * rad)

    base = (i * tl).astype(F32) * rad[0:SUBLANES]
    ca = jnp.cos(base)[0:1]
    sa = jnp.sin(base)[0:1]
    cos_ang = ca * cb_ref[...] - sa * sb_ref[...]
    sin_ang = sa * cb_ref[...] + ca * sb_ref[...]
    t = (local + i * tl).astype(F32) * (1.0 / (seq - 1))
    fq = fq_ref[...]
    dot = functools.partial(jnp.dot, precision=HIGHEST, preferred_element_type=F32)
    side = lambda x: jnp.concatenate([x[:half], x[half:]], axis=1)
    left = lax.broadcasted_iota(jnp.int32, (half, LANES), 1) < FILTER_HIDDEN
    t_packed = jnp.where(left, t[:half], t[half:])
    pre = (t_packed * w1t_ref[...] + dot(side(cos_ang), w1c_ref[...]) - dot(side(sin_ang), w1s_ref[...])
           + b1_ref[...])
    h = jnp.sin(fq * pre)
    for n in range(FILTER_INNER):
        h = jnp.sin(fq * (dot(h, wi_ref[n]) + bi_ref[n]))
    chan = lax.broadcasted_iota(jnp.int32, (half, HYENA_WIDTH), 1).astype(F32)
    deltas = jnp.abs(MIN_DECAY + chan * ((MAX_DECAY - MIN_DECAY) / (HYENA_WIDTH - 1)))
    for part in range(2):
        rws = slice(part * half, (part + 1) * half)
        pos = lax.broadcasted_iota(jnp.int32, (half, HYENA_WIDTH), 0) + (i * tl + part * half)
        decay = jnp.exp(-(pos.astype(F32) * (1.0 / (seq - 1))) * deltas)
        for n in range(2):
            cols = slice(n * HYENA_WIDTH, (n + 1) * HYENA_WIDTH)
            h_ref[rws, cols] = dot(h, w3_ref[part, :, cols]) * decay
    h_fwd = h_ref[0:tl, 0:HYENA_WIDTH]
    lag = lax.broadcasted_iota(jnp.int32, (tl, HYENA_WIDTH), 0) + (i * tl + 1)
    h_bwd = jnp.where(lag < seq, h_ref[pl.ds(1, tl), HYENA_WIDTH:2 * HYENA_WIDTH], 0.0)
    a_ref[...] = h_fwd + h_bwd
    d_ref[...] = h_bwd - h_fwd


def _filter_taps(seq, w1, b1, freq, w_inner, b_inner, w3, tl=512):
    pad = LANES - FILTER_BANDS
    twice = lambda v: jnp.tile(v.reshape(1, -1), (1, 2))
    blockdiag = lambda w: jnp.kron(jnp.eye(2, dtype=w.dtype), w)
    w1c = blockdiag(jnp.pad(w1[1:1 + FILTER_BANDS], ((0, pad), (0, 0))))
    w1s = blockdiag(jnp.pad(w1[1 + FILTER_BANDS:], ((0, pad), (0, 0))))
    zero = jnp.zeros_like(w3)
    args = (twice(w1[0]), w1c, w1s, twice(b1), twice(freq),
            jnp.stack([blockdiag(w_inner[n]) for n in range(FILTER_INNER)]),
            jnp.stack([twice(b_inner[n]) for n in range(FILTER_INNER)]),
            jnp.stack([jnp.concatenate([w3, zero]), jnp.concatenate([zero, w3])]))
    out = jax.ShapeDtypeStruct((seq, HYENA_WIDTH), F32)
    ospec = pl.BlockSpec((tl, HYENA_WIDTH), lambda i: (i, 0))
    return pl.pallas_call(
        functools.partial(_filter_kernel, seq=seq, tl=tl),
        out_shape=(out, out),
        grid=(seq // tl,),
        in_specs=[_resident(a.shape) for a in args],
        out_specs=(ospec, ospec),
        scratch_shapes=[pltpu.VMEM((tl + 2 * SUBLANES, LANES), F32),
                        pltpu.VMEM((tl + 2 * SUBLANES, LANES), F32),
                        pltpu.VMEM((tl + 2 * SUBLANES, 2 * HYENA_WIDTH), F32)],
        compiler_params=_params(("arbitrary",)),
        name="hyena_filter",
    )(*args)


P_BLK = 512
F_HALF = P_BLK // 2


def _pair_coefficients(seq):
    s1n = seq // P_BLK
    g = np.arange(s1n)[:, None]
    s = np.arange(s1n)[None, :]
    ang = np.pi * (2 * g + 1) * s / (2 * s1n)
    return jnp.asarray(np.cos(ang), F32), jnp.asarray(np.sin(ang), F32)


def _dft_kernel(t_ref, tt_ref, cb_ref, sb_ref, *, seq):
    g = pl.program_id(0)
    s1n = seq // P_BLK

    @pl.when(g == 0)
    def _():
        f2 = lax.broadcasted_iota(jnp.int32, (F_HALF, P_BLK), 0)
        s_odd = 2 * lax.broadcasted_iota(jnp.int32, (F_HALF, P_BLK), 1) + 1
        beta = ((f2 * s_odd) & (2 * P_BLK - 1)).astype(F32) * (math.pi / P_BLK)
        cb_ref[...] = jnp.cos(beta)
        sb_ref[...] = jnp.sin(beta)

    cb = cb_ref[...]
    sb = sb_ref[...]
    s_odd = 2 * lax.broadcasted_iota(jnp.int32, (SUBLANES, P_BLK), 1) + 1

    def block(f1):
        alpha = ((s_odd * (2 * f1 + 1)) & (8 * seq - 1)).astype(F32) * (math.pi / (4 * seq))
        ca = jnp.cos(alpha)[0:1]
        sa = jnp.sin(alpha)[0:1]
        return ca * cb - sa * sb, sa * cb + ca * sb

    mra, mia = block(g)
    mrb, mib = block(2 * s1n - 1 - g)
    for r, (left, right) in enumerate(((mra, -mia), (mia, mra), (mrb, mib), (mib, -mrb))):
        rows = slice(r * F_HALF, (r + 1) * F_HALF)
        t_ref[0, rows, :P_BLK] = left.astype(BF16)
        t_ref[0, rows, P_BLK:] = right.astype(BF16)
        tt_ref[0, :P_BLK, rows] = left.T.astype(BF16)
        tt_ref[0, P_BLK:, rows] = right.T.astype(BF16)


def _dft_matrices(seq):
    s1n = seq // P_BLK
    mat = jax.ShapeDtypeStruct((s1n, 2 * P_BLK, 2 * P_BLK), BF16)
    spec = pl.BlockSpec((1, 2 * P_BLK, 2 * P_BLK), lambda g: (g, 0, 0))
    return pl.pallas_call(
        functools.partial(_dft_kernel, seq=seq),
        out_shape=(mat, mat),
        grid=(s1n,),
        out_specs=(spec, spec),
        scratch_shapes=[pltpu.VMEM((F_HALF, P_BLK), F32), pltpu.VMEM((F_HALF, P_BLK), F32)],
        compiler_params=_params(("arbitrary",)),
        name="dft_matrices",
    )()


_SMEM = pl.BlockSpec(memory_space=pltpu.SMEM)
_COL_HALVES = (slice(0, HYENA_WIDTH // 2), slice(HYENA_WIDTH // 2, HYENA_WIDTH))


ROW_CHUNK = 64


def _stacked_blocks(cr_ref, sr_ref, g, block, s1n):
    re, im = [], []
    for r in range(0, P_BLK, ROW_CHUNK):
        ar = block(0, r)
        bi = None
        for s in range(1, s1n):
            b = block(s, r)
            ar = ar + cr_ref[g, s] * b
            bi = sr_ref[g, s] * b if bi is None else bi + sr_ref[g, s] * b
        re.append(ar)
        im.append(bi)
    return jnp.concatenate(re + im, axis=0).astype(BF16)


def _spectrum_kernel(cr_ref, sr_ref, t_ref, a_ref, d_ref, g_ref, *, seq):
    g = pl.program_id(0)
    s1n = seq // P_BLK
    time_block = lambda ref: (lambda s, r: ref[s * P_BLK + r:s * P_BLK + r + ROW_CHUNK, :])
    t_re = jnp.concatenate([t_ref[0, 0:F_HALF], t_ref[0, P_BLK:P_BLK + F_HALF]], axis=0)
    t_im = jnp.concatenate([t_ref[0, F_HALF:P_BLK], t_ref[0, P_BLK + F_HALF:2 * P_BLK]], axis=0)
    ka = jnp.dot(t_re, _stacked_blocks(cr_ref, sr_ref, g, time_block(a_ref), s1n),
                 preferred_element_type=F32)
    kd = jnp.dot(t_im, _stacked_blocks(cr_ref, sr_ref, g, time_block(d_ref), s1n),
                 preferred_element_type=F32)
    f2 = lax.broadcasted_iota(jnp.int32, (F_HALF, LANES), 0)
    reps = HYENA_WIDTH // LANES
    for blk, f1 in ((0, g), (1, 2 * s1n - 1 - g)):
        lo = blk * P_BLK
        phi = (2 * (f1 + 2 * s1n * f2) + 1).astype(F32) * (math.pi / (4 * seq))
        c = pltpu.repeat(jnp.cos(phi), reps, axis=1) * (1.0 / seq)
        s = pltpu.repeat(jnp.sin(phi), reps, axis=1) * (1.0 / seq)
        for r in range(0, F_HALF, ROW_CHUNK):
            rows = slice(blk * F_HALF + r, blk * F_HALF + r + ROW_CHUNK)
            cc, ss = c[r:r + ROW_CHUNK], s[r:r + ROW_CHUNK]
            g_ref[0, lo + r:lo + r + ROW_CHUNK] = cc * ka[rows] - ss * kd[rows]
            g_ref[0, lo + F_HALF + r:lo + F_HALF + r + ROW_CHUNK] = cc * kd[rows] + ss * ka[rows]


def _filter_spectrum(cr, sr, t, a, d):
    seq = a.shape[0]
    s1n = seq // P_BLK
    return pl.pallas_call(
        functools.partial(_spectrum_kernel, seq=seq),
        out_shape=jax.ShapeDtypeStruct((s1n, 2 * P_BLK, HYENA_WIDTH), F32),
        grid=(s1n,),
        in_specs=[_SMEM, _SMEM, pl.BlockSpec((1, 2 * P_BLK, 2 * P_BLK), lambda g: (g, 0, 0)),
                  _resident(a.shape), _resident(d.shape)],
        out_specs=pl.BlockSpec((1, 2 * P_BLK, HYENA_WIDTH), lambda g: (g, 0, 0)),
        compiler_params=_params(("parallel",)),
        name="hyena_spectrum",
    )(cr, sr, t, a, d)


def _hy_fwd_kernel(cr_ref, sr_ref, t_ref, g_ref, z_ref, y_ref, *, s1n, nb):
    g = pl.program_id(0)
    for bi in range(nb):
        for cols in _COL_HALVES:
            x = _stacked_blocks(
                cr_ref, sr_ref, g,
                lambda s, r: z_ref[bi, s * P_BLK + r:s * P_BLK + r + ROW_CHUNK, cols].astype(F32), s1n)
            res = jnp.dot(t_ref[0], x, preferred_element_type=F32)
            for lo in range(0, 2 * P_BLK, P_BLK):
                for r in range(lo, lo + F_HALF, ROW_CHUNK):
                    re_rows = slice(r, r + ROW_CHUNK)
                    im_rows = slice(r + F_HALF, r + F_HALF + ROW_CHUNK)
                    zr, wi = res[re_rows], res[im_rows]
                    gr, gi = g_ref[0, re_rows, cols], g_ref[0, im_rows, cols]
                    y_ref[bi, 0, re_rows, cols] = (gr * zr + gi * wi).astype(BF16)
                    y_ref[bi, 0, im_rows, cols] = (gr * wi - gi * zr).astype(BF16)


def _hy_inv_kernel(cr_ref, sr_ref, tt_ref, y_ref, z_ref, g0_ref, fb_ref, gn_ref, o_ref,
                   acc_ref, zs_ref, g0s_ref, *, s1n, nb):
    g = pl.program_id(1)

    @pl.when(g == 0)
    def _():
        acc_ref[...] = jnp.zeros_like(acc_ref)

    tblk = pl.ds(pl.multiple_of(g * P_BLK, P_BLK), P_BLK)
    zs_ref[:, tblk, :] = z_ref[...]
    g0s_ref[:, tblk, :] = g0_ref[...]
    for bi in range(nb):
        for cols in _COL_HALVES:
            uv = jnp.dot(tt_ref[g], y_ref[bi, 0, :, cols], preferred_element_type=F32)
            for r in range(0, P_BLK, ROW_CHUNK):
                u = uv[r:r + ROW_CHUNK]
                vn = uv[P_BLK + r:P_BLK + r + ROW_CHUNK]
                acc_ref[bi, r:r + ROW_CHUNK, cols] += u
                for t1 in range(1, s1n):
                    rows = slice(t1 * P_BLK + r, t1 * P_BLK + r + ROW_CHUNK)
                    acc_ref[bi, rows, cols] += cr_ref[g, t1] * u + sr_ref[g, t1] * vn

    @pl.when(g == s1n - 1)
    def _():
        for bi in range(nb):
            for r in range(0, s1n * P_BLK, ROW_CHUNK):
                rows = slice(r, r + ROW_CHUNK)
                y = ((acc_ref[bi, rows, :] + zs_ref[bi, rows, :].astype(F32) * fb_ref[...])
                     * g0s_ref[bi, rows, :].astype(F32))
                o_ref[bi, rows, :] = _rms_norm(y, gn_ref[...]).astype(BF16)


_INV_STATE_BUDGET = 28 * 1024 * 1024


def _hyena_conv(zg, g0, cr, sr, t, tt, gspec, fbias, g_hyena):
    bsz, seq, w = zg.shape
    s1n = seq // P_BLK
    nb = max(n for n in (4, 2, 1) if bsz % n == 0)
    mat = pl.BlockSpec((1, 2 * P_BLK, 2 * P_BLK), lambda g, b: (g, 0, 0))
    spec = pl.BlockSpec((1, 2 * P_BLK, w), lambda g, b: (g, 0, 0))
    y = pl.pallas_call(
        functools.partial(_hy_fwd_kernel, s1n=s1n, nb=nb),
        out_shape=jax.ShapeDtypeStruct((bsz, s1n, 2 * P_BLK, w), BF16),
        grid=(s1n, bsz // nb),
        in_specs=[_SMEM, _SMEM, mat, spec, pl.BlockSpec((nb, seq, w), lambda g, b: (b, 0, 0))],
        out_specs=pl.BlockSpec((nb, 1, 2 * P_BLK, w), lambda g, b: (b, g, 0, 0)),
        compiler_params=_params(("parallel", "parallel")),
        name="hyena_fwd",
    )(cr, sr, t, gspec, zg)
    state_bytes = seq * w * (4 + 2 + 2 + 2 * 2)
    nb = 2 if bsz % 2 == 0 and 2 * state_bytes <= _INV_STATE_BUDGET else 1
    blk = pl.BlockSpec((nb, P_BLK, w), lambda b, g: (b, g, 0))
    return pl.pallas_call(
        functools.partial(_hy_inv_kernel, s1n=s1n, nb=nb),
        out_shape=jax.ShapeDtypeStruct((bsz, seq, w), BF16),
        grid=(bsz // nb, s1n),
        in_specs=[_SMEM, _SMEM, _resident(tt.shape),
                  pl.BlockSpec((nb, 1, 2 * P_BLK, w), lambda b, g: (b, g, 0, 0)),
                  blk, blk, _resident(fbias.shape), _resident(g_hyena.shape)],
        out_specs=pl.BlockSpec((nb, seq, w), lambda b, g: (b, 0, 0)),
        scratch_shapes=[pltpu.VMEM((nb, seq, w), F32), pltpu.VMEM((nb, seq, w), BF16),
                        pltpu.VMEM((nb, seq, w), BF16)],
        compiler_params=_params(("parallel", "arbitrary")),
        name="hyena_inv",
    )(cr, sr, tt, y, zg, g0, fbias, g_hyena)


MERGE_TILE = 2048
MERGE_ROW_CHUNKS = 8


def _merge_kernel(ma_ref, mh_ref, x_ref, wa_ref, wh_ref, g_ref, b_ref, o_ref):
    tl = x_ref.shape[1]
    chunk = tl // MERGE_ROW_CHUNKS
    for lo in range(0, tl, chunk):
        rows = slice(lo, lo + chunk)
        y = (jnp.dot(ma_ref[0, rows], wa_ref[...], preferred_element_type=F32)
             + jnp.dot(mh_ref[0, rows], wh_ref[...], preferred_element_type=F32))
        o_ref[0, rows] = _layer_norm(ALPHA * x_ref[0, rows] + y, g_ref[...], b_ref[...])


def _merge(ma, mh, x, w_out, g, b, tl):
    bsz, seq, d = x.shape
    wa, wh = w_out[:ATTN_WIDTH], w_out[ATTN_WIDTH:]
    half = pl.BlockSpec((1, tl, ATTN_WIDTH), lambda bb, i: (bb, i, 0))
    full = pl.BlockSpec((1, tl, d), lambda bb, i: (bb, i, 0))
    return pl.pallas_call(
        _merge_kernel,
        out_shape=jax.ShapeDtypeStruct(x.shape, F32),
        grid=(bsz, seq // tl),
        in_specs=[half, half, full, _resident(wa.shape), _resident(wh.shape),
                  _resident(g.shape), _resident(b.shape)],
        out_specs=full,
        compiler_params=_params(("parallel", "parallel")),
        name="merge_ln1",
    )(ma, mh, x, wa, wh, g, b)


FF_CHUNK = 256
N_FF_CHUNKS = D_FF // FF_CHUNK
FFN_OUT_PARTS = 1


def _ffn_kernel(xp_ref, x_ref, xn_ref, wi_ref, cw_ref, cb_ref, wo_ref, g_ref, b_ref,
                o_ref, xs_ref, hid_ref, os_ref, *, tl):
    _fill_halo_slabs(xs_ref, xp_ref, x_ref, xn_ref, tl)
    xh = _load_interleaved(xs_ref).astype(BF16)
    for j in range(N_FF_CHUNKS):
        conv = []
        for part in range(2):
            cols = slice(part * D_FF + j * FF_CHUNK, part * D_FF + (j + 1) * FF_CHUNK)
            u = jnp.dot(xh, wi_ref[:, cols], preferred_element_type=F32)
            conv.append(_dwconv3_interleaved(u, cw_ref[:, cols], cb_ref[:, cols]))
        gate = conv[1]
        gelu = 0.5 * gate * (1.0 + lax.erf(gate * (2.0 ** -0.5)))
        hid_ref[:, j * FF_CHUNK:(j + 1) * FF_CHUNK] = (conv[0] * gelu).astype(BF16)
    nv = xs_ref.shape[1] // SUBLANES
    for lo in range(0, nv, nv // FFN_OUT_PARTS):
        groups = range(lo, lo + nv // FFN_OUT_PARTS)
        rws = slice(lo * SUBLANES, (lo + nv // FFN_OUT_PARTS) * SUBLANES)
        y = jnp.dot(hid_ref[rws, :], wo_ref[...], preferred_element_type=F32)
        out = _layer_norm(ALPHA * _load_interleaved(xs_ref, groups) + y, g_ref[...], b_ref[...])
        _store_natural(os_ref, 0, out, groups)
    for k in range(os_ref.shape[0]):
        o_ref[0, :, k * LANES:(k + 1) * LANES] = os_ref[k, HALO:HALO + tl, :]


def _ffn(x1, w_in, conv_w, conv_b, w_out, g, b, tl):
    bsz, seq, d = x1.shape
    cb = conv_b.reshape(1, -1)
    return pl.pallas_call(
        functools.partial(_ffn_kernel, tl=tl),
        out_shape=jax.ShapeDtypeStruct(x1.shape, F32),
        grid=(bsz, seq // tl),
        in_specs=_halo_specs(tl, seq, d) + [_resident(w_in.shape), _resident(conv_w.shape), _resident(cb.shape),
                                            _resident(w_out.shape), _resident(g.shape), _resident(b.shape)],
        out_specs=pl.BlockSpec((1, tl, d), lambda bb, i: (bb, i, 0)),
        scratch_shapes=[pltpu.VMEM((d // LANES, tl + 2 * HALO, LANES), F32),
                        pltpu.VMEM((tl + 2 * HALO, D_FF), BF16),
                        pltpu.VMEM((d // LANES, tl + 2 * HALO, LANES), F32)],
        compiler_params=_params(("parallel", "parallel")),
        name="conv_ffn",
    )(x1, x1, x1, w_in, conv_w, cb, w_out, g, b)


def _encoder_layer(x, p, tl=1024, rb=16):
    seq = x.shape[1]
    rows = seq // GRID_W
    assert seq % tl == 0 and seq % P_BLK == 0 and seq % GRID_W == 0
    assert rows >= NA_ROWS and rows % rb == 0
    q, k, v, zg, g0 = _in_proj(x, p["w_in"], p["short_w"], p["short_b"], tl)
    ma = _attention(q, k, v, p["bias"], p["g_attn"], rb)
    a, d = _filter_taps(seq, p["filt_w1"], p["filt_b1"], p["filt_freq"], p["filt_w_inner"],
                        p["filt_b_inner"], p["filt_w3"])
    cr, sr = _pair_coefficients(seq)
    t, tt = _dft_matrices(seq)
    gspec = _filter_spectrum(cr, sr, t, a, d)
    mh = _hyena_conv(zg, g0, cr, sr, t, tt, gspec, p["filt_bias"], p["g_hyena"])
    x1 = _merge(ma, mh, x, p["w_out"], p["ln1_g"], p["ln1_b"], min(MERGE_TILE, seq))
    return _ffn(x1, p["ffn_w_in"], p["ffn_conv_w"], p["ffn_conv_b"], p["ffn_w_out"],
                p["ln2_g"], p["ln2_b"], tl)


def kernel(x_prompt, x_sample, w_in, short_w, short_b, rpb, filt_w1, filt_b1, filt_freq, filt_w_inner,
           filt_b_inner, filt_w3, filt_bias, g_attn, g_hyena, w_out, ln1_g, ln1_b, ffn_w_in, ffn_conv_w,
           ffn_conv_b, ffn_w_out, ln2_g, ln2_b):
    assert w_in.shape[0] == DEPTH == 1
    row = lambda a: a[0].reshape(1, -1)
    p = dict(
        w_in=w_in[0].astype(BF16), short_w=short_w[0], short_b=short_b[0],
        bias=_attention_bias(rpb[0]), g_attn=row(g_attn),
        filt_w1=filt_w1[0], filt_b1=filt_b1[0], filt_freq=filt_freq[0], filt_w_inner=filt_w_inner[0],
        filt_b_inner=filt_b_inner[0], filt_w3=filt_w3[0], filt_bias=filt_bias[0], g_hyena=row(g_hyena),
        w_out=w_out[0].astype(BF16), ln1_g=row(ln1_g), ln1_b=row(ln1_b),
        ffn_w_in=ffn_w_in[0].astype(BF16), ffn_conv_w=ffn_conv_w[0], ffn_conv_b=ffn_conv_b[0],
        ffn_w_out=ffn_w_out[0].astype(BF16), ln2_g=row(ln2_g), ln2_b=row(ln2_b),
    )
    return (_encoder_layer(x_prompt, p), _encoder_layer(x_sample, p))
```

```python
import functools
import math

import numpy as np
import jax
import jax.numpy as jnp
from jax import lax
from jax.experimental import pallas as pl
from jax.experimental.pallas import tpu as pltpu

F32 = jnp.float32
BF16 = jnp.bfloat16

D_MODEL = 1024
GRID_W = 64
ATTN_WIDTH = 512
HYENA_WIDTH = 512
HEAD_DIM = 64
N_HEADS = ATTN_WIDTH // HEAD_DIM
NA_ROWS = 8
NA_COLS = 16
FILTER_EMB = 33
FILTER_BANDS = (FILTER_EMB - 1) // 2
FILTER_HIDDEN = 64
FILTER_INNER = 2
MAX_DECAY = math.log(1e-2) / 0.3
MIN_DECAY = math.log(1e-2) / 1.5
D_FF = 2816
DEPTH = 1
ALPHA = (2 * DEPTH) ** 0.25
LN_EPS = 1e-5
RMS_EPS = 1e-6
LOG2E = math.log2(math.e)

SUBLANES = 8
LANES = 128
HALO = SUBLANES
VMEM_LIMIT = 56 * 1024 * 1024

HIGHEST = lax.Precision.HIGHEST


def _params(sem):
    return pltpu.CompilerParams(dimension_semantics=sem, vmem_limit_bytes=VMEM_LIMIT)


def _resident(shape):
    nd = len(shape)
    return pl.BlockSpec(shape, lambda *_: (0,) * nd, pipeline_mode=pl.Buffered(1))


def _layer_norm(y, g, b):
    mu = jnp.mean(y, axis=-1, keepdims=True)
    yc = y - mu
    var = jnp.mean(yc * yc, axis=-1, keepdims=True)
    return yc * lax.rsqrt(var + LN_EPS) * g + b


def _rms_norm(y, g):
    ms = jnp.mean(y * y, axis=-1, keepdims=True)
    return y * lax.rsqrt(ms + RMS_EPS) * g


def _fill_halo_slabs(xs_ref, xp_ref, x_ref, xn_ref, tl):
    i = pl.program_id(1)
    last = pl.num_programs(1) - 1
    prev = jnp.where(i > 0, xp_ref[0], 0.0)
    nxt = jnp.where(i < last, xn_ref[0], 0.0)
    for k in range(xs_ref.shape[0]):
        sl = slice(k * LANES, (k + 1) * LANES)
        xs_ref[k, 0:HALO, :] = prev[:, sl]
        xs_ref[k, HALO:HALO + tl, :] = x_ref[0, :, sl]
        xs_ref[k, HALO + tl:HALO + tl + HALO, :] = nxt[:, sl]


def _load_interleaved(xs_ref):
    nslab, rows, _ = xs_ref.shape
    nv = rows // SUBLANES
    assert rows % SUBLANES == 0 and nv % SUBLANES != 0
    return jnp.concatenate(
        [jnp.concatenate([xs_ref[k, pl.ds(j, SUBLANES, stride=nv), :] for j in range(nv)], axis=0)
         for k in range(nslab)], axis=1)


def _store_natural(os_ref, first, val):
    nv = os_ref.shape[1] // SUBLANES
    for k in range(val.shape[1] // LANES):
        for j in range(nv):
            os_ref[first + k, pl.ds(j, SUBLANES, stride=nv), :] = val[j * SUBLANES:(j + 1) * SUBLANES,
                                                                      k * LANES:(k + 1) * LANES]


def _dwconv3_interleaved(u, w, b):
    head = pltpu.roll(u[-SUBLANES:], 1, axis=0)
    tail = pltpu.roll(u[:SUBLANES], SUBLANES - 1, axis=0)
    up = jnp.concatenate([head, u[:-SUBLANES]], axis=0)
    un = jnp.concatenate([u[SUBLANES:], tail], axis=0)
    return up * w[0:1] + u * w[1:2] + un * w[2:3] + b


def _halo_specs(tl, seq, d):
    nb = tl // HALO
    last = seq // HALO - 1
    return [
        pl.BlockSpec((1, HALO, d), lambda b, i: (b, jnp.maximum(i * nb - 1, 0), 0)),
        pl.BlockSpec((1, tl, d), lambda b, i: (b, i, 0)),
        pl.BlockSpec((1, HALO, d), lambda b, i: (b, jnp.minimum((i + 1) * nb, last), 0)),
    ]


def _in_proj_kernel(xp_ref, x_ref, xn_ref, w_ref, sw_ref, sb_ref,
                    q_ref, k_ref, v_ref, zg_ref, g0_ref, xs_ref, os_ref, *, tl):
    xq = x_ref[0].astype(BF16)
    proj = lambda n: jnp.dot(xq, w_ref[:, n * ATTN_WIDTH:(n + 1) * ATTN_WIDTH], preferred_element_type=F32)
    q_ref[0] = (proj(0) * (HEAD_DIM ** -0.5 * LOG2E)).astype(BF16)
    k_ref[0] = proj(1).astype(BF16)
    v = proj(2).astype(BF16)
    ones = jnp.ones((tl, LANES), BF16)
    v_ref[0] = jnp.concatenate([piece for p in range(N_HEADS // 2)
                                for piece in (v[:, p * LANES:(p + 1) * LANES], ones)], axis=1)
    _fill_halo_slabs(xs_ref, xp_ref, x_ref, xn_ref, tl)
    xh = _load_interleaved(xs_ref).astype(BF16)
    conv = []
    for n in range(3):
        lo = 3 * ATTN_WIDTH + n * HYENA_WIDTH
        u = jnp.dot(xh, w_ref[:, lo:lo + HYENA_WIDTH], preferred_element_type=F32)
        conv.append(_dwconv3_interleaved(u, sw_ref[n], sb_ref[n]))
    nslab = HYENA_WIDTH // LANES
    for first, ref, val in ((0, g0_ref, conv[0]), (nslab, zg_ref, conv[2] * conv[1])):
        _store_natural(os_ref, first, val)
        for k in range(nslab):
            ref[0, :, k * LANES:(k + 1) * LANES] = os_ref[first + k, HALO:HALO + tl, :].astype(BF16)


def _in_proj(x, w_in, short_w, short_b, tl):
    bsz, seq, d = x.shape
    sw = short_w.reshape(3, 3, HYENA_WIDTH).transpose(1, 0, 2)
    sb = short_b.reshape(3, 1, HYENA_WIDTH)
    out = jax.ShapeDtypeStruct((bsz, seq, ATTN_WIDTH), BF16)
    ospec = pl.BlockSpec((1, tl, ATTN_WIDTH), lambda b, i: (b, i, 0))
    vout = jax.ShapeDtypeStruct((bsz, seq, 2 * ATTN_WIDTH), BF16)
    vspec = pl.BlockSpec((1, tl, 2 * ATTN_WIDTH), lambda b, i: (b, i, 0))
    return pl.pallas_call(
        functools.partial(_in_proj_kernel, tl=tl),
        out_shape=(out, out, vout, out, out),
        grid=(bsz, seq // tl),
        in_specs=_halo_specs(tl, seq, d) + [_resident(w_in.shape), _resident(sw.shape), _resident(sb.shape)],
        out_specs=(ospec, ospec, vspec, ospec, ospec),
        scratch_shapes=[pltpu.VMEM((d // LANES, tl + 2 * HALO, LANES), F32),
                        pltpu.VMEM((2 * HYENA_WIDTH // LANES, tl + 2 * HALO, LANES), F32)],
        compiler_params=_params(("parallel", "parallel")),
        name="in_proj",
    )(x, x, x, w_in, sw, sb)


N_PAIRS = N_HEADS // 2
KEY_WIN = NA_ROWS * GRID_W
N_DR = 2 * NA_ROWS - 1
N_DC = 2 * NA_COLS - 1
SOFTMAX_PARTS = 4


def _bias_kernel(rpb_ref, ea_ref, eb_ref, o_ref, t_ref):
    row = lax.broadcasted_iota(jnp.int32, (GRID_W, LANES), 0)
    lane = lax.broadcasted_iota(jnp.int32, (GRID_W, LANES), 1)
    first = lane < GRID_W
    qcol = lane & (GRID_W - 1)
    col_start = jnp.clip(qcol - NA_COLS // 2, 0, GRID_W - NA_COLS)
    valid = (row >= col_start) & (row < col_start + NA_COLS)
    wa = jnp.dot(rpb_ref[0, 0], ea_ref[...], precision=HIGHEST, preferred_element_type=F32)
    wb = jnp.dot(rpb_ref[0, 1], eb_ref[...], precision=HIGHEST, preferred_element_type=F32)

    def rotate_rows(x):
        return pltpu.roll(x, 0, axis=1, stride=1, stride_axis=0)

    for dr in range(N_DR):
        ta = rotate_rows(jnp.broadcast_to(wa[dr:dr + 1], (GRID_W, LANES)))
        tb = rotate_rows(jnp.broadcast_to(wb[dr:dr + 1], (GRID_W, LANES)))
        t_ref[dr] = jnp.where(valid, jnp.where(first, ta, tb) * LOG2E, -jnp.inf)
    for cls in range(NA_ROWS):
        for i in range(NA_ROWS):
            o_ref[0, cls, i * GRID_W:(i + 1) * GRID_W, :] = t_ref[i - cls + NA_ROWS - 1]


def _toeplitz_selectors():
    m = np.arange(LANES)
    dc = np.where(m < GRID_W, (NA_COLS - 1) - np.minimum(m, NA_COLS - 1),
                  np.minimum(LANES - m, NA_COLS - 1) + (NA_COLS - 1))
    ea = np.zeros((LANES, LANES), np.float32)
    ea[dc, m] = 1.0
    ea[:, GRID_W] = 0.0
    eb = np.roll(ea, GRID_W, axis=1)
    return ea, eb


def _attention_bias(rpb):
    ea, eb = _toeplitz_selectors()
    rp = jnp.pad(rpb.astype(F32), ((0, 0), (0, 16 - N_DR), (0, LANES - N_DC))).reshape(N_PAIRS, 2, 16, LANES)
    return pl.pallas_call(
        _bias_kernel,
        out_shape=jax.ShapeDtypeStruct((N_PAIRS, NA_ROWS, KEY_WIN, LANES), F32),
        grid=(N_PAIRS,),
        in_specs=[pl.BlockSpec((1, 2, 16, LANES), lambda p: (p, 0, 0, 0)),
                  _resident(ea.shape), _resident(eb.shape)],
        out_specs=pl.BlockSpec((1, NA_ROWS, KEY_WIN, LANES), lambda p: (p, 0, 0, 0)),
        scratch_shapes=[pltpu.VMEM((N_DR, GRID_W, LANES), F32)],
        compiler_params=_params(("parallel",)),
        name="attn_bias",
    )(rp, jnp.asarray(ea), jnp.asarray(eb))


def _attn_kernel(q_ref, k_ref, vx_ref, bias_ref, g_ref, o_ref, *, rows, rb, unroll):
    jb = pl.program_id(1)
    lane = lax.broadcasted_iota(jnp.int32, (GRID_W, LANES), 1)
    first = lane < HEAD_DIM
    keep = (first.astype(F32).astype(BF16), (~first).astype(F32).astype(BF16))
    nt = (((1,), (1,)), ((), ()))
    tn = (((0,), (0,)), ((), ()))

    def score_stage(j):
        r = jb * rb + j
        start = jnp.clip(r - NA_ROWS // 2, 0, rows - NA_ROWS)
        qoff = pl.multiple_of(j * GRID_W, GRID_W)
        koff = pl.multiple_of(start * GRID_W, GRID_W)
        scores = []
        for p in range(N_PAIRS):
            sl = slice(p * LANES, (p + 1) * LANES)
            qp = q_ref[0, pl.ds(qoff, GRID_W), sl]
            qblk = jnp.concatenate([qp * keep[0], qp * keep[1]], axis=0)
            kp = k_ref[0, pl.ds(koff, KEY_WIN), sl]
            scores.append(lax.dot_general(kp, qblk, nt, preferred_element_type=F32))
        return r - start, qoff, koff, scores

    def output_stage(cls, qoff, koff, scores):
        probs = []
        step = KEY_WIN // SOFTMAX_PARTS
        for p in range(N_PAIRS):
            parts, maxes = [], []
            for lo in range(0, KEY_WIN, step):
                s = scores[p][lo:lo + step] + bias_ref[p, cls, lo:lo + step]
                m_g = jnp.max(s, axis=0, keepdims=True)
                parts.append(jnp.exp2(s - m_g))
                maxes.append(m_g)
            m = functools.reduce(jnp.maximum, maxes)
            probs.append(jnp.concatenate([e.astype(BF16) * jnp.exp2(m_g - m).astype(BF16)
                                          for e, m_g in zip(parts, maxes)], axis=0))
        outs = []
        for p in range(N_PAIRS):
            vx = vx_ref[0, pl.ds(koff, KEY_WIN), 2 * p * LANES:2 * (p + 1) * LANES]
            o = lax.dot_general(probs[p], vx, tn, preferred_element_type=F32)
            oa = o[:GRID_W, :LANES] / o[:GRID_W, LANES:]
            ob = o[GRID_W:, :LANES] / o[GRID_W:, LANES:]
            outs.append(jnp.where(first, oa, ob))
        o_ref[0, pl.ds(qoff, GRID_W), :] = _rms_norm(jnp.concatenate(outs, axis=1), g_ref[...]).astype(BF16)

    def body(jj, carry):
        state = score_stage(jj * unroll)
        for u in range(unroll):
            ahead = score_stage(jj * unroll + u + 1) if u + 1 < unroll else None
            output_stage(*state)
            state = ahead
        return carry

    lax.fori_loop(0, rb // unroll, body, 0)


def _attention(q, k, vx, bias, g_attn, rb, unroll=16):
    bsz, seq, w = q.shape
    unroll = min(unroll, rb)
    assert rb % unroll == 0
    rows = seq // GRID_W
    tq = rb * GRID_W
    return pl.pallas_call(
        functools.partial(_attn_kernel, rows=rows, rb=rb, unroll=unroll),
        out_shape=jax.ShapeDtypeStruct((bsz, seq, w), BF16),
        grid=(bsz, rows // rb),
        in_specs=[
            pl.BlockSpec((1, tq, w), lambda b, i: (b, i, 0)),
            pl.BlockSpec((1, seq, w), lambda b, i: (b, 0, 0)),
            pl.BlockSpec((1, seq, 2 * w), lambda b, i: (b, 0, 0)),
            _resident(bias.shape),
            _resident(g_attn.shape),
        ],
        out_specs=pl.BlockSpec((1, tq, w), lambda b, i: (b, i, 0)),
        compiler_params=_params(("parallel", "arbitrary")),
        name="attention",
    )(q, k, vx, bias, g_attn)


def _filter_kernel(w1t_ref, w1c_ref, w1s_ref, b1_ref, fq_ref, wi_ref, bi_ref, w3_ref,
                   a_ref, d_ref, cb_ref, sb_ref, h_ref, *, seq, tl):
    i = pl.program_id(0)
    rows = tl + 2 * SUBLANES
    half = rows // 2
    band = lax.broadcasted_iota(jnp.int32, (rows, LANES), 1)
    freqs = jnp.where(band < FILTER_BANDS,
                      1e-4 + band.astype(F32) * ((FILTER_BANDS - 1 - 1e-4) / (FILTER_BANDS - 1)), 0.0)
    rad = freqs * (2.0 * math.pi / seq)
    local = lax.broadcasted_iota(jnp.int32, (rows, LANES), 0)

    @pl.when(i == 0)
    def _():
        cb_ref[...] = jnp.cos(local.astype(F32) * rad)
        sb_ref[...] = jnp.sin(local.astype(F32) * rad)

    base = (i * tl).astype(F32) * rad[0:SUBLANES]
    ca = jnp.cos(base)[0:1]
    sa = jnp.sin(base)[0:1]
    cos_ang = ca * cb_ref[...] - sa * sb_ref[...]
    sin_ang = sa * cb_ref[...] + ca * sb_ref[...]
    t = (local + i * tl).astype(F32) * (1.0 / (seq - 1))
    fq = fq_ref[...]
    dot = functools.partial(jnp.dot, precision=HIGHEST, preferred_element_type=F32)
    side = lambda x: jnp.concatenate([x[:half], x[half:]], axis=1)
    left = lax.broadcasted_iota(jnp.int32, (half, LANES), 1) < FILTER_HIDDEN
    t_packed = jnp.where(left, t[:half], t[half:])
    pre = (t_packed * w1t_ref[...] + dot(side(cos_ang), w1c_ref[...]) - dot(side(sin_ang), w1s_ref[...])
           + b1_ref[...])
    h = jnp.sin(fq * pre)
    for n in range(FILTER_INNER):
        h = jnp.sin(fq * (dot(h, wi_ref[n]) + bi_ref[n]))
    chan = lax.broadcasted_iota(jnp.int32, (half, HYENA_WIDTH), 1).astype(F32)
    deltas = jnp.abs(MIN_DECAY + chan * ((MAX_DECAY - MIN_DECAY) / (HYENA_WIDTH - 1)))
    for part in range(2):
        rws = slice(part * half, (part + 1) * half)
        pos = lax.broadcasted_iota(jnp.int32, (half, HYENA_WIDTH), 0) + (i * tl + part * half)
        decay = jnp.exp(-(pos.astype(F32) * (1.0 / (seq - 1))) * deltas)
        for n in range(2):
            cols = slice(n * HYENA_WIDTH, (n + 1) * HYENA_WIDTH)
            h_ref[rws, cols] = dot(h, w3_ref[part, :, cols]) * decay
    h_fwd = h_ref[0:tl, 0:HYENA_WIDTH]
    lag = lax.broadcasted_iota(jnp.int32, (tl, HYENA_WIDTH), 0) + (i * tl + 1)
    h_bwd = jnp.where(lag < seq, h_ref[pl.ds(1, tl), HYENA_WIDTH:2 * HYENA_WIDTH], 0.0)
    a_ref[...] = h_fwd + h_bwd
    d_ref[...] = h_bwd - h_fwd


def _filter_taps(seq, w1, b1, freq, w_inner, b_inner, w3, tl=512):
    pad = LANES - FILTER_BANDS
    twice = lambda v: jnp.tile(v.reshape(1, -1), (1, 2))
    blockdiag = lambda w: jnp.kron(jnp.eye(2, dtype=w.dtype), w)
    w1c = blockdiag(jnp.pad(w1[1:1 + FILTER_BANDS], ((0, pad), (0, 0))))
    w1s = blockdiag(jnp.pad(w1[1 + FILTER_BANDS:], ((0, pad), (0, 0))))
    zero = jnp.zeros_like(w3)
    args = (twice(w1[0]), w1c, w1s, twice(b1), twice(freq),
            jnp.stack([blockdiag(w_inner[n]) for n in range(FILTER_INNER)]),
            jnp.stack([twice(b_inner[n]) for n in range(FILTER_INNER)]),
            jnp.stack([jnp.concatenate([w3, zero]), jnp.concatenate([zero, w3])]))
    out = jax.ShapeDtypeStruct((seq, HYENA_WIDTH), F32)
    ospec = pl.BlockSpec((tl, HYENA_WIDTH), lambda i: (i, 0))
    return pl.pallas_call(
        functools.partial(_filter_kernel, seq=seq, tl=tl),
        out_shape=(out, out),
        grid=(seq // tl,),
        in_specs=[_resident(a.shape) for a in args],
        out_specs=(ospec, ospec),
        scratch_shapes=[pltpu.VMEM((tl + 2 * SUBLANES, LANES), F32),
                        pltpu.VMEM((tl + 2 * SUBLANES, LANES), F32),
                        pltpu.VMEM((tl + 2 * SUBLANES, 2 * HYENA_WIDTH), F32)],
        compiler_params=_params(("arbitrary",)),
        name="hyena_filter",
    )(*args)


P_BLK = 512
F_HALF = P_BLK // 2


def _pair_coefficients(seq):
    s1n = seq // P_BLK
    g = np.arange(s1n)[:, None]
    s = np.arange(s1n)[None, :]
    ang = np.pi * (2 * g + 1) * s / (2 * s1n)
    return jnp.asarray(np.cos(ang), F32), jnp.asarray(np.sin(ang), F32)


def _dft_kernel(t_ref, tt_ref, cb_ref, sb_ref, *, seq):
    g = pl.program_id(0)
    s1n = seq // P_BLK

    @pl.when(g == 0)
    def _():
        f2 = lax.broadcasted_iota(jnp.int32, (F_HALF, P_BLK), 0)
        s_odd = 2 * lax.broadcasted_iota(jnp.int32, (F_HALF, P_BLK), 1) + 1
        beta = ((f2 * s_odd) & (2 * P_BLK - 1)).astype(F32) * (math.pi / P_BLK)
        cb_ref[...] = jnp.cos(beta)
        sb_ref[...] = jnp.sin(beta)

    cb = cb_ref[...]
    sb = sb_ref[...]
    s_odd = 2 * lax.broadcasted_iota(jnp.int32, (SUBLANES, P_BLK), 1) + 1

    def block(f1):
        alpha = ((s_odd * (2 * f1 + 1)) & (8 * seq - 1)).astype(F32) * (math.pi / (4 * seq))
        ca = jnp.cos(alpha)[0:1]
        sa = jnp.sin(alpha)[0:1]
        return ca * cb - sa * sb, sa * cb + ca * sb

    mra, mia = block(g)
    mrb, mib = block(2 * s1n - 1 - g)
    for r, (left, right) in enumerate(((mra, -mia), (mia, mra), (mrb, mib), (mib, -mrb))):
        rows = slice(r * F_HALF, (r + 1) * F_HALF)
        t_ref[0, rows, :P_BLK] = left.astype(BF16)
        t_ref[0, rows, P_BLK:] = right.astype(BF16)
        tt_ref[0, :P_BLK, rows] = left.T.astype(BF16)
        tt_ref[0, P_BLK:, rows] = right.T.astype(BF16)


def _dft_matrices(seq):
    s1n = seq // P_BLK
    mat = jax.ShapeDtypeStruct((s1n, 2 * P_BLK, 2 * P_BLK), BF16)
    spec = pl.BlockSpec((1, 2 * P_BLK, 2 * P_BLK), lambda g: (g, 0, 0))
    return pl.pallas_call(
        functools.partial(_dft_kernel, seq=seq),
        out_shape=(mat, mat),
        grid=(s1n,),
        out_specs=(spec, spec),
        scratch_shapes=[pltpu.VMEM((F_HALF, P_BLK), F32), pltpu.VMEM((F_HALF, P_BLK), F32)],
        compiler_params=_params(("arbitrary",)),
        name="dft_matrices",
    )()


_SMEM = pl.BlockSpec(memory_space=pltpu.SMEM)
_COL_HALVES = (slice(0, HYENA_WIDTH // 2), slice(HYENA_WIDTH // 2, HYENA_WIDTH))


ROW_CHUNK = 64


def _stacked_blocks(cr_ref, sr_ref, g, block, s1n):
    re, im = [], []
    for r in range(0, P_BLK, ROW_CHUNK):
        ar = block(0, r)
        bi = None
        for s in range(1, s1n):
            b = block(s, r)
            ar = ar + cr_ref[g, s] * b
            bi = sr_ref[g, s] * b if bi is None else bi + sr_ref[g, s] * b
        re.append(ar)
        im.append(bi)
    return jnp.concatenate(re + im, axis=0).astype(BF16)


def _spectrum_kernel(cr_ref, sr_ref, t_ref, a_ref, d_ref, g_ref, *, seq):
    g = pl.program_id(0)
    s1n = seq // P_BLK
    time_block = lambda ref: (lambda s, r: ref[s * P_BLK + r:s * P_BLK + r + ROW_CHUNK, :])
    t_re = jnp.concatenate([t_ref[0, 0:F_HALF], t_ref[0, P_BLK:P_BLK + F_HALF]], axis=0)
    t_im = jnp.concatenate([t_ref[0, F_HALF:P_BLK], t_ref[0, P_BLK + F_HALF:2 * P_BLK]], axis=0)
    ka = jnp.dot(t_re, _stacked_blocks(cr_ref, sr_ref, g, time_block(a_ref), s1n),
                 preferred_element_type=F32)
    kd = jnp.dot(t_im, _stacked_blocks(cr_ref, sr_ref, g, time_block(d_ref), s1n),
                 preferred_element_type=F32)
    f2 = lax.broadcasted_iota(jnp.int32, (F_HALF, LANES), 0)
    reps = HYENA_WIDTH // LANES
    for blk, f1 in ((0, g), (1, 2 * s1n - 1 - g)):
        lo = blk * P_BLK
        phi = (2 * (f1 + 2 * s1n * f2) + 1).astype(F32) * (math.pi / (4 * seq))
        c = pltpu.repeat(jnp.cos(phi), reps, axis=1) * (1.0 / seq)
        s = pltpu.repeat(jnp.sin(phi), reps, axis=1) * (1.0 / seq)
        for r in range(0, F_HALF, ROW_CHUNK):
            rows = slice(blk * F_HALF + r, blk * F_HALF + r + ROW_CHUNK)
            cc, ss = c[r:r + ROW_CHUNK], s[r:r + ROW_CHUNK]
            g_ref[0, lo + r:lo + r + ROW_CHUNK] = cc * ka[rows] - ss * kd[rows]
            g_ref[0, lo + F_HALF + r:lo + F_HALF + r + ROW_CHUNK] = cc * kd[rows] + ss * ka[rows]


def _filter_spectrum(cr, sr, t, a, d):
    seq = a.shape[0]
    s1n = seq // P_BLK
    return pl.pallas_call(
        functools.partial(_spectrum_kernel, seq=seq),
        out_shape=jax.ShapeDtypeStruct((s1n, 2 * P_BLK, HYENA_WIDTH), F32),
        grid=(s1n,),
        in_specs=[_SMEM, _SMEM, pl.BlockSpec((1, 2 * P_BLK, 2 * P_BLK), lambda g: (g, 0, 0)),
                  _resident(a.shape), _resident(d.shape)],
        out_specs=pl.BlockSpec((1, 2 * P_BLK, HYENA_WIDTH), lambda g: (g, 0, 0)),
        compiler_params=_params(("parallel",)),
        name="hyena_spectrum",
    )(cr, sr, t, a, d)


def _hy_fwd_kernel(cr_ref, sr_ref, t_ref, g_ref, z_ref, y_ref, *, s1n, nb):
    g = pl.program_id(0)
    for bi in range(nb):
        for cols in _COL_HALVES:
            x = _stacked_blocks(
                cr_ref, sr_ref, g,
                lambda s, r: z_ref[bi, s * P_BLK + r:s * P_BLK + r + ROW_CHUNK, cols].astype(F32), s1n)
            res = jnp.dot(t_ref[0], x, preferred_element_type=F32)
            for lo in range(0, 2 * P_BLK, P_BLK):
                for r in range(lo, lo + F_HALF, ROW_CHUNK):
                    re_rows = slice(r, r + ROW_CHUNK)
                    im_rows = slice(r + F_HALF, r + F_HALF + ROW_CHUNK)
                    zr, wi = res[re_rows], res[im_rows]
                    gr, gi = g_ref[0, re_rows, cols], g_ref[0, im_rows, cols]
                    y_ref[bi, 0, re_rows, cols] = (gr * zr + gi * wi).astype(BF16)
                    y_ref[bi, 0, im_rows, cols] = (gr * wi - gi * zr).astype(BF16)


def _hy_inv_kernel(cr_ref, sr_ref, tt_ref, y_ref, z_ref, g0_ref, fb_ref, gn_ref, o_ref,
                   acc_ref, zs_ref, g0s_ref, *, s1n, nb):
    g = pl.program_id(1)

    @pl.when(g == 0)
    def _():
        acc_ref[...] = jnp.zeros_like(acc_ref)

    tblk = pl.ds(pl.multiple_of(g * P_BLK, P_BLK), P_BLK)
    zs_ref[:, tblk, :] = z_ref[...]
    g0s_ref[:, tblk, :] = g0_ref[...]
    for bi in range(nb):
        for cols in _COL_HALVES:
            uv = jnp.dot(tt_ref[g], y_ref[bi, 0, :, cols], preferred_element_type=F32)
            for r in range(0, P_BLK, ROW_CHUNK):
                u = uv[r:r + ROW_CHUNK]
                vn = uv[P_BLK + r:P_BLK + r + ROW_CHUNK]
                acc_ref[bi, r:r + ROW_CHUNK, cols] += u
                for t1 in range(1, s1n):
                    rows = slice(t1 * P_BLK + r, t1 * P_BLK + r + ROW_CHUNK)
                    acc_ref[bi, rows, cols] += cr_ref[g, t1] * u + sr_ref[g, t1] * vn

    @pl.when(g == s1n - 1)
    def _():
        for bi in range(nb):
            for r in range(0, s1n * P_BLK, ROW_CHUNK):
                rows = slice(r, r + ROW_CHUNK)
                y = ((acc_ref[bi, rows, :] + zs_ref[bi, rows, :].astype(F32) * fb_ref[...])
                     * g0s_ref[bi, rows, :].astype(F32))
                o_ref[bi, rows, :] = _rms_norm(y, gn_ref[...]).astype(BF16)


_INV_STATE_BUDGET = 28 * 1024 * 1024


def _hyena_conv(zg, g0, cr, sr, t, tt, gspec, fbias, g_hyena):
    bsz, seq, w = zg.shape
    s1n = seq // P_BLK
    nb = max(n for n in (4, 2, 1) if bsz % n == 0)
    mat = pl.BlockSpec((1, 2 * P_BLK, 2 * P_BLK), lambda g, b: (g, 0, 0))
    spec = pl.BlockSpec((1, 2 * P_BLK, w), lambda g, b: (g, 0, 0))
    y = pl.pallas_call(
        functools.partial(_hy_fwd_kernel, s1n=s1n, nb=nb),
        out_shape=jax.ShapeDtypeStruct((bsz, s1n, 2 * P_BLK, w), BF16),
        grid=(s1n, bsz // nb),
        in_specs=[_SMEM, _SMEM, mat, spec, pl.BlockSpec((nb, seq, w), lambda g, b: (b, 0, 0))],
        out_specs=pl.BlockSpec((nb, 1, 2 * P_BLK, w), lambda g, b: (b, g, 0, 0)),
        compiler_params=_params(("parallel", "parallel")),
        name="hyena_fwd",
    )(cr, sr, t, gspec, zg)
    state_bytes = seq * w * (4 + 2 + 2 + 2 * 2)
    nb = 2 if bsz % 2 == 0 and 2 * state_bytes <= _INV_STATE_BUDGET else 1
    blk = pl.BlockSpec((nb, P_BLK, w), lambda b, g: (b, g, 0))
    return pl.pallas_call(
        functools.partial(_hy_inv_kernel, s1n=s1n, nb=nb),
        out_shape=jax.ShapeDtypeStruct((bsz, seq, w), BF16),
        grid=(bsz // nb, s1n),
        in_specs=[_SMEM, _SMEM, _resident(tt.shape),
                  pl.BlockSpec((nb, 1, 2 * P_BLK, w), lambda b, g: (b, g, 0, 0)),
                  blk, blk, _resident(fbias.shape), _resident(g_hyena.shape)],
        out_specs=pl.BlockSpec((nb, seq, w), lambda b, g: (b, 0, 0)),
        scratch_shapes=[pltpu.VMEM((nb, seq, w), F32), pltpu.VMEM((nb, seq, w), BF16),
                        pltpu.VMEM((nb, seq, w), BF16)],
        compiler_params=_params(("parallel", "arbitrary")),
        name="hyena_inv",
    )(cr, sr, tt, y, zg, g0, fbias, g_hyena)


MERGE_TILE = 2048
MERGE_ROW_CHUNKS = 8


def _merge_kernel(ma_ref, mh_ref, x_ref, wa_ref, wh_ref, g_ref, b_ref, o_ref):
    tl = x_ref.shape[1]
    chunk = tl // MERGE_ROW_CHUNKS
    for lo in range(0, tl, chunk):
        rows = slice(lo, lo + chunk)
        y = (jnp.dot(ma_ref[0, rows], wa_ref[...], preferred_element_type=F32)
             + jnp.dot(mh_ref[0, rows], wh_ref[...], preferred_element_type=F32))
        o_ref[0, rows] = _layer_norm(ALPHA * x_ref[0, rows] + y, g_ref[...], b_ref[...])


def _merge(ma, mh, x, w_out, g, b, tl):
    bsz, seq, d = x.shape
    wa, wh = w_out[:ATTN_WIDTH], w_out[ATTN_WIDTH:]
    half = pl.BlockSpec((1, tl, ATTN_WIDTH), lambda bb, i: (bb, i, 0))
    full = pl.BlockSpec((1, tl, d), lambda bb, i: (bb, i, 0))
    return pl.pallas_call(
        _merge_kernel,
        out_shape=jax.ShapeDtypeStruct(x.shape, F32),
        grid=(bsz, seq // tl),
        in_specs=[half, half, full, _resident(wa.shape), _resident(wh.shape),
                  _resident(g.shape), _resident(b.shape)],
        out_specs=full,
        compiler_params=_params(("parallel", "parallel")),
        name="merge_ln1",
    )(ma, mh, x, wa, wh, g, b)


FF_CHUNK = 256
N_FF_CHUNKS = D_FF // FF_CHUNK


def _ffn_kernel(xp_ref, x_ref, xn_ref, wi_ref, cw_ref, cb_ref, wo_ref, g_ref, b_ref,
                o_ref, xs_ref, hid_ref, os_ref, *, tl):
    _fill_halo_slabs(xs_ref, xp_ref, x_ref, xn_ref, tl)
    xh = _load_interleaved(xs_ref).astype(BF16)
    for j in range(N_FF_CHUNKS):
        conv = []
        for part in range(2):
            cols = slice(part * D_FF + j * FF_CHUNK, part * D_FF + (j + 1) * FF_CHUNK)
            u = jnp.dot(xh, wi_ref[:, cols], preferred_element_type=F32)
            conv.append(_dwconv3_interleaved(u, cw_ref[:, cols], cb_ref[:, cols]))
        gate = conv[1]
        gelu = 0.5 * gate * (1.0 + lax.erf(gate * (2.0 ** -0.5)))
        hid_ref[:, j * FF_CHUNK:(j + 1) * FF_CHUNK] = (conv[0] * gelu).astype(BF16)
    y = jnp.dot(hid_ref[...], wo_ref[...], preferred_element_type=F32)
    out = _layer_norm(ALPHA * _load_interleaved(xs_ref) + y, g_ref[...], b_ref[...])
    _store_natural(os_ref, 0, out)
    for k in range(os_ref.shape[0]):
        o_ref[0, :, k * LANES:(k + 1) * LANES] = os_ref[k, HALO:HALO + tl, :]


def _ffn(x1, w_in, conv_w, conv_b, w_out, g, b, tl):
    bsz, seq, d = x1.shape
    cb = conv_b.reshape(1, -1)
    return pl.pallas_call(
        functools.partial(_ffn_kernel, tl=tl),
        out_shape=jax.ShapeDtypeStruct(x1.shape, F32),
        grid=(bsz, seq // tl),
        in_specs=_halo_specs(tl, seq, d) + [_resident(w_in.shape), _resident(conv_w.shape), _resident(cb.shape),
                                            _resident(w_out.shape), _resident(g.shape), _resident(b.shape)],
        out_specs=pl.BlockSpec((1, tl, d), lambda bb, i: (bb, i, 0)),
        scratch_shapes=[pltpu.VMEM((d // LANES, tl + 2 * HALO, LANES), F32),
                        pltpu.VMEM((tl + 2 * HALO, D_FF), BF16),
                        pltpu.VMEM((d // LANES, tl + 2 * HALO, LANES), F32)],
        compiler_params=_params(("parallel", "parallel")),
        name="conv_ffn",
    )(x1, x1, x1, w_in, conv_w, cb, w_out, g, b)


def _encoder_layer(x, p, tl=1024, rb=16):
    seq = x.shape[1]
    rows = seq // GRID_W
    assert seq % tl == 0 and seq % P_BLK == 0 and seq % GRID_W == 0
    assert rows >= NA_ROWS and rows % rb == 0
    q, k, v, zg, g0 = _in_proj(x, p["w_in"], p["short_w"], p["short_b"], tl)
    ma = _attention(q, k, v, p["bias"], p["g_attn"], rb)
    a, d = _filter_taps(seq, p["filt_w1"], p["filt_b1"], p["filt_freq"], p["filt_w_inner"],
                        p["filt_b_inner"], p["filt_w3"])
    cr, sr = _pair_coefficients(seq)
    t, tt = _dft_matrices(seq)
    gspec = _filter_spectrum(cr, sr, t, a, d)
    mh = _hyena_conv(zg, g0, cr, sr, t, tt, gspec, p["filt_bias"], p["g_hyena"])
    x1 = _merge(ma, mh, x, p["w_out"], p["ln1_g"], p["ln1_b"], min(MERGE_TILE, seq))
    return _ffn(x1, p["ffn_w_in"], p["ffn_conv_w"], p["ffn_conv_b"], p["ffn_w_out"],
                p["ln2_g"], p["ln2_b"], tl)


def kernel(x_prompt, x_sample, w_in, short_w, short_b, rpb, filt_w1, filt_b1, filt_freq, filt_w_inner,
           filt_b_inner, filt_w3, filt_bias, g_attn, g_hyena, w_out, ln1_g, ln1_b, ffn_w_in, ffn_conv_w,
           ffn_conv_b, ffn_w_out, ln2_g, ln2_b):
    assert w_in.shape[0] == DEPTH == 1
    row = lambda a: a[0].reshape(1, -1)
    p = dict(
        w_in=w_in[0].astype(BF16), short_w=short_w[0], short_b=short_b[0],
        bias=_attention_bias(rpb[0]), g_attn=row(g_attn),
        filt_w1=filt_w1[0], filt_b1=filt_b1[0], filt_freq=filt_freq[0], filt_w_inner=filt_w_inner[0],
        filt_b_inner=filt_b_inner[0], filt_w3=filt_w3[0], filt_bias=filt_bias[0], g_hyena=row(g_hyena),
        w_out=w_out[0].astype(BF16), ln1_g=row(ln1_g), ln1_b=row(ln1_b),
        ffn_w_in=ffn_w_in[0].astype(BF16), ffn_conv_w=ffn_conv_w[0], ffn_conv_b=ffn_conv_b[0],
        ffn_w_out=ffn_w_out[0].astype(BF16), ln2_g=row(ln2_g), ln2_b=row(ln2_b),
    )
    return (_encoder_layer(x_prompt, p), _encoder_layer(x_sample, p))
```

```python
import functools
import math

import numpy as np
import jax
import jax.numpy as jnp
from jax import lax
from jax.experimental import pallas as pl
from jax.experimental.pallas import tpu as pltpu

F32 = jnp.float32
BF16 = jnp.bfloat16

D_MODEL = 1024
GRID_W = 64
ATTN_WIDTH = 512
HYENA_WIDTH = 512
HEAD_DIM = 64
N_HEADS = ATTN_WIDTH // HEAD_DIM
NA_ROWS = 8
NA_COLS = 16
FILTER_EMB = 33
FILTER_BANDS = (FILTER_EMB - 1) // 2
FILTER_HIDDEN = 64
FILTER_INNER = 2
MAX_DECAY = math.log(1e-2) / 0.3
MIN_DECAY = math.log(1e-2) / 1.5
D_FF = 2816
DEPTH = 1
ALPHA = (2 * DEPTH) ** 0.25
LN_EPS = 1e-5
RMS_EPS = 1e-6
LOG2E = math.log2(math.e)

SUBLANES = 8
LANES = 128
HALO = SUBLANES
VMEM_LIMIT = 56 * 1024 * 1024

HIGHEST = lax.Precision.HIGHEST


def _params(sem):
    return pltpu.CompilerParams(dimension_semantics=sem, vmem_limit_bytes=VMEM_LIMIT)


def _resident(shape):
    nd = len(shape)
    return pl.BlockSpec(shape, lambda *_: (0,) * nd, pipeline_mode=pl.Buffered(1))


def _layer_norm(y, g, b):
    mu = jnp.mean(y, axis=-1, keepdims=True)
    yc = y - mu
    var = jnp.mean(yc * yc, axis=-1, keepdims=True)
    return yc * lax.rsqrt(var + LN_EPS) * g + b


def _rms_norm(y, g):
    ms = jnp.mean(y * y, axis=-1, keepdims=True)
    return y * lax.rsqrt(ms + RMS_EPS) * g


def _fill_halo_slabs(xs_ref, xp_ref, x_ref, xn_ref, tl):
    i = pl.program_id(1)
    last = pl.num_programs(1) - 1
    prev = jnp.where(i > 0, xp_ref[0], 0.0)
    nxt = jnp.where(i < last, xn_ref[0], 0.0)
    for k in range(xs_ref.shape[0]):
        sl = slice(k * LANES, (k + 1) * LANES)
        xs_ref[k, 0:HALO, :] = prev[:, sl]
        xs_ref[k, HALO:HALO + tl, :] = x_ref[0, :, sl]
        xs_ref[k, HALO + tl:HALO + tl + HALO, :] = nxt[:, sl]


def _load_interleaved(xs_ref):
    nslab, rows, _ = xs_ref.shape
    nv = rows // SUBLANES
    assert rows % SUBLANES == 0 and nv % SUBLANES != 0
    return jnp.concatenate(
        [jnp.concatenate([xs_ref[k, pl.ds(j, SUBLANES, stride=nv), :] for j in range(nv)], axis=0)
         for k in range(nslab)], axis=1)


def _store_natural(os_ref, first, val):
    nv = os_ref.shape[1] // SUBLANES
    for k in range(val.shape[1] // LANES):
        for j in range(nv):
            os_ref[first + k, pl.ds(j, SUBLANES, stride=nv), :] = val[j * SUBLANES:(j + 1) * SUBLANES,
                                                                      k * LANES:(k + 1) * LANES]


def _dwconv3_interleaved(u, w, b):
    head = pltpu.roll(u[-SUBLANES:], 1, axis=0)
    tail = pltpu.roll(u[:SUBLANES], SUBLANES - 1, axis=0)
    up = jnp.concatenate([head, u[:-SUBLANES]], axis=0)
    un = jnp.concatenate([u[SUBLANES:], tail], axis=0)
    return up * w[0:1] + u * w[1:2] + un * w[2:3] + b


def _halo_specs(tl, seq, d):
    nb = tl // HALO
    last = seq // HALO - 1
    return [
        pl.BlockSpec((1, HALO, d), lambda b, i: (b, jnp.maximum(i * nb - 1, 0), 0)),
        pl.BlockSpec((1, tl, d), lambda b, i: (b, i, 0)),
        pl.BlockSpec((1, HALO, d), lambda b, i: (b, jnp.minimum((i + 1) * nb, last), 0)),
    ]


def _in_proj_kernel(xp_ref, x_ref, xn_ref, w_ref, sw_ref, sb_ref,
                    q_ref, k_ref, v_ref, zg_ref, g0_ref, xs_ref, os_ref, *, tl):
    xq = x_ref[0].astype(BF16)
    proj = lambda n: jnp.dot(xq, w_ref[:, n * ATTN_WIDTH:(n + 1) * ATTN_WIDTH], preferred_element_type=F32)
    q_ref[0] = (proj(0) * (HEAD_DIM ** -0.5 * LOG2E)).astype(BF16)
    k_ref[0] = proj(1).astype(BF16)
    v = proj(2).astype(BF16)
    ones = jnp.ones((tl, LANES), BF16)
    v_ref[0] = jnp.concatenate([piece for p in range(N_HEADS // 2)
                                for piece in (v[:, p * LANES:(p + 1) * LANES], ones)], axis=1)
    _fill_halo_slabs(xs_ref, xp_ref, x_ref, xn_ref, tl)
    xh = _load_interleaved(xs_ref).astype(BF16)
    conv = []
    for n in range(3):
        lo = 3 * ATTN_WIDTH + n * HYENA_WIDTH
        u = jnp.dot(xh, w_ref[:, lo:lo + HYENA_WIDTH], preferred_element_type=F32)
        conv.append(_dwconv3_interleaved(u, sw_ref[n], sb_ref[n]))
    nslab = HYENA_WIDTH // LANES
    for first, ref, val in ((0, g0_ref, conv[0]), (nslab, zg_ref, conv[2] * conv[1])):
        _store_natural(os_ref, first, val)
        for k in range(nslab):
            ref[0, :, k * LANES:(k + 1) * LANES] = os_ref[first + k, HALO:HALO + tl, :].astype(BF16)


def _in_proj(x, w_in, short_w, short_b, tl):
    bsz, seq, d = x.shape
    sw = short_w.reshape(3, 3, HYENA_WIDTH).transpose(1, 0, 2)
    sb = short_b.reshape(3, 1, HYENA_WIDTH)
    out = jax.ShapeDtypeStruct((bsz, seq, ATTN_WIDTH), BF16)
    ospec = pl.BlockSpec((1, tl, ATTN_WIDTH), lambda b, i: (b, i, 0))
    vout = jax.ShapeDtypeStruct((bsz, seq, 2 * ATTN_WIDTH), BF16)
    vspec = pl.BlockSpec((1, tl, 2 * ATTN_WIDTH), lambda b, i: (b, i, 0))
    return pl.pallas_call(
        functools.partial(_in_proj_kernel, tl=tl),
        out_shape=(out, out, vout, out, out),
        grid=(bsz, seq // tl),
        in_specs=_halo_specs(tl, seq, d) + [_resident(w_in.shape), _resident(sw.shape), _resident(sb.shape)],
        out_specs=(ospec, ospec, vspec, ospec, ospec),
        scratch_shapes=[pltpu.VMEM((d // LANES, tl + 2 * HALO, LANES), F32),
                        pltpu.VMEM((2 * HYENA_WIDTH // LANES, tl + 2 * HALO, LANES), F32)],
        compiler_params=_params(("parallel", "parallel")),
        name="in_proj",
    )(x, x, x, w_in, sw, sb)


N_PAIRS = N_HEADS // 2
KEY_WIN = NA_ROWS * GRID_W
N_DR = 2 * NA_ROWS - 1
N_DC = 2 * NA_COLS - 1
SOFTMAX_PARTS = 4


def _bias_kernel(rpb_ref, ea_ref, eb_ref, o_ref, t_ref):
    row = lax.broadcasted_iota(jnp.int32, (GRID_W, LANES), 0)
    lane = lax.broadcasted_iota(jnp.int32, (GRID_W, LANES), 1)
    first = lane < GRID_W
    qcol = lane & (GRID_W - 1)
    col_start = jnp.clip(qcol - NA_COLS // 2, 0, GRID_W - NA_COLS)
    valid = (row >= col_start) & (row < col_start + NA_COLS)
    wa = jnp.dot(rpb_ref[0, 0], ea_ref[...], precision=HIGHEST, preferred_element_type=F32)
    wb = jnp.dot(rpb_ref[0, 1], eb_ref[...], precision=HIGHEST, preferred_element_type=F32)

    def rotate_rows(x):
        return pltpu.roll(x, 0, axis=1, stride=1, stride_axis=0)

    for dr in range(N_DR):
        ta = rotate_rows(jnp.broadcast_to(wa[dr:dr + 1], (GRID_W, LANES)))
        tb = rotate_rows(jnp.broadcast_to(wb[dr:dr + 1], (GRID_W, LANES)))
        t_ref[dr] = jnp.where(valid, jnp.where(first, ta, tb) * LOG2E, -jnp.inf)
    for cls in range(NA_ROWS):
        for i in range(NA_ROWS):
            o_ref[0, cls, i * GRID_W:(i + 1) * GRID_W, :] = t_ref[i - cls + NA_ROWS - 1]


def _toeplitz_selectors():
    m = np.arange(LANES)
    dc = np.where(m < GRID_W, (NA_COLS - 1) - np.minimum(m, NA_COLS - 1),
                  np.minimum(LANES - m, NA_COLS - 1) + (NA_COLS - 1))
    ea = np.zeros((LANES, LANES), np.float32)
    ea[dc, m] = 1.0
    ea[:, GRID_W] = 0.0
    eb = np.roll(ea, GRID_W, axis=1)
    return ea, eb


def _attention_bias(rpb):
    ea, eb = _toeplitz_selectors()
    rp = jnp.pad(rpb.astype(F32), ((0, 0), (0, 16 - N_DR), (0, LANES - N_DC))).reshape(N_PAIRS, 2, 16, LANES)
    return pl.pallas_call(
        _bias_kernel,
        out_shape=jax.ShapeDtypeStruct((N_PAIRS, NA_ROWS, KEY_WIN, LANES), F32),
        grid=(N_PAIRS,),
        in_specs=[pl.BlockSpec((1, 2, 16, LANES), lambda p: (p, 0, 0, 0)),
                  _resident(ea.shape), _resident(eb.shape)],
        out_specs=pl.BlockSpec((1, NA_ROWS, KEY_WIN, LANES), lambda p: (p, 0, 0, 0)),
        scratch_shapes=[pltpu.VMEM((N_DR, GRID_W, LANES), F32)],
        compiler_params=_params(("parallel",)),
        name="attn_bias",
    )(rp, jnp.asarray(ea), jnp.asarray(eb))


def _attn_kernel(q_ref, k_ref, vx_ref, bias_ref, g_ref, o_ref, *, rows, rb, unroll):
    jb = pl.program_id(1)
    lane = lax.broadcasted_iota(jnp.int32, (GRID_W, LANES), 1)
    first = lane < HEAD_DIM
    keep = (first.astype(F32).astype(BF16), (~first).astype(F32).astype(BF16))
    nt = (((1,), (1,)), ((), ()))
    tn = (((0,), (0,)), ((), ()))

    def score_stage(j):
        r = jb * rb + j
        start = jnp.clip(r - NA_ROWS // 2, 0, rows - NA_ROWS)
        qoff = pl.multiple_of(j * GRID_W, GRID_W)
        koff = pl.multiple_of(start * GRID_W, GRID_W)
        scores = []
        for p in range(N_PAIRS):
            sl = slice(p * LANES, (p + 1) * LANES)
            qp = q_ref[0, pl.ds(qoff, GRID_W), sl]
            qblk = jnp.concatenate([qp * keep[0], qp * keep[1]], axis=0)
            kp = k_ref[0, pl.ds(koff, KEY_WIN), sl]
            scores.append(lax.dot_general(kp, qblk, nt, preferred_element_type=F32))
        return r - start, qoff, koff, scores

    def output_stage(cls, qoff, koff, scores):
        probs = []
        step = KEY_WIN // SOFTMAX_PARTS
        for p in range(N_PAIRS):
            parts, maxes = [], []
            for lo in range(0, KEY_WIN, step):
                s = scores[p][lo:lo + step] + bias_ref[p, cls, lo:lo + step]
                m_g = jnp.max(s, axis=0, keepdims=True)
                parts.append(jnp.exp2(s - m_g))
                maxes.append(m_g)
            m = functools.reduce(jnp.maximum, maxes)
            probs.append(jnp.concatenate([e.astype(BF16) * jnp.exp2(m_g - m).astype(BF16)
                                          for e, m_g in zip(parts, maxes)], axis=0))
        outs = []
        for p in range(N_PAIRS):
            vx = vx_ref[0, pl.ds(koff, KEY_WIN), 2 * p * LANES:2 * (p + 1) * LANES]
            o = lax.dot_general(probs[p], vx, tn, preferred_element_type=F32)
            oa = o[:GRID_W, :LANES] / o[:GRID_W, LANES:]
            ob = o[GRID_W:, :LANES] / o[GRID_W:, LANES:]
            outs.append(jnp.where(first, oa, ob))
        o_ref[0, pl.ds(qoff, GRID_W), :] = _rms_norm(jnp.concatenate(outs, axis=1), g_ref[...]).astype(BF16)

    def body(jj, carry):
        state = score_stage(jj * unroll)
        for u in range(unroll):
            ahead = score_stage(jj * unroll + u + 1) if u + 1 < unroll else None
            output_stage(*state)
            state = ahead
        return carry

    lax.fori_loop(0, rb // unroll, body, 0)


def _attention(q, k, vx, bias, g_attn, rb, unroll=32):
    bsz, seq, w = q.shape
    unroll = min(unroll, rb)
    assert rb % unroll == 0
    rows = seq // GRID_W
    tq = rb * GRID_W
    return pl.pallas_call(
        functools.partial(_attn_kernel, rows=rows, rb=rb, unroll=unroll),
        out_shape=jax.ShapeDtypeStruct((bsz, seq, w), BF16),
        grid=(bsz, rows // rb),
        in_specs=[
            pl.BlockSpec((1, tq, w), lambda b, i: (b, i, 0)),
            pl.BlockSpec((1, seq, w), lambda b, i: (b, 0, 0)),
            pl.BlockSpec((1, seq, 2 * w), lambda b, i: (b, 0, 0)),
            _resident(bias.shape),
            _resident(g_attn.shape),
        ],
        out_specs=pl.BlockSpec((1, tq, w), lambda b, i: (b, i, 0)),
        compiler_params=_params(("parallel", "arbitrary")),
        name="attention",
    )(q, k, vx, bias, g_attn)


def _filter_kernel(w1t_ref, w1c_ref, w1s_ref, b1_ref, fq_ref, wi_ref, bi_ref, w3_ref,
                   a_ref, d_ref, cb_ref, sb_ref, h_ref, *, seq, tl):
    i = pl.program_id(0)
    rows = tl + 2 * SUBLANES
    half = rows // 2
    band = lax.broadcasted_iota(jnp.int32, (rows, LANES), 1)
    freqs = jnp.where(band < FILTER_BANDS,
                      1e-4 + band.astype(F32) * ((FILTER_BANDS - 1 - 1e-4) / (FILTER_BANDS - 1)), 0.0)
    rad = freqs * (2.0 * math.pi / seq)
    local = lax.broadcasted_iota(jnp.int32, (rows, LANES), 0)

    @pl.when(i == 0)
    def _():
        cb_ref[...] = jnp.cos(local.astype(F32) * rad)
        sb_ref[...] = jnp.sin(local.astype(F32) * rad)

    base = (i * tl).astype(F32) * rad[0:SUBLANES]
    ca = jnp.cos(base)[0:1]
    sa = jnp.sin(base)[0:1]
    cos_ang = ca * cb_ref[...] - sa * sb_ref[...]
    sin_ang = sa * cb_ref[...] + ca * sb_ref[...]
    t = (local + i * tl).astype(F32) * (1.0 / (seq - 1))
    fq = fq_ref[...]
    dot = functools.partial(jnp.dot, precision=HIGHEST, preferred_element_type=F32)
    side = lambda x: jnp.concatenate([x[:half], x[half:]], axis=1)
    left = lax.broadcasted_iota(jnp.int32, (half, LANES), 1) < FILTER_HIDDEN
    t_packed = jnp.where(left, t[:half], t[half:])
    pre = (t_packed * w1t_ref[...] + dot(side(cos_ang), w1c_ref[...]) - dot(side(sin_ang), w1s_ref[...])
           + b1_ref[...])
    h = jnp.sin(fq * pre)
    for n in range(FILTER_INNER):
        h = jnp.sin(fq * (dot(h, wi_ref[n]) + bi_ref[n]))
    chan = lax.broadcasted_iota(jnp.int32, (half, HYENA_WIDTH), 1).astype(F32)
    deltas = jnp.abs(MIN_DECAY + chan * ((MAX_DECAY - MIN_DECAY) / (HYENA_WIDTH - 1)))
    for part in range(2):
        rws = slice(part * half, (part + 1) * half)
        pos = lax.broadcasted_iota(jnp.int32, (half, HYENA_WIDTH), 0) + (i * tl + part * half)
        decay = jnp.exp(-(pos.astype(F32) * (1.0 / (seq - 1))) * deltas)
        for n in range(2):
            cols = slice(n * HYENA_WIDTH, (n + 1) * HYENA_WIDTH)
            h_ref[rws, cols] = dot(h, w3_ref[part, :, cols]) * decay
    h_fwd = h_ref[0:tl, 0:HYENA_WIDTH]
    lag = lax.broadcasted_iota(jnp.int32, (tl, HYENA_WIDTH), 0) + (i * tl + 1)
    h_bwd = jnp.where(lag < seq, h_ref[pl.ds(1, tl), HYENA_WIDTH:2 * HYENA_WIDTH], 0.0)
    a_ref[...] = h_fwd + h_bwd
    d_ref[...] = h_bwd - h_fwd


def _filter_taps(seq, w1, b1, freq, w_inner, b_inner, w3, tl=512):
    pad = LANES - FILTER_BANDS
    twice = lambda v: jnp.tile(v.reshape(1, -1), (1, 2))
    blockdiag = lambda w: jnp.kron(jnp.eye(2, dtype=w.dtype), w)
    w1c = blockdiag(jnp.pad(w1[1:1 + FILTER_BANDS], ((0, pad), (0, 0))))
    w1s = blockdiag(jnp.pad(w1[1 + FILTER_BANDS:], ((0, pad), (0, 0))))
    zero = jnp.zeros_like(w3)
    args = (twice(w1[0]), w1c, w1s, twice(b1), twice(freq),
            jnp.stack([blockdiag(w_inner[n]) for n in range(FILTER_INNER)]),
            jnp.stack([twice(b_inner[n]) for n in range(FILTER_INNER)]),
            jnp.stack([jnp.concatenate([w3, zero]), jnp.concatenate([zero, w3])]))
    out = jax.ShapeDtypeStruct((seq, HYENA_WIDTH), F32)
    ospec = pl.BlockSpec((tl, HYENA_WIDTH), lambda i: (i, 0))
    return pl.pallas_call(
        functools.partial(_filter_kernel, seq=seq, tl=tl),
        out_shape=(out, out),
        grid=(seq // tl,),
        in_specs=[_resident(a.shape) for a in args],
        out_specs=(ospec, ospec),
        scratch_shapes=[pltpu.VMEM((tl + 2 * SUBLANES, LANES), F32),
                        pltpu.VMEM((tl + 2 * SUBLANES, LANES), F32),
                        pltpu.VMEM((tl + 2 * SUBLANES, 2 * HYENA_WIDTH), F32)],
        compiler_params=_params(("arbitrary",)),
        name="hyena_filter",
    )(*args)


P_BLK = 512
F_HALF = P_BLK // 2


def _pair_coefficients(seq):
    s1n = seq // P_BLK
    g = np.arange(s1n)[:, None]
    s = np.arange(s1n)[None, :]
    ang = np.pi * (2 * g + 1) * s / (2 * s1n)
    return jnp.asarray(np.cos(ang), F32), jnp.asarray(np.sin(ang), F32)


def _dft_kernel(t_ref, tt_ref, cb_ref, sb_ref, *, seq):
    g = pl.program_id(0)
    s1n = seq // P_BLK

    @pl.when(g == 0)
    def _():
        f2 = lax.broadcasted_iota(jnp.int32, (F_HALF, P_BLK), 0)
        s_odd = 2 * lax.broadcasted_iota(jnp.int32, (F_HALF, P_BLK), 1) + 1
        beta = ((f2 * s_odd) & (2 * P_BLK - 1)).astype(F32) * (math.pi / P_BLK)
        cb_ref[...] = jnp.cos(beta)
        sb_ref[...] = jnp.sin(beta)

    cb = cb_ref[...]
    sb = sb_ref[...]
    s_odd = 2 * lax.broadcasted_iota(jnp.int32, (SUBLANES, P_BLK), 1) + 1

    def block(f1):
        alpha = ((s_odd * (2 * f1 + 1)) & (8 * seq - 1)).astype(F32) * (math.pi / (4 * seq))
        ca = jnp.cos(alpha)[0:1]
        sa = jnp.sin(alpha)[0:1]
        return ca * cb - sa * sb, sa * cb + ca * sb

    mra, mia = block(g)
    mrb, mib = block(2 * s1n - 1 - g)
    for r, (left, right) in enumerate(((mra, -mia), (mia, mra), (mrb, mib), (mib, -mrb))):
        rows = slice(r * F_HALF, (r + 1) * F_HALF)
        t_ref[0, rows, :P_BLK] = left.astype(BF16)
        t_ref[0, rows, P_BLK:] = right.astype(BF16)
        tt_ref[0, :P_BLK, rows] = left.T.astype(BF16)
        tt_ref[0, P_BLK:, rows] = right.T.astype(BF16)


def _dft_matrices(seq):
    s1n = seq // P_BLK
    mat = jax.ShapeDtypeStruct((s1n, 2 * P_BLK, 2 * P_BLK), BF16)
    spec = pl.BlockSpec((1, 2 * P_BLK, 2 * P_BLK), lambda g: (g, 0, 0))
    return pl.pallas_call(
        functools.partial(_dft_kernel, seq=seq),
        out_shape=(mat, mat),
        grid=(s1n,),
        out_specs=(spec, spec),
        scratch_shapes=[pltpu.VMEM((F_HALF, P_BLK), F32), pltpu.VMEM((F_HALF, P_BLK), F32)],
        compiler_params=_params(("arbitrary",)),
        name="dft_matrices",
    )()


_SMEM = pl.BlockSpec(memory_space=pltpu.SMEM)
_COL_HALVES = (slice(0, HYENA_WIDTH // 2), slice(HYENA_WIDTH // 2, HYENA_WIDTH))


ROW_CHUNK = 64


def _stacked_blocks(cr_ref, sr_ref, g, block, s1n):
    re, im = [], []
    for r in range(0, P_BLK, ROW_CHUNK):
        ar = block(0, r)
        bi = None
        for s in range(1, s1n):
            b = block(s, r)
            ar = ar + cr_ref[g, s] * b
            bi = sr_ref[g, s] * b if bi is None else bi + sr_ref[g, s] * b
        re.append(ar)
        im.append(bi)
    return jnp.concatenate(re + im, axis=0).astype(BF16)


def _spectrum_kernel(cr_ref, sr_ref, t_ref, a_ref, d_ref, g_ref, *, seq):
    g = pl.program_id(0)
    s1n = seq // P_BLK
    time_block = lambda ref: (lambda s, r: ref[s * P_BLK + r:s * P_BLK + r + ROW_CHUNK, :])
    t_re = jnp.concatenate([t_ref[0, 0:F_HALF], t_ref[0, P_BLK:P_BLK + F_HALF]], axis=0)
    t_im = jnp.concatenate([t_ref[0, F_HALF:P_BLK], t_ref[0, P_BLK + F_HALF:2 * P_BLK]], axis=0)
    ka = jnp.dot(t_re, _stacked_blocks(cr_ref, sr_ref, g, time_block(a_ref), s1n),
                 preferred_element_type=F32)
    kd = jnp.dot(t_im, _stacked_blocks(cr_ref, sr_ref, g, time_block(d_ref), s1n),
                 preferred_element_type=F32)
    f2 = lax.broadcasted_iota(jnp.int32, (F_HALF, LANES), 0)
    reps = HYENA_WIDTH // LANES
    for blk, f1 in ((0, g), (1, 2 * s1n - 1 - g)):
        lo = blk * P_BLK
        phi = (2 * (f1 + 2 * s1n * f2) + 1).astype(F32) * (math.pi / (4 * seq))
        c = pltpu.repeat(jnp.cos(phi), reps, axis=1) * (1.0 / seq)
        s = pltpu.repeat(jnp.sin(phi), reps, axis=1) * (1.0 / seq)
        for r in range(0, F_HALF, ROW_CHUNK):
            rows = slice(blk * F_HALF + r, blk * F_HALF + r + ROW_CHUNK)
            cc, ss = c[r:r + ROW_CHUNK], s[r:r + ROW_CHUNK]
            g_ref[0, lo + r:lo + r + ROW_CHUNK] = cc * ka[rows] - ss * kd[rows]
            g_ref[0, lo + F_HALF + r:lo + F_HALF + r + ROW_CHUNK] = cc * kd[rows] + ss * ka[rows]


def _filter_spectrum(cr, sr, t, a, d):
    seq = a.shape[0]
    s1n = seq // P_BLK
    return pl.pallas_call(
        functools.partial(_spectrum_kernel, seq=seq),
        out_shape=jax.ShapeDtypeStruct((s1n, 2 * P_BLK, HYENA_WIDTH), F32),
        grid=(s1n,),
        in_specs=[_SMEM, _SMEM, pl.BlockSpec((1, 2 * P_BLK, 2 * P_BLK), lambda g: (g, 0, 0)),
                  _resident(a.shape), _resident(d.shape)],
        out_specs=pl.BlockSpec((1, 2 * P_BLK, HYENA_WIDTH), lambda g: (g, 0, 0)),
        compiler_params=_params(("parallel",)),
        name="hyena_spectrum",
    )(cr, sr, t, a, d)


def _hy_fwd_kernel(cr_ref, sr_ref, t_ref, g_ref, z_ref, y_ref, *, s1n, nb):
    g = pl.program_id(0)
    for bi in range(nb):
        for cols in _COL_HALVES:
            x = _stacked_blocks(
                cr_ref, sr_ref, g,
                lambda s, r: z_ref[bi, s * P_BLK + r:s * P_BLK + r + ROW_CHUNK, cols].astype(F32), s1n)
            res = jnp.dot(t_ref[0], x, preferred_element_type=F32)
            for lo in range(0, 2 * P_BLK, P_BLK):
                for r in range(lo, lo + F_HALF, ROW_CHUNK):
                    re_rows = slice(r, r + ROW_CHUNK)
                    im_rows = slice(r + F_HALF, r + F_HALF + ROW_CHUNK)
                    zr, wi = res[re_rows], res[im_rows]
                    gr, gi = g_ref[0, re_rows, cols], g_ref[0, im_rows, cols]
                    y_ref[bi, 0, re_rows, cols] = (gr * zr + gi * wi).astype(BF16)
                    y_ref[bi, 0, im_rows, cols] = (gr * wi - gi * zr).astype(BF16)


def _hy_inv_kernel(cr_ref, sr_ref, tt_ref, y_ref, z_ref, g0_ref, fb_ref, gn_ref, o_ref,
                   acc_ref, zs_ref, g0s_ref, *, s1n, nb):
    g = pl.program_id(1)

    @pl.when(g == 0)
    def _():
        acc_ref[...] = jnp.zeros_like(acc_ref)

    tblk = pl.ds(pl.multiple_of(g * P_BLK, P_BLK), P_BLK)
    zs_ref[:, tblk, :] = z_ref[...]
    g0s_ref[:, tblk, :] = g0_ref[...]
    for bi in range(nb):
        for cols in _COL_HALVES:
            uv = jnp.dot(tt_ref[g], y_ref[bi, 0, :, cols], preferred_element_type=F32)
            for r in range(0, P_BLK, ROW_CHUNK):
                u = uv[r:r + ROW_CHUNK]
                vn = uv[P_BLK + r:P_BLK + r + ROW_CHUNK]
                acc_ref[bi, r:r + ROW_CHUNK, cols] += u
                for t1 in range(1, s1n):
                    rows = slice(t1 * P_BLK + r, t1 * P_BLK + r + ROW_CHUNK)
                    acc_ref[bi, rows, cols] += cr_ref[g, t1] * u + sr_ref[g, t1] * vn

    @pl.when(g == s1n - 1)
    def _():
        for bi in range(nb):
            for r in range(0, s1n * P_BLK, ROW_CHUNK):
                rows = slice(r, r + ROW_CHUNK)
                y = ((acc_ref[bi, rows, :] + zs_ref[bi, rows, :].astype(F32) * fb_ref[...])
                     * g0s_ref[bi, rows, :].astype(F32))
                o_ref[bi, rows, :] = _rms_norm(y, gn_ref[...]).astype(BF16)


_INV_STATE_BUDGET = 28 * 1024 * 1024


def _hyena_conv(zg, g0, cr, sr, t, tt, gspec, fbias, g_hyena):
    bsz, seq, w = zg.shape
    s1n = seq // P_BLK
    nb = max(n for n in (4, 2, 1) if bsz % n == 0)
    mat = pl.BlockSpec((1, 2 * P_BLK, 2 * P_BLK), lambda g, b: (g, 0, 0))
    spec = pl.BlockSpec((1, 2 * P_BLK, w), lambda g, b: (g, 0, 0))
    y = pl.pallas_call(
        functools.partial(_hy_fwd_kernel, s1n=s1n, nb=nb),
        out_shape=jax.ShapeDtypeStruct((bsz, s1n, 2 * P_BLK, w), BF16),
        grid=(s1n, bsz // nb),
        in_specs=[_SMEM, _SMEM, mat, spec, pl.BlockSpec((nb, seq, w), lambda g, b: (b, 0, 0))],
        out_specs=pl.BlockSpec((nb, 1, 2 * P_BLK, w), lambda g, b: (b, g, 0, 0)),
        compiler_params=_params(("parallel", "parallel")),
        name="hyena_fwd",
    )(cr, sr, t, gspec, zg)
    state_bytes = seq * w * (4 + 2 + 2 + 2 * 2)
    nb = 2 if bsz % 2 == 0 and 2 * state_bytes <= _INV_STATE_BUDGET else 1
    blk = pl.BlockSpec((nb, P_BLK, w), lambda b, g: (b, g, 0))
    return pl.pallas_call(
        functools.partial(_hy_inv_kernel, s1n=s1n, nb=nb),
        out_shape=jax.ShapeDtypeStruct((bsz, seq, w), BF16),
        grid=(bsz // nb, s1n),
        in_specs=[_SMEM, _SMEM, _resident(tt.shape),
                  pl.BlockSpec((nb, 1, 2 * P_BLK, w), lambda b, g: (b, g, 0, 0)),
                  blk, blk, _resident(fbias.shape), _resident(g_hyena.shape)],
        out_specs=pl.BlockSpec((nb, seq, w), lambda b, g: (b, 0, 0)),
        scratch_shapes=[pltpu.VMEM((nb, seq, w), F32), pltpu.VMEM((nb, seq, w), BF16),
                        pltpu.VMEM((nb, seq, w), BF16)],
        compiler_params=_params(("parallel", "arbitrary")),
        name="hyena_inv",
    )(cr, sr, tt, y, zg, g0, fbias, g_hyena)


MERGE_TILE = 2048
MERGE_ROW_CHUNKS = 8


def _merge_kernel(ma_ref, mh_ref, x_ref, wa_ref, wh_ref, g_ref, b_ref, o_ref):
    tl = x_ref.shape[1]
    chunk = tl // MERGE_ROW_CHUNKS
    for lo in range(0, tl, chunk):
        rows = slice(lo, lo + chunk)
        y = (jnp.dot(ma_ref[0, rows], wa_ref[...], preferred_element_type=F32)
             + jnp.dot(mh_ref[0, rows], wh_ref[...], preferred_element_type=F32))
        o_ref[0, rows] = _layer_norm(ALPHA * x_ref[0, rows] + y, g_ref[...], b_ref[...])


def _merge(ma, mh, x, w_out, g, b, tl):
    bsz, seq, d = x.shape
    wa, wh = w_out[:ATTN_WIDTH], w_out[ATTN_WIDTH:]
    half = pl.BlockSpec((1, tl, ATTN_WIDTH), lambda bb, i: (bb, i, 0))
    full = pl.BlockSpec((1, tl, d), lambda bb, i: (bb, i, 0))
    return pl.pallas_call(
        _merge_kernel,
        out_shape=jax.ShapeDtypeStruct(x.shape, F32),
        grid=(bsz, seq // tl),
        in_specs=[half, half, full, _resident(wa.shape), _resident(wh.shape),
                  _resident(g.shape), _resident(b.shape)],
        out_specs=full,
        compiler_params=_params(("parallel", "parallel")),
        name="merge_ln1",
    )(ma, mh, x, wa, wh, g, b)


FF_CHUNK = 256
N_FF_CHUNKS = D_FF // FF_CHUNK


def _ffn_kernel(xp_ref, x_ref, xn_ref, wi_ref, cw_ref, cb_ref, wo_ref, g_ref, b_ref,
                o_ref, xs_ref, hid_ref, os_ref, *, tl):
    _fill_halo_slabs(xs_ref, xp_ref, x_ref, xn_ref, tl)
    xh = _load_interleaved(xs_ref).astype(BF16)
    for j in range(N_FF_CHUNKS):
        conv = []
        for part in range(2):
            cols = slice(part * D_FF + j * FF_CHUNK, part * D_FF + (j + 1) * FF_CHUNK)
            u = jnp.dot(xh, wi_ref[:, cols], preferred_element_type=F32)
            conv.append(_dwconv3_interleaved(u, cw_ref[:, cols], cb_ref[:, cols]))
        gate = conv[1]
        gelu = 0.5 * gate * (1.0 + lax.erf(gate * (2.0 ** -0.5)))
        hid_ref[:, j * FF_CHUNK:(j + 1) * FF_CHUNK] = (conv[0] * gelu).astype(BF16)
    y = jnp.dot(hid_ref[...], wo_ref[...], preferred_element_type=F32)
    out = _layer_norm(ALPHA * _load_interleaved(xs_ref) + y, g_ref[...], b_ref[...])
    _store_natural(os_ref, 0, out)
    for k in range(os_ref.shape[0]):
        o_ref[0, :, k * LANES:(k + 1) * LANES] = os_ref[k, HALO:HALO + tl, :]


def _ffn(x1, w_in, conv_w, conv_b, w_out, g, b, tl):
    bsz, seq, d = x1.shape
    cb = conv_b.reshape(1, -1)
    return pl.pallas_call(
        functools.partial(_ffn_kernel, tl=tl),
        out_shape=jax.ShapeDtypeStruct(x1.shape, F32),
        grid=(bsz, seq // tl),
        in_specs=_halo_specs(tl, seq, d) + [_resident(w_in.shape), _resident(conv_w.shape), _resident(cb.shape),
                                            _resident(w_out.shape), _resident(g.shape), _resident(b.shape)],
        out_specs=pl.BlockSpec((1, tl, d), lambda bb, i: (bb, i, 0)),
        scratch_shapes=[pltpu.VMEM((d // LANES, tl + 2 * HALO, LANES), F32),
                        pltpu.VMEM((tl + 2 * HALO, D_FF), BF16),
                        pltpu.VMEM((d // LANES, tl + 2 * HALO, LANES), F32)],
        compiler_params=_params(("parallel", "parallel")),
        name="conv_ffn",
    )(x1, x1, x1, w_in, conv_w, cb, w_out, g, b)


def _encoder_layer(x, p, tl=1024, rb=32):
    seq = x.shape[1]
    rows = seq // GRID_W
    assert seq % tl == 0 and seq % P_BLK == 0 and seq % GRID_W == 0
    assert rows >= NA_ROWS and rows % rb == 0
    q, k, v, zg, g0 = _in_proj(x, p["w_in"], p["short_w"], p["short_b"], tl)
    ma = _attention(q, k, v, p["bias"], p["g_attn"], rb)
    a, d = _filter_taps(seq, p["filt_w1"], p["filt_b1"], p["filt_freq"], p["filt_w_inner"],
                        p["filt_b_inner"], p["filt_w3"])
    cr, sr = _pair_coefficients(seq)
    t, tt = _dft_matrices(seq)
    gspec = _filter_spectrum(cr, sr, t, a, d)
    mh = _hyena_conv(zg, g0, cr, sr, t, tt, gspec, p["filt_bias"], p["g_hyena"])
    x1 = _merge(ma, mh, x, p["w_out"], p["ln1_g"], p["ln1_b"], min(MERGE_TILE, seq))
    return _ffn(x1, p["ffn_w_in"], p["ffn_conv_w"], p["ffn_conv_b"], p["ffn_w_out"],
                p["ln2_g"], p["ln2_b"], tl)


def kernel(x_prompt, x_sample, w_in, short_w, short_b, rpb, filt_w1, filt_b1, filt_freq, filt_w_inner,
           filt_b_inner, filt_w3, filt_bias, g_attn, g_hyena, w_out, ln1_g, ln1_b, ffn_w_in, ffn_conv_w,
           ffn_conv_b, ffn_w_out, ln2_g, ln2_b):
    assert w_in.shape[0] == DEPTH == 1
    row = lambda a: a[0].reshape(1, -1)
    p = dict(
        w_in=w_in[0].astype(BF16), short_w=short_w[0], short_b=short_b[0],
        bias=_attention_bias(rpb[0]), g_attn=row(g_attn),
        filt_w1=filt_w1[0], filt_b1=filt_b1[0], filt_freq=filt_freq[0], filt_w_inner=filt_w_inner[0],
        filt_b_inner=filt_b_inner[0], filt_w3=filt_w3[0], filt_bias=filt_bias[0], g_hyena=row(g_hyena),
        w_out=w_out[0].astype(BF16), ln1_g=row(ln1_g), ln1_b=row(ln1_b),
        ffn_w_in=ffn_w_in[0].astype(BF16), ffn_conv_w=ffn_conv_w[0], ffn_conv_b=ffn_conv_b[0],
        ffn_w_out=ffn_w_out[0].astype(BF16), ln2_g=row(ln2_g), ln2_b=row(ln2_b),
    )
    return (_encoder_layer(x_prompt, p), _encoder_layer(x_sample, p))
```

```python
import functools
import math

import numpy as np
import jax
import jax.numpy as jnp
from jax import lax
from jax.experimental import pallas as pl
from jax.experimental.pallas import tpu as pltpu

F32 = jnp.float32
BF16 = jnp.bfloat16

D_MODEL = 1024
GRID_W = 64
ATTN_WIDTH = 512
HYENA_WIDTH = 512
HEAD_DIM = 64
N_HEADS = ATTN_WIDTH // HEAD_DIM
NA_ROWS = 8
NA_COLS = 16
FILTER_EMB = 33
FILTER_BANDS = (FILTER_EMB - 1) // 2
FILTER_HIDDEN = 64
FILTER_INNER = 2
MAX_DECAY = math.log(1e-2) / 0.3
MIN_DECAY = math.log(1e-2) / 1.5
D_FF = 2816
DEPTH = 1
ALPHA = (2 * DEPTH) ** 0.25
LN_EPS = 1e-5
RMS_EPS = 1e-6
LOG2E = math.log2(math.e)

SUBLANES = 8
LANES = 128
HALO = SUBLANES
VMEM_LIMIT = 56 * 1024 * 1024

HIGHEST = lax.Precision.HIGHEST


def _params(sem):
    return pltpu.CompilerParams(dimension_semantics=sem, vmem_limit_bytes=VMEM_LIMIT)


def _resident(shape):
    nd = len(shape)
    return pl.BlockSpec(shape, lambda *_: (0,) * nd, pipeline_mode=pl.Buffered(1))


def _layer_norm(y, g, b):
    mu = jnp.mean(y, axis=-1, keepdims=True)
    yc = y - mu
    var = jnp.mean(yc * yc, axis=-1, keepdims=True)
    return yc * lax.rsqrt(var + LN_EPS) * g + b


def _rms_norm(y, g):
    ms = jnp.mean(y * y, axis=-1, keepdims=True)
    return y * lax.rsqrt(ms + RMS_EPS) * g


def _fill_halo_slabs(xs_ref, xp_ref, x_ref, xn_ref, tl):
    i = pl.program_id(1)
    last = pl.num_programs(1) - 1
    prev = jnp.where(i > 0, xp_ref[0], 0.0)
    nxt = jnp.where(i < last, xn_ref[0], 0.0)
    for k in range(xs_ref.shape[0]):
        sl = slice(k * LANES, (k + 1) * LANES)
        xs_ref[k, 0:HALO, :] = prev[:, sl]
        xs_ref[k, HALO:HALO + tl, :] = x_ref[0, :, sl]
        xs_ref[k, HALO + tl:HALO + tl + HALO, :] = nxt[:, sl]


def _load_interleaved(xs_ref):
    nslab, rows, _ = xs_ref.shape
    nv = rows // SUBLANES
    assert rows % SUBLANES == 0 and nv % SUBLANES != 0
    return jnp.concatenate(
        [jnp.concatenate([xs_ref[k, pl.ds(j, SUBLANES, stride=nv), :] for j in range(nv)], axis=0)
         for k in range(nslab)], axis=1)


def _store_natural(os_ref, first, val):
    nv = os_ref.shape[1] // SUBLANES
    for k in range(val.shape[1] // LANES):
        for j in range(nv):
            os_ref[first + k, pl.ds(j, SUBLANES, stride=nv), :] = val[j * SUBLANES:(j + 1) * SUBLANES,
                                                                      k * LANES:(k + 1) * LANES]


def _dwconv3_interleaved(u, w, b):
    head = pltpu.roll(u[-SUBLANES:], 1, axis=0)
    tail = pltpu.roll(u[:SUBLANES], SUBLANES - 1, axis=0)
    up = jnp.concatenate([head, u[:-SUBLANES]], axis=0)
    un = jnp.concatenate([u[SUBLANES:], tail], axis=0)
    return up * w[0:1] + u * w[1:2] + un * w[2:3] + b


def _halo_specs(tl, seq, d):
    nb = tl // HALO
    last = seq // HALO - 1
    return [
        pl.BlockSpec((1, HALO, d), lambda b, i: (b, jnp.maximum(i * nb - 1, 0), 0)),
        pl.BlockSpec((1, tl, d), lambda b, i: (b, i, 0)),
        pl.BlockSpec((1, HALO, d), lambda b, i: (b, jnp.minimum((i + 1) * nb, last), 0)),
    ]


def _in_proj_kernel(xp_ref, x_ref, xn_ref, w_ref, sw_ref, sb_ref,
                    q_ref, k_ref, v_ref, zg_ref, g0_ref, xs_ref, os_ref, *, tl):
    xq = x_ref[0].astype(BF16)
    proj = lambda n: jnp.dot(xq, w_ref[:, n * ATTN_WIDTH:(n + 1) * ATTN_WIDTH], preferred_element_type=F32)
    q_ref[0] = (proj(0) * (HEAD_DIM ** -0.5 * LOG2E)).astype(BF16)
    k_ref[0] = proj(1).astype(BF16)
    v = proj(2).astype(BF16)
    ones = jnp.ones((tl, LANES), BF16)
    v_ref[0] = jnp.concatenate([piece for p in range(N_HEADS // 2)
                                for piece in (v[:, p * LANES:(p + 1) * LANES], ones)], axis=1)
    _fill_halo_slabs(xs_ref, xp_ref, x_ref, xn_ref, tl)
    xh = _load_interleaved(xs_ref).astype(BF16)
    conv = []
    for n in range(3):
        lo = 3 * ATTN_WIDTH + n * HYENA_WIDTH
        u = jnp.dot(xh, w_ref[:, lo:lo + HYENA_WIDTH], preferred_element_type=F32)
        conv.append(_dwconv3_interleaved(u, sw_ref[n], sb_ref[n]))
    nslab = HYENA_WIDTH // LANES
    for first, ref, val in ((0, g0_ref, conv[0]), (nslab, zg_ref, conv[2] * conv[1])):
        _store_natural(os_ref, first, val)
        for k in range(nslab):
            ref[0, :, k * LANES:(k + 1) * LANES] = os_ref[first + k, HALO:HALO + tl, :].astype(BF16)


def _in_proj(x, w_in, short_w, short_b, tl):
    bsz, seq, d = x.shape
    sw = short_w.reshape(3, 3, HYENA_WIDTH).transpose(1, 0, 2)
    sb = short_b.reshape(3, 1, HYENA_WIDTH)
    out = jax.ShapeDtypeStruct((bsz, seq, ATTN_WIDTH), BF16)
    ospec = pl.BlockSpec((1, tl, ATTN_WIDTH), lambda b, i: (b, i, 0))
    vout = jax.ShapeDtypeStruct((bsz, seq, 2 * ATTN_WIDTH), BF16)
    vspec = pl.BlockSpec((1, tl, 2 * ATTN_WIDTH), lambda b, i: (b, i, 0))
    return pl.pallas_call(
        functools.partial(_in_proj_kernel, tl=tl),
        out_shape=(out, out, vout, out, out),
        grid=(bsz, seq // tl),
        in_specs=_halo_specs(tl, seq, d) + [_resident(w_in.shape), _resident(sw.shape), _resident(sb.shape)],
        out_specs=(ospec, ospec, vspec, ospec, ospec),
        scratch_shapes=[pltpu.VMEM((d // LANES, tl + 2 * HALO, LANES), F32),
                        pltpu.VMEM((2 * HYENA_WIDTH // LANES, tl + 2 * HALO, LANES), F32)],
        compiler_params=_params(("parallel", "parallel")),
        name="in_proj",
    )(x, x, x, w_in, sw, sb)


N_PAIRS = N_HEADS // 2
KEY_WIN = NA_ROWS * GRID_W
N_DR = 2 * NA_ROWS - 1
N_DC = 2 * NA_COLS - 1
SOFTMAX_PARTS = 4


def _bias_kernel(rpb_ref, ea_ref, eb_ref, o_ref, t_ref):
    row = lax.broadcasted_iota(jnp.int32, (GRID_W, LANES), 0)
    lane = lax.broadcasted_iota(jnp.int32, (GRID_W, LANES), 1)
    first = lane < GRID_W
    qcol = lane & (GRID_W - 1)
    col_start = jnp.clip(qcol - NA_COLS // 2, 0, GRID_W - NA_COLS)
    valid = (row >= col_start) & (row < col_start + NA_COLS)
    wa = jnp.dot(rpb_ref[0, 0], ea_ref[...], precision=HIGHEST, preferred_element_type=F32)
    wb = jnp.dot(rpb_ref[0, 1], eb_ref[...], precision=HIGHEST, preferred_element_type=F32)

    def rotate_rows(x):
        return pltpu.roll(x, 0, axis=1, stride=1, stride_axis=0)

    for dr in range(N_DR):
        ta = rotate_rows(jnp.broadcast_to(wa[dr:dr + 1], (GRID_W, LANES)))
        tb = rotate_rows(jnp.broadcast_to(wb[dr:dr + 1], (GRID_W, LANES)))
        t_ref[dr] = jnp.where(valid, jnp.where(first, ta, tb) * LOG2E, -jnp.inf)
    for cls in range(NA_ROWS):
        for i in range(NA_ROWS):
            o_ref[0, cls, i * GRID_W:(i + 1) * GRID_W, :] = t_ref[i - cls + NA_ROWS - 1]


def _toeplitz_selectors():
    m = np.arange(LANES)
    dc = np.where(m < GRID_W, (NA_COLS - 1) - np.minimum(m, NA_COLS - 1),
                  np.minimum(LANES - m, NA_COLS - 1) + (NA_COLS - 1))
    ea = np.zeros((LANES, LANES), np.float32)
    ea[dc, m] = 1.0
    ea[:, GRID_W] = 0.0
    eb = np.roll(ea, GRID_W, axis=1)
    return ea, eb


def _attention_bias(rpb):
    ea, eb = _toeplitz_selectors()
    rp = jnp.pad(rpb.astype(F32), ((0, 0), (0, 16 - N_DR), (0, LANES - N_DC))).reshape(N_PAIRS, 2, 16, LANES)
    return pl.pallas_call(
        _bias_kernel,
        out_shape=jax.ShapeDtypeStruct((N_PAIRS, NA_ROWS, KEY_WIN, LANES), F32),
        grid=(N_PAIRS,),
        in_specs=[pl.BlockSpec((1, 2, 16, LANES), lambda p: (p, 0, 0, 0)),
                  _resident(ea.shape), _resident(eb.shape)],
        out_specs=pl.BlockSpec((1, NA_ROWS, KEY_WIN, LANES), lambda p: (p, 0, 0, 0)),
        scratch_shapes=[pltpu.VMEM((N_DR, GRID_W, LANES), F32)],
        compiler_params=_params(("parallel",)),
        name="attn_bias",
    )(rp, jnp.asarray(ea), jnp.asarray(eb))


def _attn_kernel(q_ref, k_ref, vx_ref, bias_ref, g_ref, o_ref, *, rows, rb, unroll):
    jb = pl.program_id(1)
    lane = lax.broadcasted_iota(jnp.int32, (GRID_W, LANES), 1)
    first = lane < HEAD_DIM
    keep = (first.astype(F32).astype(BF16), (~first).astype(F32).astype(BF16))
    nt = (((1,), (1,)), ((), ()))
    tn = (((0,), (0,)), ((), ()))

    def score_stage(j):
        r = jb * rb + j
        start = jnp.clip(r - NA_ROWS // 2, 0, rows - NA_ROWS)
        qoff = pl.multiple_of(j * GRID_W, GRID_W)
        koff = pl.multiple_of(start * GRID_W, GRID_W)
        scores = []
        for p in range(N_PAIRS):
            sl = slice(p * LANES, (p + 1) * LANES)
            qp = q_ref[0, pl.ds(qoff, GRID_W), sl]
            qblk = jnp.concatenate([qp * keep[0], qp * keep[1]], axis=0)
            kp = k_ref[0, pl.ds(koff, KEY_WIN), sl]
            scores.append(lax.dot_general(kp, qblk, nt, preferred_element_type=F32))
        return r - start, qoff, koff, scores

    def output_stage(cls, qoff, koff, scores):
        probs = []
        step = KEY_WIN // SOFTMAX_PARTS
        for p in range(N_PAIRS):
            parts, maxes = [], []
            for lo in range(0, KEY_WIN, step):
                s = scores[p][lo:lo + step] + bias_ref[p, cls, lo:lo + step]
                m_g = jnp.max(s, axis=0, keepdims=True)
                parts.append(jnp.exp2(s - m_g))
                maxes.append(m_g)
            m = functools.reduce(jnp.maximum, maxes)
            probs.append(jnp.concatenate([e.astype(BF16) * jnp.exp2(m_g - m).astype(BF16)
                                          for e, m_g in zip(parts, maxes)], axis=0))
        outs = []
        for p in range(N_PAIRS):
            vx = vx_ref[0, pl.ds(koff, KEY_WIN), 2 * p * LANES:2 * (p + 1) * LANES]
            o = lax.dot_general(probs[p], vx, tn, preferred_element_type=F32)
            oa = o[:GRID_W, :LANES] / o[:GRID_W, LANES:]
            ob = o[GRID_W:, :LANES] / o[GRID_W:, LANES:]
            outs.append(jnp.where(first, oa, ob))
        o_ref[0, pl.ds(qoff, GRID_W), :] = _rms_norm(jnp.concatenate(outs, axis=1), g_ref[...]).astype(BF16)

    def body(jj, carry):
        state = score_stage(jj * unroll)
        for u in range(unroll):
            ahead = score_stage(jj * unroll + u + 1) if u + 1 < unroll else None
            output_stage(*state)
            state = ahead
        return carry

    lax.fori_loop(0, rb // unroll, body, 0)


def _attention(q, k, vx, bias, g_attn, rb, unroll=32):
    bsz, seq, w = q.shape
    unroll = min(unroll, rb)
    assert rb % unroll == 0
    rows = seq // GRID_W
    tq = rb * GRID_W
    return pl.pallas_call(
        functools.partial(_attn_kernel, rows=rows, rb=rb, unroll=unroll),
        out_shape=jax.ShapeDtypeStruct((bsz, seq, w), BF16),
        grid=(bsz, rows // rb),
        in_specs=[
            pl.BlockSpec((1, tq, w), lambda b, i: (b, i, 0)),
            pl.BlockSpec((1, seq, w), lambda b, i: (b, 0, 0)),
            pl.BlockSpec((1, seq, 2 * w), lambda b, i: (b, 0, 0)),
            _resident(bias.shape),
            _resident(g_attn.shape),
        ],
        out_specs=pl.BlockSpec((1, tq, w), lambda b, i: (b, i, 0)),
        compiler_params=_params(("parallel", "arbitrary")),
        name="attention",
    )(q, k, vx, bias, g_attn)


def _filter_kernel(w1t_ref, w1c_ref, w1s_ref, b1_ref, fq_ref, wi_ref, bi_ref, w3_ref,
                   a_ref, d_ref, cb_ref, sb_ref, h_ref, *, seq, tl):
    i = pl.program_id(0)
    rows = tl + 2 * SUBLANES
    half = rows // 2
    band = lax.broadcasted_iota(jnp.int32, (rows, LANES), 1)
    freqs = jnp.where(band < FILTER_BANDS,
                      1e-4 + band.astype(F32) * ((FILTER_BANDS - 1 - 1e-4) / (FILTER_BANDS - 1)), 0.0)
    rad = freqs * (2.0 * math.pi / seq)
    local = lax.broadcasted_iota(jnp.int32, (rows, LANES), 0)

    @pl.when(i == 0)
    def _():
        cb_ref[...] = jnp.cos(local.astype(F32) * rad)
        sb_ref[...] = jnp.sin(local.astype(F32) * rad)

    base = (i * tl).astype(F32) * rad[0:SUBLANES]
    ca = jnp.cos(base)[0:1]
    sa = jnp.sin(base)[0:1]
    cos_ang = ca * cb_ref[...] - sa * sb_ref[...]
    sin_ang = sa * cb_ref[...] + ca * sb_ref[...]
    t = (local + i * tl).astype(F32) * (1.0 / (seq - 1))
    fq = fq_ref[...]
    dot = functools.partial(jnp.dot, precision=HIGHEST, preferred_element_type=F32)
    side = lambda x: jnp.concatenate([x[:half], x[half:]], axis=1)
    left = lax.broadcasted_iota(jnp.int32, (half, LANES), 1) < FILTER_HIDDEN
    t_packed = jnp.where(left, t[:half], t[half:])
    pre = (t_packed * w1t_ref[...] + dot(side(cos_ang), w1c_ref[...]) - dot(side(sin_ang), w1s_ref[...])
           + b1_ref[...])
    h = jnp.sin(fq * pre)
    for n in range(FILTER_INNER):
        h = jnp.sin(fq * (dot(h, wi_ref[n]) + bi_ref[n]))
    chan = lax.broadcasted_iota(jnp.int32, (half, HYENA_WIDTH), 1).astype(F32)
    deltas = jnp.abs(MIN_DECAY + chan * ((MAX_DECAY - MIN_DECAY) / (HYENA_WIDTH - 1)))
    for part in range(2):
        rws = slice(part * half, (part + 1) * half)
        pos = lax.broadcasted_iota(jnp.int32, (half, HYENA_WIDTH), 0) + (i * tl + part * half)
        decay = jnp.exp(-(pos.astype(F32) * (1.0 / (seq - 1))) * deltas)
        for n in range(2):
            cols = slice(n * HYENA_WIDTH, (n + 1) * HYENA_WIDTH)
            h_ref[rws, cols] = dot(h, w3_ref[part, :, cols]) * decay
    h_fwd = h_ref[0:tl, 0:HYENA_WIDTH]
    lag = lax.broadcasted_iota(jnp.int32, (tl, HYENA_WIDTH), 0) + (i * tl + 1)
    h_bwd = jnp.where(lag < seq, h_ref[pl.ds(1, tl), HYENA_WIDTH:2 * HYENA_WIDTH], 0.0)
    a_ref[...] = h_fwd + h_bwd
    d_ref[...] = h_bwd - h_fwd


def _filter_taps(seq, w1, b1, freq, w_inner, b_inner, w3, tl=512):
    pad = LANES - FILTER_BANDS
    twice = lambda v: jnp.tile(v.reshape(1, -1), (1, 2))
    blockdiag = lambda w: jnp.kron(jnp.eye(2, dtype=w.dtype), w)
    w1c = blockdiag(jnp.pad(w1[1:1 + FILTER_BANDS], ((0, pad), (0, 0))))
    w1s = blockdiag(jnp.pad(w1[1 + FILTER_BANDS:], ((0, pad), (0, 0))))
    zero = jnp.zeros_like(w3)
    args = (twice(w1[0]), w1c, w1s, twice(b1), twice(freq),
            jnp.stack([blockdiag(w_inner[n]) for n in range(FILTER_INNER)]),
            jnp.stack([twice(b_inner[n]) for n in range(FILTER_INNER)]),
            jnp.stack([jnp.concatenate([w3, zero]), jnp.concatenate([zero, w3])]))
    out = jax.ShapeDtypeStruct((seq, HYENA_WIDTH), F32)
    ospec = pl.BlockSpec((tl, HYENA_WIDTH), lambda i: (i, 0))
    return pl.pallas_call(
        functools.partial(_filter_kernel, seq=seq, tl=tl),
        out_shape=(out, out),
        grid=(seq // tl,),
        in_specs=[_resident(a.shape) for a in args],
        out_specs=(ospec, ospec),
        scratch_shapes=[pltpu.VMEM((tl + 2 * SUBLANES, LANES), F32),
                        pltpu.VMEM((tl + 2 * SUBLANES, LANES), F32),
                        pltpu.VMEM((tl + 2 * SUBLANES, 2 * HYENA_WIDTH), F32)],
        compiler_params=_params(("arbitrary",)),
        name="hyena_filter",
    )(*args)


P_BLK = 512
F_HALF = P_BLK // 2


def _pair_coefficients(seq):
    s1n = seq // P_BLK
    g = np.arange(s1n)[:, None]
    s = np.arange(s1n)[None, :]
    ang = np.pi * (2 * g + 1) * s / (2 * s1n)
    return jnp.asarray(np.cos(ang), F32), jnp.asarray(np.sin(ang), F32)


def _dft_kernel(t_ref, tt_ref, cb_ref, sb_ref, *, seq):
    g = pl.program_id(0)
    s1n = seq // P_BLK

    @pl.when(g == 0)
    def _():
        f2 = lax.broadcasted_iota(jnp.int32, (F_HALF, P_BLK), 0)
        s_odd = 2 * lax.broadcasted_iota(jnp.int32, (F_HALF, P_BLK), 1) + 1
        beta = ((f2 * s_odd) & (2 * P_BLK - 1)).astype(F32) * (math.pi / P_BLK)
        cb_ref[...] = jnp.cos(beta)
        sb_ref[...] = jnp.sin(beta)

    cb = cb_ref[...]
    sb = sb_ref[...]
    s_odd = 2 * lax.broadcasted_iota(jnp.int32, (SUBLANES, P_BLK), 1) + 1

    def block(f1):
        alpha = ((s_odd * (2 * f1 + 1)) & (8 * seq - 1)).astype(F32) * (math.pi / (4 * seq))
        ca = jnp.cos(alpha)[0:1]
        sa = jnp.sin(alpha)[0:1]
        return ca * cb - sa * sb, sa * cb + ca * sb

    mra, mia = block(g)
    mrb, mib = block(2 * s1n - 1 - g)
    for r, (left, right) in enumerate(((mra, -mia), (mia, mra), (mrb, mib), (mib, -mrb))):
        rows = slice(r * F_HALF, (r + 1) * F_HALF)
        t_ref[0, rows, :P_BLK] = left.astype(BF16)
        t_ref[0, rows, P_BLK:] = right.astype(BF16)
        tt_ref[0, :P_BLK, rows] = left.T.astype(BF16)
        tt_ref[0, P_BLK:, rows] = right.T.astype(BF16)


def _dft_matrices(seq):
    s1n = seq // P_BLK
    mat = jax.ShapeDtypeStruct((s1n, 2 * P_BLK, 2 * P_BLK), BF16)
    spec = pl.BlockSpec((1, 2 * P_BLK, 2 * P_BLK), lambda g: (g, 0, 0))
    return pl.pallas_call(
        functools.partial(_dft_kernel, seq=seq),
        out_shape=(mat, mat),
        grid=(s1n,),
        out_specs=(spec, spec),
        scratch_shapes=[pltpu.VMEM((F_HALF, P_BLK), F32), pltpu.VMEM((F_HALF, P_BLK), F32)],
        compiler_params=_params(("arbitrary",)),
        name="dft_matrices",
    )()


_SMEM = pl.BlockSpec(memory_space=pltpu.SMEM)
_COL_HALVES = (slice(0, HYENA_WIDTH // 2), slice(HYENA_WIDTH // 2, HYENA_WIDTH))


ROW_CHUNK = 64


def _stacked_blocks(cr_ref, sr_ref, g, block, s1n):
    re, im = [], []
    for r in range(0, P_BLK, ROW_CHUNK):
        ar = block(0, r)
        bi = None
        for s in range(1, s1n):
            b = block(s, r)
            ar = ar + cr_ref[g, s] * b
            bi = sr_ref[g, s] * b if bi is None else bi + sr_ref[g, s] * b
        re.append(ar)
        im.append(bi)
    return jnp.concatenate(re + im, axis=0).astype(BF16)


def _spectrum_kernel(cr_ref, sr_ref, t_ref, a_ref, d_ref, g_ref, *, seq):
    g = pl.program_id(0)
    s1n = seq // P_BLK
    time_block = lambda ref: (lambda s, r: ref[s * P_BLK + r:s * P_BLK + r + ROW_CHUNK, :])
    t_re = jnp.concatenate([t_ref[0, 0:F_HALF], t_ref[0, P_BLK:P_BLK + F_HALF]], axis=0)
    t_im = jnp.concatenate([t_ref[0, F_HALF:P_BLK], t_ref[0, P_BLK + F_HALF:2 * P_BLK]], axis=0)
    ka = jnp.dot(t_re, _stacked_blocks(cr_ref, sr_ref, g, time_block(a_ref), s1n),
                 preferred_element_type=F32)
    kd = jnp.dot(t_im, _stacked_blocks(cr_ref, sr_ref, g, time_block(d_ref), s1n),
                 preferred_element_type=F32)
    f2 = lax.broadcasted_iota(jnp.int32, (F_HALF, LANES), 0)
    reps = HYENA_WIDTH // LANES
    for blk, f1 in ((0, g), (1, 2 * s1n - 1 - g)):
        lo = blk * P_BLK
        phi = (2 * (f1 + 2 * s1n * f2) + 1).astype(F32) * (math.pi / (4 * seq))
        c = pltpu.repeat(jnp.cos(phi), reps, axis=1) * (1.0 / seq)
        s = pltpu.repeat(jnp.sin(phi), reps, axis=1) * (1.0 / seq)
        for r in range(0, F_HALF, ROW_CHUNK):
            rows = slice(blk * F_HALF + r, blk * F_HALF + r + ROW_CHUNK)
            cc, ss = c[r:r + ROW_CHUNK], s[r:r + ROW_CHUNK]
            g_ref[0, lo + r:lo + r + ROW_CHUNK] = cc * ka[rows] - ss * kd[rows]
            g_ref[0, lo + F_HALF + r:lo + F_HALF + r + ROW_CHUNK] = cc * kd[rows] + ss * ka[rows]


def _filter_spectrum(cr, sr, t, a, d):
    seq = a.shape[0]
    s1n = seq // P_BLK
    return pl.pallas_call(
        functools.partial(_spectrum_kernel, seq=seq),
        out_shape=jax.ShapeDtypeStruct((s1n, 2 * P_BLK, HYENA_WIDTH), F32),
        grid=(s1n,),
        in_specs=[_SMEM, _SMEM, pl.BlockSpec((1, 2 * P_BLK, 2 * P_BLK), lambda g: (g, 0, 0)),
                  _resident(a.shape), _resident(d.shape)],
        out_specs=pl.BlockSpec((1, 2 * P_BLK, HYENA_WIDTH), lambda g: (g, 0, 0)),
        compiler_params=_params(("parallel",)),
        name="hyena_spectrum",
    )(cr, sr, t, a, d)


def _hy_fwd_kernel(cr_ref, sr_ref, t_ref, g_ref, z_ref, y_ref, *, s1n, nb):
    g = pl.program_id(0)
    for bi in range(nb):
        for cols in _COL_HALVES:
            x = _stacked_blocks(
                cr_ref, sr_ref, g,
                lambda s, r: z_ref[bi, s * P_BLK + r:s * P_BLK + r + ROW_CHUNK, cols].astype(F32), s1n)
            res = jnp.dot(t_ref[0], x, preferred_element_type=F32)
            for lo in range(0, 2 * P_BLK, P_BLK):
                for r in range(lo, lo + F_HALF, ROW_CHUNK):
                    re_rows = slice(r, r + ROW_CHUNK)
                    im_rows = slice(r + F_HALF, r + F_HALF + ROW_CHUNK)
                    zr, wi = res[re_rows], res[im_rows]
                    gr, gi = g_ref[0, re_rows, cols], g_ref[0, im_rows, cols]
                    y_ref[bi, 0, re_rows, cols] = (gr * zr + gi * wi).astype(BF16)
                    y_ref[bi, 0, im_rows, cols] = (gr * wi - gi * zr).astype(BF16)


def _hy_inv_kernel(cr_ref, sr_ref, tt_ref, y_ref, z_ref, g0_ref, fb_ref, o_ref,
                   acc_ref, zs_ref, g0s_ref, *, s1n, nb):
    g = pl.program_id(1)

    @pl.when(g == 0)
    def _():
        acc_ref[...] = jnp.zeros_like(acc_ref)

    tblk = pl.ds(pl.multiple_of(g * P_BLK, P_BLK), P_BLK)
    zs_ref[:, tblk, :] = z_ref[...]
    g0s_ref[:, tblk, :] = g0_ref[...]
    for bi in range(nb):
        for cols in _COL_HALVES:
            uv = jnp.dot(tt_ref[g], y_ref[bi, 0, :, cols], preferred_element_type=F32)
            for r in range(0, P_BLK, ROW_CHUNK):
                u = uv[r:r + ROW_CHUNK]
                vn = uv[P_BLK + r:P_BLK + r + ROW_CHUNK]
                acc_ref[bi, r:r + ROW_CHUNK, cols] += u
                for t1 in range(1, s1n):
                    rows = slice(t1 * P_BLK + r, t1 * P_BLK + r + ROW_CHUNK)
                    acc_ref[bi, rows, cols] += cr_ref[g, t1] * u + sr_ref[g, t1] * vn

    @pl.when(g == s1n - 1)
    def _():
        for bi in range(nb):
            for r in range(0, s1n * P_BLK, ROW_CHUNK):
                rows = slice(r, r + ROW_CHUNK)
                y = ((acc_ref[bi, rows, :] + zs_ref[bi, rows, :].astype(F32) * fb_ref[...])
                     * g0s_ref[bi, rows, :].astype(F32))
                o_ref[bi, rows, :] = y.astype(BF16)


_INV_STATE_BUDGET = 28 * 1024 * 1024


def _hyena_conv(zg, g0, cr, sr, t, tt, gspec, fbias):
    bsz, seq, w = zg.shape
    s1n = seq // P_BLK
    nb = max(n for n in (4, 2, 1) if bsz % n == 0)
    mat = pl.BlockSpec((1, 2 * P_BLK, 2 * P_BLK), lambda g, b: (g, 0, 0))
    spec = pl.BlockSpec((1, 2 * P_BLK, w), lambda g, b: (g, 0, 0))
    y = pl.pallas_call(
        functools.partial(_hy_fwd_kernel, s1n=s1n, nb=nb),
        out_shape=jax.ShapeDtypeStruct((bsz, s1n, 2 * P_BLK, w), BF16),
        grid=(s1n, bsz // nb),
        in_specs=[_SMEM, _SMEM, mat, spec, pl.BlockSpec((nb, seq, w), lambda g, b: (b, 0, 0))],
        out_specs=pl.BlockSpec((nb, 1, 2 * P_BLK, w), lambda g, b: (b, g, 0, 0)),
        compiler_params=_params(("parallel", "parallel")),
        name="hyena_fwd",
    )(cr, sr, t, gspec, zg)
    state_bytes = seq * w * (4 + 2 + 2 + 2 * 2)
    nb = 2 if bsz % 2 == 0 and 2 * state_bytes <= _INV_STATE_BUDGET else 1
    blk = pl.BlockSpec((nb, P_BLK, w), lambda b, g: (b, g, 0))
    return pl.pallas_call(
        functools.partial(_hy_inv_kernel, s1n=s1n, nb=nb),
        out_shape=jax.ShapeDtypeStruct((bsz, seq, w), BF16),
        grid=(bsz // nb, s1n),
        in_specs=[_SMEM, _SMEM, _resident(tt.shape),
                  pl.BlockSpec((nb, 1, 2 * P_BLK, w), lambda b, g: (b, g, 0, 0)),
                  blk, blk, _resident(fbias.shape)],
        out_specs=pl.BlockSpec((nb, seq, w), lambda b, g: (b, 0, 0)),
        scratch_shapes=[pltpu.VMEM((nb, seq, w), F32), pltpu.VMEM((nb, seq, w), BF16),
                        pltpu.VMEM((nb, seq, w), BF16)],
        compiler_params=_params(("parallel", "arbitrary")),
        name="hyena_inv",
    )(cr, sr, tt, y, zg, g0, fbias)


MERGE_TILE = 2048
MERGE_ROW_CHUNKS = 8


def _merge_kernel(ma_ref, hy_ref, x_ref, wa_ref, wh_ref, gh_ref, g_ref, b_ref, o_ref):
    tl = x_ref.shape[1]
    chunk = tl // MERGE_ROW_CHUNKS
    for lo in range(0, tl, chunk):
        rows = slice(lo, lo + chunk)
        mh = _rms_norm(hy_ref[0, rows].astype(F32), gh_ref[...]).astype(BF16)
        y = (jnp.dot(ma_ref[0, rows], wa_ref[...], preferred_element_type=F32)
             + jnp.dot(mh, wh_ref[...], preferred_element_type=F32))
        o_ref[0, rows] = _layer_norm(ALPHA * x_ref[0, rows] + y, g_ref[...], b_ref[...])


def _merge(ma, hy, g_hyena, x, w_out, g, b, tl):
    bsz, seq, d = x.shape
    wa, wh = w_out[:ATTN_WIDTH], w_out[ATTN_WIDTH:]
    half = pl.BlockSpec((1, tl, ATTN_WIDTH), lambda bb, i: (bb, i, 0))
    full = pl.BlockSpec((1, tl, d), lambda bb, i: (bb, i, 0))
    return pl.pallas_call(
        _merge_kernel,
        out_shape=jax.ShapeDtypeStruct(x.shape, F32),
        grid=(bsz, seq // tl),
        in_specs=[half, half, full, _resident(wa.shape), _resident(wh.shape),
                  _resident(g_hyena.shape), _resident(g.shape), _resident(b.shape)],
        out_specs=full,
        compiler_params=_params(("parallel", "parallel")),
        name="merge_ln1",
    )(ma, hy, x, wa, wh, g_hyena, g, b)


FF_CHUNK = 256
N_FF_CHUNKS = D_FF // FF_CHUNK


def _ffn_kernel(xp_ref, x_ref, xn_ref, wi_ref, cw_ref, cb_ref, wo_ref, g_ref, b_ref,
                o_ref, xs_ref, hid_ref, os_ref, *, tl):
    _fill_halo_slabs(xs_ref, xp_ref, x_ref, xn_ref, tl)
    xh = _load_interleaved(xs_ref).astype(BF16)
    for j in range(N_FF_CHUNKS):
        conv = []
        for part in range(2):
            cols = slice(part * D_FF + j * FF_CHUNK, part * D_FF + (j + 1) * FF_CHUNK)
            u = jnp.dot(xh, wi_ref[:, cols], preferred_element_type=F32)
            conv.append(_dwconv3_interleaved(u, cw_ref[:, cols], cb_ref[:, cols]))
        gate = conv[1]
        gelu = 0.5 * gate * (1.0 + lax.erf(gate * (2.0 ** -0.5)))
        hid_ref[:, j * FF_CHUNK:(j + 1) * FF_CHUNK] = (conv[0] * gelu).astype(BF16)
    y = jnp.dot(hid_ref[...], wo_ref[...], preferred_element_type=F32)
    out = _layer_norm(ALPHA * _load_interleaved(xs_ref) + y, g_ref[...], b_ref[...])
    _store_natural(os_ref, 0, out)
    for k in range(os_ref.shape[0]):
        o_ref[0, :, k * LANES:(k + 1) * LANES] = os_ref[k, HALO:HALO + tl, :]


def _ffn(x1, w_in, conv_w, conv_b, w_out, g, b, tl):
    bsz, seq, d = x1.shape
    cb = conv_b.reshape(1, -1)
    return pl.pallas_call(
        functools.partial(_ffn_kernel, tl=tl),
        out_shape=jax.ShapeDtypeStruct(x1.shape, F32),
        grid=(bsz, seq // tl),
        in_specs=_halo_specs(tl, seq, d) + [_resident(w_in.shape), _resident(conv_w.shape), _resident(cb.shape),
                                            _resident(w_out.shape), _resident(g.shape), _resident(b.shape)],
        out_specs=pl.BlockSpec((1, tl, d), lambda bb, i: (bb, i, 0)),
        scratch_shapes=[pltpu.VMEM((d // LANES, tl + 2 * HALO, LANES), F32),
                        pltpu.VMEM((tl + 2 * HALO, D_FF), BF16),
                        pltpu.VMEM((d // LANES, tl + 2 * HALO, LANES), F32)],
        compiler_params=_params(("parallel", "parallel")),
        name="conv_ffn",
    )(x1, x1, x1, w_in, conv_w, cb, w_out, g, b)


def _encoder_layer(x, p, tl=1024, rb=32):
    seq = x.shape[1]
    rows = seq // GRID_W
    assert seq % tl == 0 and seq % P_BLK == 0 and seq % GRID_W == 0
    assert rows >= NA_ROWS and rows % rb == 0
    q, k, v, zg, g0 = _in_proj(x, p["w_in"], p["short_w"], p["short_b"], tl)
    ma = _attention(q, k, v, p["bias"], p["g_attn"], rb)
    a, d = _filter_taps(seq, p["filt_w1"], p["filt_b1"], p["filt_freq"], p["filt_w_inner"],
                        p["filt_b_inner"], p["filt_w3"])
    cr, sr = _pair_coefficients(seq)
    t, tt = _dft_matrices(seq)
    gspec = _filter_spectrum(cr, sr, t, a, d)
    hy = _hyena_conv(zg, g0, cr, sr, t, tt, gspec, p["filt_bias"])
    x1 = _merge(ma, hy, p["g_hyena"], x, p["w_out"], p["ln1_g"], p["ln1_b"], min(MERGE_TILE, seq))
    return _ffn(x1, p["ffn_w_in"], p["ffn_conv_w"], p["ffn_conv_b"], p["ffn_w_out"],
                p["ln2_g"], p["ln2_b"], tl)


def kernel(x_prompt, x_sample, w_in, short_w, short_b, rpb, filt_w1, filt_b1, filt_freq, filt_w_inner,
           filt_b_inner, filt_w3, filt_bias, g_attn, g_hyena, w_out, ln1_g, ln1_b, ffn_w_in, ffn_conv_w,
           ffn_conv_b, ffn_w_out, ln2_g, ln2_b):
    assert w_in.shape[0] == DEPTH == 1
    row = lambda a: a[0].reshape(1, -1)
    p = dict(
        w_in=w_in[0].astype(BF16), short_w=short_w[0], short_b=short_b[0],
        bias=_attention_bias(rpb[0]), g_attn=row(g_attn),
        filt_w1=filt_w1[0], filt_b1=filt_b1[0], filt_freq=filt_freq[0], filt_w_inner=filt_w_inner[0],
        filt_b_inner=filt_b_inner[0], filt_w3=filt_w3[0], filt_bias=filt_bias[0], g_hyena=row(g_hyena),
        w_out=w_out[0].astype(BF16), ln1_g=row(ln1_g), ln1_b=row(ln1_b),
        ffn_w_in=ffn_w_in[0].astype(BF16), ffn_conv_w=ffn_conv_w[0], ffn_conv_b=ffn_conv_b[0],
        ffn_w_out=ffn_w_out[0].astype(BF16), ln2_g=row(ln2_g), ln2_b=row(ln2_b),
    )
    return (_encoder_layer(x_prompt, p), _encoder_layer(x_sample, p))
```

```python
import functools
import math

import numpy as np
import jax
import jax.numpy as jnp
from jax import lax
from jax.experimental import pallas as pl
from jax.experimental.pallas import tpu as pltpu

F32 = jnp.float32
BF16 = jnp.bfloat16

D_MODEL = 1024
GRID_W = 64
ATTN_WIDTH = 512
HYENA_WIDTH = 512
HEAD_DIM = 64
N_HEADS = ATTN_WIDTH // HEAD_DIM
NA_ROWS = 8
NA_COLS = 16
FILTER_EMB = 33
FILTER_BANDS = (FILTER_EMB - 1) // 2
FILTER_HIDDEN = 64
FILTER_INNER = 2
MAX_DECAY = math.log(1e-2) / 0.3
MIN_DECAY = math.log(1e-2) / 1.5
D_FF = 2816
DEPTH = 1
ALPHA = (2 * DEPTH) ** 0.25
LN_EPS = 1e-5
RMS_EPS = 1e-6
LOG2E = math.log2(math.e)

SUBLANES = 8
LANES = 128
HALO = SUBLANES
VMEM_LIMIT = 56 * 1024 * 1024

HIGHEST = lax.Precision.HIGHEST


def _params(sem):
    return pltpu.CompilerParams(dimension_semantics=sem, vmem_limit_bytes=VMEM_LIMIT)


def _resident(shape):
    nd = len(shape)
    return pl.BlockSpec(shape, lambda *_: (0,) * nd, pipeline_mode=pl.Buffered(1))


def _layer_norm(y, g, b):
    mu = jnp.mean(y, axis=-1, keepdims=True)
    yc = y - mu
    var = jnp.mean(yc * yc, axis=-1, keepdims=True)
    return yc * lax.rsqrt(var + LN_EPS) * g + b


def _rms_norm(y, g):
    ms = jnp.mean(y * y, axis=-1, keepdims=True)
    return y * lax.rsqrt(ms + RMS_EPS) * g


def _fill_halo_slabs(xs_ref, xp_ref, x_ref, xn_ref, tl):
    i = pl.program_id(1)
    last = pl.num_programs(1) - 1
    prev = jnp.where(i > 0, xp_ref[0], 0.0)
    nxt = jnp.where(i < last, xn_ref[0], 0.0)
    for k in range(xs_ref.shape[0]):
        sl = slice(k * LANES, (k + 1) * LANES)
        xs_ref[k, 0:HALO, :] = prev[:, sl]
        xs_ref[k, HALO:HALO + tl, :] = x_ref[0, :, sl]
        xs_ref[k, HALO + tl:HALO + tl + HALO, :] = nxt[:, sl]


def _load_interleaved(xs_ref):
    nslab, rows, _ = xs_ref.shape
    nv = rows // SUBLANES
    assert rows % SUBLANES == 0 and nv % SUBLANES != 0
    return jnp.concatenate(
        [jnp.concatenate([xs_ref[k, pl.ds(j, SUBLANES, stride=nv), :] for j in range(nv)], axis=0)
         for k in range(nslab)], axis=1)


def _store_natural(os_ref, first, val):
    nv = os_ref.shape[1] // SUBLANES
    for k in range(val.shape[1] // LANES):
        for j in range(nv):
            os_ref[first + k, pl.ds(j, SUBLANES, stride=nv), :] = val[j * SUBLANES:(j + 1) * SUBLANES,
                                                                      k * LANES:(k + 1) * LANES]


def _dwconv3_interleaved(u, w, b):
    head = pltpu.roll(u[-SUBLANES:], 1, axis=0)
    tail = pltpu.roll(u[:SUBLANES], SUBLANES - 1, axis=0)
    up = jnp.concatenate([head, u[:-SUBLANES]], axis=0)
    un = jnp.concatenate([u[SUBLANES:], tail], axis=0)
    return up * w[0:1] + u * w[1:2] + un * w[2:3] + b


def _halo_specs(tl, seq, d):
    nb = tl // HALO
    last = seq // HALO - 1
    return [
        pl.BlockSpec((1, HALO, d), lambda b, i: (b, jnp.maximum(i * nb - 1, 0), 0)),
        pl.BlockSpec((1, tl, d), lambda b, i: (b, i, 0)),
        pl.BlockSpec((1, HALO, d), lambda b, i: (b, jnp.minimum((i + 1) * nb, last), 0)),
    ]


def _in_proj_kernel(xp_ref, x_ref, xn_ref, w_ref, sw_ref, sb_ref,
                    q_ref, k_ref, v_ref, zg_ref, g0_ref, xs_ref, os_ref, *, tl):
    xq = x_ref[0].astype(BF16)
    proj = lambda n: jnp.dot(xq, w_ref[:, n * ATTN_WIDTH:(n + 1) * ATTN_WIDTH], preferred_element_type=F32)
    q_ref[0] = (proj(0) * (HEAD_DIM ** -0.5 * LOG2E)).astype(BF16)
    k_ref[0] = proj(1).astype(BF16)
    v = proj(2).astype(BF16)
    ones = jnp.ones((tl, LANES), BF16)
    v_ref[0] = jnp.concatenate([piece for p in range(N_HEADS // 2)
                                for piece in (v[:, p * LANES:(p + 1) * LANES], ones)], axis=1)
    _fill_halo_slabs(xs_ref, xp_ref, x_ref, xn_ref, tl)
    xh = _load_interleaved(xs_ref).astype(BF16)
    conv = []
    for n in range(3):
        lo = 3 * ATTN_WIDTH + n * HYENA_WIDTH
        u = jnp.dot(xh, w_ref[:, lo:lo + HYENA_WIDTH], preferred_element_type=F32)
        conv.append(_dwconv3_interleaved(u, sw_ref[n], sb_ref[n]))
    nslab = HYENA_WIDTH // LANES
    for first, ref, val in ((0, g0_ref, conv[0]), (nslab, zg_ref, conv[2] * conv[1])):
        _store_natural(os_ref, first, val)
        for k in range(nslab):
            ref[0, :, k * LANES:(k + 1) * LANES] = os_ref[first + k, HALO:HALO + tl, :].astype(BF16)


def _in_proj(x, w_in, short_w, short_b, tl):
    bsz, seq, d = x.shape
    sw = short_w.reshape(3, 3, HYENA_WIDTH).transpose(1, 0, 2)
    sb = short_b.reshape(3, 1, HYENA_WIDTH)
    out = jax.ShapeDtypeStruct((bsz, seq, ATTN_WIDTH), BF16)
    ospec = pl.BlockSpec((1, tl, ATTN_WIDTH), lambda b, i: (b, i, 0))
    vout = jax.ShapeDtypeStruct((bsz, seq, 2 * ATTN_WIDTH), BF16)
    vspec = pl.BlockSpec((1, tl, 2 * ATTN_WIDTH), lambda b, i: (b, i, 0))
    return pl.pallas_call(
        functools.partial(_in_proj_kernel, tl=tl),
        out_shape=(out, out, vout, out, out),
        grid=(bsz, seq // tl),
        in_specs=_halo_specs(tl, seq, d) + [_resident(w_in.shape), _resident(sw.shape), _resident(sb.shape)],
        out_specs=(ospec, ospec, vspec, ospec, ospec),
        scratch_shapes=[pltpu.VMEM((d // LANES, tl + 2 * HALO, LANES), F32),
                        pltpu.VMEM((2 * HYENA_WIDTH // LANES, tl + 2 * HALO, LANES), F32)],
        compiler_params=_params(("parallel", "parallel")),
        name="in_proj",
    )(x, x, x, w_in, sw, sb)


N_PAIRS = N_HEADS // 2
KEY_WIN = NA_ROWS * GRID_W
N_DR = 2 * NA_ROWS - 1
N_DC = 2 * NA_COLS - 1
SOFTMAX_PARTS = 4


def _bias_kernel(rpb_ref, ea_ref, eb_ref, o_ref, t_ref):
    row = lax.broadcasted_iota(jnp.int32, (GRID_W, LANES), 0)
    lane = lax.broadcasted_iota(jnp.int32, (GRID_W, LANES), 1)
    first = lane < GRID_W
    qcol = lane & (GRID_W - 1)
    col_start = jnp.clip(qcol - NA_COLS // 2, 0, GRID_W - NA_COLS)
    valid = (row >= col_start) & (row < col_start + NA_COLS)
    wa = jnp.dot(rpb_ref[0, 0], ea_ref[...], precision=HIGHEST, preferred_element_type=F32)
    wb = jnp.dot(rpb_ref[0, 1], eb_ref[...], precision=HIGHEST, preferred_element_type=F32)

    def rotate_rows(x):
        return pltpu.roll(x, 0, axis=1, stride=1, stride_axis=0)

    for dr in range(N_DR):
        ta = rotate_rows(jnp.broadcast_to(wa[dr:dr + 1], (GRID_W, LANES)))
        tb = rotate_rows(jnp.broadcast_to(wb[dr:dr + 1], (GRID_W, LANES)))
        t_ref[dr] = jnp.where(valid, jnp.where(first, ta, tb) * LOG2E, -jnp.inf)
    for cls in range(NA_ROWS):
        for i in range(NA_ROWS):
            o_ref[0, cls, i * GRID_W:(i + 1) * GRID_W, :] = t_ref[i - cls + NA_ROWS - 1]


def _toeplitz_selectors():
    m = np.arange(LANES)
    dc = np.where(m < GRID_W, (NA_COLS - 1) - np.minimum(m, NA_COLS - 1),
                  np.minimum(LANES - m, NA_COLS - 1) + (NA_COLS - 1))
    ea = np.zeros((LANES, LANES), np.float32)
    ea[dc, m] = 1.0
    ea[:, GRID_W] = 0.0
    eb = np.roll(ea, GRID_W, axis=1)
    return ea, eb


def _attention_bias(rpb):
    ea, eb = _toeplitz_selectors()
    rp = jnp.pad(rpb.astype(F32), ((0, 0), (0, 16 - N_DR), (0, LANES - N_DC))).reshape(N_PAIRS, 2, 16, LANES)
    return pl.pallas_call(
        _bias_kernel,
        out_shape=jax.ShapeDtypeStruct((N_PAIRS, NA_ROWS, KEY_WIN, LANES), F32),
        grid=(N_PAIRS,),
        in_specs=[pl.BlockSpec((1, 2, 16, LANES), lambda p: (p, 0, 0, 0)),
                  _resident(ea.shape), _resident(eb.shape)],
        out_specs=pl.BlockSpec((1, NA_ROWS, KEY_WIN, LANES), lambda p: (p, 0, 0, 0)),
        scratch_shapes=[pltpu.VMEM((N_DR, GRID_W, LANES), F32)],
        compiler_params=_params(("parallel",)),
        name="attn_bias",
    )(rp, jnp.asarray(ea), jnp.asarray(eb))


def _attn_kernel(q_ref, k_ref, vx_ref, bias_ref, o_ref, *, rows, rb, unroll):
    jb = pl.program_id(1)
    lane = lax.broadcasted_iota(jnp.int32, (GRID_W, LANES), 1)
    first = lane < HEAD_DIM
    keep = (first.astype(F32).astype(BF16), (~first).astype(F32).astype(BF16))
    nt = (((1,), (1,)), ((), ()))
    tn = (((0,), (0,)), ((), ()))

    def score_stage(j):
        r = jb * rb + j
        start = jnp.clip(r - NA_ROWS // 2, 0, rows - NA_ROWS)
        qoff = pl.multiple_of(j * GRID_W, GRID_W)
        koff = pl.multiple_of(start * GRID_W, GRID_W)
        scores = []
        for p in range(N_PAIRS):
            sl = slice(p * LANES, (p + 1) * LANES)
            qp = q_ref[0, pl.ds(qoff, GRID_W), sl]
            qblk = jnp.concatenate([qp * keep[0], qp * keep[1]], axis=0)
            kp = k_ref[0, pl.ds(koff, KEY_WIN), sl]
            scores.append(lax.dot_general(kp, qblk, nt, preferred_element_type=F32))
        return r - start, qoff, koff, scores

    def output_stage(cls, qoff, koff, scores):
        probs = []
        step = KEY_WIN // SOFTMAX_PARTS
        for p in range(N_PAIRS):
            parts, maxes = [], []
            for lo in range(0, KEY_WIN, step):
                s = scores[p][lo:lo + step] + bias_ref[p, cls, lo:lo + step]
                m_g = jnp.max(s, axis=0, keepdims=True)
                parts.append(jnp.exp2(s - m_g))
                maxes.append(m_g)
            m = functools.reduce(jnp.maximum, maxes)
            probs.append(jnp.concatenate([e.astype(BF16) * jnp.exp2(m_g - m).astype(BF16)
                                          for e, m_g in zip(parts, maxes)], axis=0))
        outs = []
        for p in range(N_PAIRS):
            vx = vx_ref[0, pl.ds(koff, KEY_WIN), 2 * p * LANES:2 * (p + 1) * LANES]
            o = lax.dot_general(probs[p], vx, tn, preferred_element_type=F32)
            oa = o[:GRID_W, :LANES] / o[:GRID_W, LANES:]
            ob = o[GRID_W:, :LANES] / o[GRID_W:, LANES:]
            outs.append(jnp.where(first, oa, ob))
        o_ref[0, pl.ds(qoff, GRID_W), :] = jnp.concatenate(outs, axis=1).astype(BF16)

    def body(jj, carry):
        state = score_stage(jj * unroll)
        for u in range(unroll):
            ahead = score_stage(jj * unroll + u + 1) if u + 1 < unroll else None
            output_stage(*state)
            state = ahead
        return carry

    lax.fori_loop(0, rb // unroll, body, 0)


def _attention(q, k, vx, bias, rb, unroll=32):
    bsz, seq, w = q.shape
    unroll = min(unroll, rb)
    assert rb % unroll == 0
    rows = seq // GRID_W
    tq = rb * GRID_W
    return pl.pallas_call(
        functools.partial(_attn_kernel, rows=rows, rb=rb, unroll=unroll),
        out_shape=jax.ShapeDtypeStruct((bsz, seq, w), BF16),
        grid=(bsz, rows // rb),
        in_specs=[
            pl.BlockSpec((1, tq, w), lambda b, i: (b, i, 0)),
            pl.BlockSpec((1, seq, w), lambda b, i: (b, 0, 0)),
            pl.BlockSpec((1, seq, 2 * w), lambda b, i: (b, 0, 0)),
            _resident(bias.shape),
        ],
        out_specs=pl.BlockSpec((1, tq, w), lambda b, i: (b, i, 0)),
        compiler_params=_params(("parallel", "arbitrary")),
        name="attention",
    )(q, k, vx, bias)


def _filter_kernel(w1t_ref, w1c_ref, w1s_ref, b1_ref, fq_ref, wi_ref, bi_ref, w3_ref,
                   a_ref, d_ref, cb_ref, sb_ref, h_ref, *, seq, tl):
    i = pl.program_id(0)
    rows = tl + 2 * SUBLANES
    half = rows // 2
    band = lax.broadcasted_iota(jnp.int32, (rows, LANES), 1)
    freqs = jnp.where(band < FILTER_BANDS,
                      1e-4 + band.astype(F32) * ((FILTER_BANDS - 1 - 1e-4) / (FILTER_BANDS - 1)), 0.0)
    rad = freqs * (2.0 * math.pi / seq)
    local = lax.broadcasted_iota(jnp.int32, (rows, LANES), 0)

    @pl.when(i == 0)
    def _():
        cb_ref[...] = jnp.cos(local.astype(F32) * rad)
        sb_ref[...] = jnp.sin(local.astype(F32) * rad)

    base = (i * tl).astype(F32) * rad[0:SUBLANES]
    ca = jnp.cos(base)[0:1]
    sa = jnp.sin(base)[0:1]
    cos_ang = ca * cb_ref[...] - sa * sb_ref[...]
    sin_ang = sa * cb_ref[...] + ca * sb_ref[...]
    t = (local + i * tl).astype(F32) * (1.0 / (seq - 1))
    fq = fq_ref[...]
    dot = functools.partial(jnp.dot, precision=HIGHEST, preferred_element_type=F32)
    side = lambda x: jnp.concatenate([x[:half], x[half:]], axis=1)
    left = lax.broadcasted_iota(jnp.int32, (half, LANES), 1) < FILTER_HIDDEN
    t_packed = jnp.where(left, t[:half], t[half:])
    pre = (t_packed * w1t_ref[...] + dot(side(cos_ang), w1c_ref[...]) - dot(side(sin_ang), w1s_ref[...])
           + b1_ref[...])
    h = jnp.sin(fq * pre)
    for n in range(FILTER_INNER):
        h = jnp.sin(fq * (dot(h, wi_ref[n]) + bi_ref[n]))
    chan = lax.broadcasted_iota(jnp.int32, (half, HYENA_WIDTH), 1).astype(F32)
    deltas = jnp.abs(MIN_DECAY + chan * ((MAX_DECAY - MIN_DECAY) / (HYENA_WIDTH - 1)))
    for part in range(2):
        rws = slice(part * half, (part + 1) * half)
        pos = lax.broadcasted_iota(jnp.int32, (half, HYENA_WIDTH), 0) + (i * tl + part * half)
        decay = jnp.exp(-(pos.astype(F32) * (1.0 / (seq - 1))) * deltas)
        for n in range(2):
            cols = slice(n * HYENA_WIDTH, (n + 1) * HYENA_WIDTH)
            h_ref[rws, cols] = dot(h, w3_ref[part, :, cols]) * decay
    h_fwd = h_ref[0:tl, 0:HYENA_WIDTH]
    lag = lax.broadcasted_iota(jnp.int32, (tl, HYENA_WIDTH), 0) + (i * tl + 1)
    h_bwd = jnp.where(lag < seq, h_ref[pl.ds(1, tl), HYENA_WIDTH:2 * HYENA_WIDTH], 0.0)
    a_ref[...] = h_fwd + h_bwd
    d_ref[...] = h_bwd - h_fwd


def _filter_taps(seq, w1, b1, freq, w_inner, b_inner, w3, tl=512):
    pad = LANES - FILTER_BANDS
    twice = lambda v: jnp.tile(v.reshape(1, -1), (1, 2))
    blockdiag = lambda w: jnp.kron(jnp.eye(2, dtype=w.dtype), w)
    w1c = blockdiag(jnp.pad(w1[1:1 + FILTER_BANDS], ((0, pad), (0, 0))))
    w1s = blockdiag(jnp.pad(w1[1 + FILTER_BANDS:], ((0, pad), (0, 0))))
    zero = jnp.zeros_like(w3)
    args = (twice(w1[0]), w1c, w1s, twice(b1), twice(freq),
            jnp.stack([blockdiag(w_inner[n]) for n in range(FILTER_INNER)]),
            jnp.stack([twice(b_inner[n]) for n in range(FILTER_INNER)]),
            jnp.stack([jnp.concatenate([w3, zero]), jnp.concatenate([zero, w3])]))
    out = jax.ShapeDtypeStruct((seq, HYENA_WIDTH), F32)
    ospec = pl.BlockSpec((tl, HYENA_WIDTH), lambda i: (i, 0))
    return pl.pallas_call(
        functools.partial(_filter_kernel, seq=seq, tl=tl),
        out_shape=(out, out),
        grid=(seq // tl,),
        in_specs=[_resident(a.shape) for a in args],
        out_specs=(ospec, ospec),
        scratch_shapes=[pltpu.VMEM((tl + 2 * SUBLANES, LANES), F32),
                        pltpu.VMEM((tl + 2 * SUBLANES, LANES), F32),
                        pltpu.VMEM((tl + 2 * SUBLANES, 2 * HYENA_WIDTH), F32)],
        compiler_params=_params(("arbitrary",)),
        name="hyena_filter",
    )(*args)


P_BLK = 512
F_HALF = P_BLK // 2


def _pair_coefficients(seq):
    s1n = seq // P_BLK
    g = np.arange(s1n)[:, None]
    s = np.arange(s1n)[None, :]
    ang = np.pi * (2 * g + 1) * s / (2 * s1n)
    return jnp.asarray(np.cos(ang), F32), jnp.asarray(np.sin(ang), F32)


def _dft_kernel(t_ref, tt_ref, cb_ref, sb_ref, *, seq):
    g = pl.program_id(0)
    s1n = seq // P_BLK

    @pl.when(g == 0)
    def _():
        f2 = lax.broadcasted_iota(jnp.int32, (F_HALF, P_BLK), 0)
        s_odd = 2 * lax.broadcasted_iota(jnp.int32, (F_HALF, P_BLK), 1) + 1
        beta = ((f2 * s_odd) & (2 * P_BLK - 1)).astype(F32) * (math.pi / P_BLK)
        cb_ref[...] = jnp.cos(beta)
        sb_ref[...] = jnp.sin(beta)

    cb = cb_ref[...]
    sb = sb_ref[...]
    s_odd = 2 * lax.broadcasted_iota(jnp.int32, (SUBLANES, P_BLK), 1) + 1

    def block(f1):
        alpha = ((s_odd * (2 * f1 + 1)) & (8 * seq - 1)).astype(F32) * (math.pi / (4 * seq))
        ca = jnp.cos(alpha)[0:1]
        sa = jnp.sin(alpha)[0:1]
        return ca * cb - sa * sb, sa * cb + ca * sb

    mra, mia = block(g)
    mrb, mib = block(2 * s1n - 1 - g)
    for r, (left, right) in enumerate(((mra, -mia), (mia, mra), (mrb, mib), (mib, -mrb))):
        rows = slice(r * F_HALF, (r + 1) * F_HALF)
        t_ref[0, rows, :P_BLK] = left.astype(BF16)
        t_ref[0, rows, P_BLK:] = right.astype(BF16)
        tt_ref[0, :P_BLK, rows] = left.T.astype(BF16)
        tt_ref[0, P_BLK:, rows] = right.T.astype(BF16)


def _dft_matrices(seq):
    s1n = seq // P_BLK
    mat = jax.ShapeDtypeStruct((s1n, 2 * P_BLK, 2 * P_BLK), BF16)
    spec = pl.BlockSpec((1, 2 * P_BLK, 2 * P_BLK), lambda g: (g, 0, 0))
    return pl.pallas_call(
        functools.partial(_dft_kernel, seq=seq),
        out_shape=(mat, mat),
        grid=(s1n,),
        out_specs=(spec, spec),
        scratch_shapes=[pltpu.VMEM((F_HALF, P_BLK), F32), pltpu.VMEM((F_HALF, P_BLK), F32)],
        compiler_params=_params(("arbitrary",)),
        name="dft_matrices",
    )()


_SMEM = pl.BlockSpec(memory_space=pltpu.SMEM)
_COL_HALVES = (slice(0, HYENA_WIDTH // 2), slice(HYENA_WIDTH // 2, HYENA_WIDTH))


ROW_CHUNK = 64


def _stacked_blocks(cr_ref, sr_ref, g, block, s1n):
    re, im = [], []
    for r in range(0, P_BLK, ROW_CHUNK):
        ar = block(0, r)
        bi = None
        for s in range(1, s1n):
            b = block(s, r)
            ar = ar + cr_ref[g, s] * b
            bi = sr_ref[g, s] * b if bi is None else bi + sr_ref[g, s] * b
        re.append(ar)
        im.append(bi)
    return jnp.concatenate(re + im, axis=0).astype(BF16)


def _spectrum_kernel(cr_ref, sr_ref, t_ref, a_ref, d_ref, g_ref, *, seq):
    g = pl.program_id(0)
    s1n = seq // P_BLK
    time_block = lambda ref: (lambda s, r: ref[s * P_BLK + r:s * P_BLK + r + ROW_CHUNK, :])
    t_re = jnp.concatenate([t_ref[0, 0:F_HALF], t_ref[0, P_BLK:P_BLK + F_HALF]], axis=0)
    t_im = jnp.concatenate([t_ref[0, F_HALF:P_BLK], t_ref[0, P_BLK + F_HALF:2 * P_BLK]], axis=0)
    ka = jnp.dot(t_re, _stacked_blocks(cr_ref, sr_ref, g, time_block(a_ref), s1n),
                 preferred_element_type=F32)
    kd = jnp.dot(t_im, _stacked_blocks(cr_ref, sr_ref, g, time_block(d_ref), s1n),
                 preferred_element_type=F32)
    f2 = lax.broadcasted_iota(jnp.int32, (F_HALF, LANES), 0)
    reps = HYENA_WIDTH // LANES
    for blk, f1 in ((0, g), (1, 2 * s1n - 1 - g)):
        lo = blk * P_BLK
        phi = (2 * (f1 + 2 * s1n * f2) + 1).astype(F32) * (math.pi / (4 * seq))
        c = pltpu.repeat(jnp.cos(phi), reps, axis=1) * (1.0 / seq)
        s = pltpu.repeat(jnp.sin(phi), reps, axis=1) * (1.0 / seq)
        for r in range(0, F_HALF, ROW_CHUNK):
            rows = slice(blk * F_HALF + r, blk * F_HALF + r + ROW_CHUNK)
            cc, ss = c[r:r + ROW_CHUNK], s[r:r + ROW_CHUNK]
            g_ref[0, lo + r:lo + r + ROW_CHUNK] = cc * ka[rows] - ss * kd[rows]
            g_ref[0, lo + F_HALF + r:lo + F_HALF + r + ROW_CHUNK] = cc * kd[rows] + ss * ka[rows]


def _filter_spectrum(cr, sr, t, a, d):
    seq = a.shape[0]
    s1n = seq // P_BLK
    return pl.pallas_call(
        functools.partial(_spectrum_kernel, seq=seq),
        out_shape=jax.ShapeDtypeStruct((s1n, 2 * P_BLK, HYENA_WIDTH), F32),
        grid=(s1n,),
        in_specs=[_SMEM, _SMEM, pl.BlockSpec((1, 2 * P_BLK, 2 * P_BLK), lambda g: (g, 0, 0)),
                  _resident(a.shape), _resident(d.shape)],
        out_specs=pl.BlockSpec((1, 2 * P_BLK, HYENA_WIDTH), lambda g: (g, 0, 0)),
        compiler_params=_params(("parallel",)),
        name="hyena_spectrum",
    )(cr, sr, t, a, d)


def _hy_fwd_kernel(cr_ref, sr_ref, t_ref, g_ref, z_ref, y_ref, *, s1n, nb):
    g = pl.program_id(0)
    for bi in range(nb):
        for cols in _COL_HALVES:
            x = _stacked_blocks(
                cr_ref, sr_ref, g,
                lambda s, r: z_ref[bi, s * P_BLK + r:s * P_BLK + r + ROW_CHUNK, cols].astype(F32), s1n)
            res = jnp.dot(t_ref[0], x, preferred_element_type=F32)
            for lo in range(0, 2 * P_BLK, P_BLK):
                for r in range(lo, lo + F_HALF, ROW_CHUNK):
                    re_rows = slice(r, r + ROW_CHUNK)
                    im_rows = slice(r + F_HALF, r + F_HALF + ROW_CHUNK)
                    zr, wi = res[re_rows], res[im_rows]
                    gr, gi = g_ref[0, re_rows, cols], g_ref[0, im_rows, cols]
                    y_ref[bi, 0, re_rows, cols] = (gr * zr + gi * wi).astype(BF16)
                    y_ref[bi, 0, im_rows, cols] = (gr * wi - gi * zr).astype(BF16)


def _hy_inv_kernel(cr_ref, sr_ref, tt_ref, y_ref, z_ref, g0_ref, fb_ref, o_ref,
                   acc_ref, zs_ref, g0s_ref, *, s1n, nb):
    g = pl.program_id(1)

    @pl.when(g == 0)
    def _():
        acc_ref[...] = jnp.zeros_like(acc_ref)

    tblk = pl.ds(pl.multiple_of(g * P_BLK, P_BLK), P_BLK)
    zs_ref[:, tblk, :] = z_ref[...]
    g0s_ref[:, tblk, :] = g0_ref[...]
    for bi in range(nb):
        for cols in _COL_HALVES:
            uv = jnp.dot(tt_ref[g], y_ref[bi, 0, :, cols], preferred_element_type=F32)
            for r in range(0, P_BLK, ROW_CHUNK):
                u = uv[r:r + ROW_CHUNK]
                vn = uv[P_BLK + r:P_BLK + r + ROW_CHUNK]
                acc_ref[bi, r:r + ROW_CHUNK, cols] += u
                for t1 in range(1, s1n):
                    rows = slice(t1 * P_BLK + r, t1 * P_BLK + r + ROW_CHUNK)
                    acc_ref[bi, rows, cols] += cr_ref[g, t1] * u + sr_ref[g, t1] * vn

    @pl.when(g == s1n - 1)
    def _():
        for bi in range(nb):
            for r in range(0, s1n * P_BLK, ROW_CHUNK):
                rows = slice(r, r + ROW_CHUNK)
                y = ((acc_ref[bi, rows, :] + zs_ref[bi, rows, :].astype(F32) * fb_ref[...])
                     * g0s_ref[bi, rows, :].astype(F32))
                o_ref[bi, rows, :] = y.astype(BF16)


_INV_STATE_BUDGET = 28 * 1024 * 1024


def _hyena_conv(zg, g0, cr, sr, t, tt, gspec, fbias):
    bsz, seq, w = zg.shape
    s1n = seq // P_BLK
    nb = max(n for n in (4, 2, 1) if bsz % n == 0)
    mat = pl.BlockSpec((1, 2 * P_BLK, 2 * P_BLK), lambda g, b: (g, 0, 0))
    spec = pl.BlockSpec((1, 2 * P_BLK, w), lambda g, b: (g, 0, 0))
    y = pl.pallas_call(
        functools.partial(_hy_fwd_kernel, s1n=s1n, nb=nb),
        out_shape=jax.ShapeDtypeStruct((bsz, s1n, 2 * P_BLK, w), BF16),
        grid=(s1n, bsz // nb),
        in_specs=[_SMEM, _SMEM, mat, spec, pl.BlockSpec((nb, seq, w), lambda g, b: (b, 0, 0))],
        out_specs=pl.BlockSpec((nb, 1, 2 * P_BLK, w), lambda g, b: (b, g, 0, 0)),
        compiler_params=_params(("parallel", "parallel")),
        name="hyena_fwd",
    )(cr, sr, t, gspec, zg)
    state_bytes = seq * w * (4 + 2 + 2 + 2 * 2)
    nb = 2 if bsz % 2 == 0 and 2 * state_bytes <= _INV_STATE_BUDGET else 1
    blk = pl.BlockSpec((nb, P_BLK, w), lambda b, g: (b, g, 0))
    return pl.pallas_call(
        functools.partial(_hy_inv_kernel, s1n=s1n, nb=nb),
        out_shape=jax.ShapeDtypeStruct((bsz, seq, w), BF16),
        grid=(bsz // nb, s1n),
        in_specs=[_SMEM, _SMEM, _resident(tt.shape),
                  pl.BlockSpec((nb, 1, 2 * P_BLK, w), lambda b, g: (b, g, 0, 0)),
                  blk, blk, _resident(fbias.shape)],
        out_specs=pl.BlockSpec((nb, seq, w), lambda b, g: (b, 0, 0)),
        scratch_shapes=[pltpu.VMEM((nb, seq, w), F32), pltpu.VMEM((nb, seq, w), BF16),
                        pltpu.VMEM((nb, seq, w), BF16)],
        compiler_params=_params(("parallel", "arbitrary")),
        name="hyena_inv",
    )(cr, sr, tt, y, zg, g0, fbias)


MERGE_TILE = 2048
MERGE_ROW_CHUNKS = 8


def _merge_kernel(at_ref, hy_ref, x_ref, wa_ref, wh_ref, ga_ref, gh_ref, g_ref, b_ref, o_ref):
    tl = x_ref.shape[1]
    chunk = tl // MERGE_ROW_CHUNKS
    for lo in range(0, tl, chunk):
        rows = slice(lo, lo + chunk)
        mh = _rms_norm(hy_ref[0, rows].astype(F32), gh_ref[...]).astype(BF16)
        ma = _rms_norm(at_ref[0, rows].astype(F32), ga_ref[...]).astype(BF16)
        y = (jnp.dot(ma, wa_ref[...], preferred_element_type=F32)
             + jnp.dot(mh, wh_ref[...], preferred_element_type=F32))
        o_ref[0, rows] = _layer_norm(ALPHA * x_ref[0, rows] + y, g_ref[...], b_ref[...])


def _merge(at, hy, g_attn, g_hyena, x, w_out, g, b, tl):
    bsz, seq, d = x.shape
    wa, wh = w_out[:ATTN_WIDTH], w_out[ATTN_WIDTH:]
    half = pl.BlockSpec((1, tl, ATTN_WIDTH), lambda bb, i: (bb, i, 0))
    full = pl.BlockSpec((1, tl, d), lambda bb, i: (bb, i, 0))
    return pl.pallas_call(
        _merge_kernel,
        out_shape=jax.ShapeDtypeStruct(x.shape, F32),
        grid=(bsz, seq // tl),
        in_specs=[half, half, full, _resident(wa.shape), _resident(wh.shape),
                  _resident(g_attn.shape), _resident(g_hyena.shape), _resident(g.shape), _resident(b.shape)],
        out_specs=full,
        compiler_params=_params(("parallel", "parallel")),
        name="merge_ln1",
    )(at, hy, x, wa, wh, g_attn, g_hyena, g, b)


FF_CHUNK = 256
N_FF_CHUNKS = D_FF // FF_CHUNK


def _ffn_kernel(xp_ref, x_ref, xn_ref, wi_ref, cw_ref, cb_ref, wo_ref, g_ref, b_ref,
                o_ref, xs_ref, hid_ref, os_ref, *, tl):
    _fill_halo_slabs(xs_ref, xp_ref, x_ref, xn_ref, tl)
    xh = _load_interleaved(xs_ref).astype(BF16)
    for j in range(N_FF_CHUNKS):
        conv = []
        for part in range(2):
            cols = slice(part * D_FF + j * FF_CHUNK, part * D_FF + (j + 1) * FF_CHUNK)
            u = jnp.dot(xh, wi_ref[:, cols], preferred_element_type=F32)
            conv.append(_dwconv3_interleaved(u, cw_ref[:, cols], cb_ref[:, cols]))
        gate = conv[1]
        gelu = 0.5 * gate * (1.0 + lax.erf(gate * (2.0 ** -0.5)))
        hid_ref[:, j * FF_CHUNK:(j + 1) * FF_CHUNK] = (conv[0] * gelu).astype(BF16)
    y = jnp.dot(hid_ref[...], wo_ref[...], preferred_element_type=F32)
    out = _layer_norm(ALPHA * _load_interleaved(xs_ref) + y, g_ref[...], b_ref[...])
    _store_natural(os_ref, 0, out)
    for k in range(os_ref.shape[0]):
        o_ref[0, :, k * LANES:(k + 1) * LANES] = os_ref[k, HALO:HALO + tl, :]


def _ffn(x1, w_in, conv_w, conv_b, w_out, g, b, tl):
    bsz, seq, d = x1.shape
    cb = conv_b.reshape(1, -1)
    return pl.pallas_call(
        functools.partial(_ffn_kernel, tl=tl),
        out_shape=jax.ShapeDtypeStruct(x1.shape, F32),
        grid=(bsz, seq // tl),
        in_specs=_halo_specs(tl, seq, d) + [_resident(w_in.shape), _resident(conv_w.shape), _resident(cb.shape),
                                            _resident(w_out.shape), _resident(g.shape), _resident(b.shape)],
        out_specs=pl.BlockSpec((1, tl, d), lambda bb, i: (bb, i, 0)),
        scratch_shapes=[pltpu.VMEM((d // LANES, tl + 2 * HALO, LANES), F32),
                        pltpu.VMEM((tl + 2 * HALO, D_FF), BF16),
                        pltpu.VMEM((d // LANES, tl + 2 * HALO, LANES), F32)],
        compiler_params=_params(("parallel", "parallel")),
        name="conv_ffn",
    )(x1, x1, x1, w_in, conv_w, cb, w_out, g, b)


def _encoder_layer(x, p, tl=1024, rb=32):
    seq = x.shape[1]
    rows = seq // GRID_W
    assert seq % tl == 0 and seq % P_BLK == 0 and seq % GRID_W == 0
    assert rows >= NA_ROWS and rows % rb == 0
    q, k, v, zg, g0 = _in_proj(x, p["w_in"], p["short_w"], p["short_b"], tl)
    at = _attention(q, k, v, p["bias"], rb)
    a, d = _filter_taps(seq, p["filt_w1"], p["filt_b1"], p["filt_freq"], p["filt_w_inner"],
                        p["filt_b_inner"], p["filt_w3"])
    cr, sr = _pair_coefficients(seq)
    t, tt = _dft_matrices(seq)
    gspec = _filter_spectrum(cr, sr, t, a, d)
    hy = _hyena_conv(zg, g0, cr, sr, t, tt, gspec, p["filt_bias"])
    x1 = _merge(at, hy, p["g_attn"], p["g_hyena"], x, p["w_out"], p["ln1_g"], p["ln1_b"],
                min(MERGE_TILE, seq))
    return _ffn(x1, p["ffn_w_in"], p["ffn_conv_w"], p["ffn_conv_b"], p["ffn_w_out"],
                p["ln2_g"], p["ln2_b"], tl)


def kernel(x_prompt, x_sample, w_in, short_w, short_b, rpb, filt_w1, filt_b1, filt_freq, filt_w_inner,
           filt_b_inner, filt_w3, filt_bias, g_attn, g_hyena, w_out, ln1_g, ln1_b, ffn_w_in, ffn_conv_w,
           ffn_conv_b, ffn_w_out, ln2_g, ln2_b):
    assert w_in.shape[0] == DEPTH == 1
    row = lambda a: a[0].reshape(1, -1)
    p = dict(
        w_in=w_in[0].astype(BF16), short_w=short_w[0], short_b=short_b[0],
        bias=_attention_bias(rpb[0]), g_attn=row(g_attn),
        filt_w1=filt_w1[0], filt_b1=filt_b1[0], filt_freq=filt_freq[0], filt_w_inner=filt_w_inner[0],
        filt_b_inner=filt_b_inner[0], filt_w3=filt_w3[0], filt_bias=filt_bias[0], g_hyena=row(g_hyena),
        w_out=w_out[0].astype(BF16), ln1_g=row(ln1_g), ln1_b=row(ln1_b),
        ffn_w_in=ffn_w_in[0].astype(BF16), ffn_conv_w=ffn_conv_w[0], ffn_conv_b=ffn_conv_b[0],
        ffn_w_out=ffn_w_out[0].astype(BF16), ln2_g=row(ln2_g), ln2_b=row(ln2_b),
    )
    return (_encoder_layer(x_prompt, p), _encoder_layer(x_sample, p))
```
